```python
import math
import jax, jax.numpy as jnp
from jax import lax
import numpy as np

D_MODEL = 1024
BATCH = 8
SEQ = 2048
DEPTH = 4
DEC_BATCH = 128
DEC_SEQ = 1
PAST_LEN = 16384
PAGE_SIZE = 128

F32 = jnp.float32
N_AB = (DEPTH + 1) // 2
N_CD = DEPTH // 2
CONV_W = 4
CHUNK = 64
LRU_W = D_MODEL // 2
LRU_BLOCKS = 8
LRU_BW = LRU_W // LRU_BLOCKS
LRU_C = 8.0
GDN_H = D_MODEL // 256
GDN_DK = 128
GDN_DV = 128
GDN_QK = GDN_H * GDN_DK
GDN_W = GDN_H * GDN_DV
GDN_QKV = 2 * GDN_QK + GDN_W
AB_SIZES = (LRU_W, LRU_W, GDN_QKV, GDN_H, GDN_H, GDN_W)
AB_IN = sum(AB_SIZES)
AB_OUT = LRU_W + GDN_W
RWKV_HD = 64
RWKV_H = D_MODEL // 2 // RWKV_HD
RWKV_W = RWKV_H * RWKV_HD
RWKV_RW = 64
RWKV_RA = 64
RWKV_RG = 128
RWKV_SIZES = (RWKV_W, RWKV_W, RWKV_W, RWKV_RW, RWKV_RA, RWKV_RG)
RWKV_PROJ_W = sum(RWKV_SIZES)
RWKV_GN_EPS = 64e-5
MLSTM_H = D_MODEL // 256
MLSTM_DK = 128
MLSTM_DV = 128
MLSTM_QK = MLSTM_H * MLSTM_DK
MLSTM_W = MLSTM_H * MLSTM_DV
MLSTM_SIZES = (MLSTM_QK, MLSTM_QK, MLSTM_W, MLSTM_H, MLSTM_H, MLSTM_W)
CD_IN = RWKV_PROJ_W + sum(MLSTM_SIZES)
CD_OUT = RWKV_W + MLSTM_W
D_FF = 7 * D_MODEL // 2
N_EXPERTS = 8
TOP_K = 2
D_FF_EXPERT = 7 * D_MODEL // 2
LN_EPS = 1e-5
NEG_BIG = -1e30
DN_ALPHA = (2.0 * DEPTH) ** 0.25
DN_BETA = (8.0 * DEPTH) ** -0.25

kernel_name = 'hybrid_rglru_gdn_rwkv7_mlstm_step'


def _split(x, sizes):
    out, o = [], 0
    for s in sizes:
        out.append(x[..., o:o + s])
        o += s
    return out


def _ln(x, g=None, b=None, eps=LN_EPS):
    xf = x.astype(F32)
    mu = jnp.mean(xf, -1, keepdims=True)
    var = jnp.mean(jnp.square(xf - mu), -1, keepdims=True)
    y = (xf - mu) * lax.rsqrt(var + eps)
    if g is not None:
        y = y * g
    if b is not None:
        y = y + b
    return y


def _rms_norm(x, g, eps=1e-6):
    xf = x.astype(F32)
    return xf * lax.rsqrt(jnp.mean(xf * xf, -1, keepdims=True) + eps) * g


def _l2norm(x, eps=1e-6):
    xf = x.astype(F32)
    return xf * lax.rsqrt(jnp.sum(xf * xf, -1, keepdims=True) + eps)


def _causal_conv(u, buf, w, b=None):
    L = u.shape[1]
    full = jnp.concatenate([buf.astype(u.dtype), u], axis=1)
    out = full[:, 0:L] * w[0]
    for j in range(1, CONV_W):
        out = out + full[:, j:j + L] * w[j]
    if b is not None:
        out = out + b
    return out, full[:, L:]


def _chunk_dims(L):
    c = min(CHUNK, L)
    n = -(-L // c)
    return c, n, n * c - L


def _to_blocks(t, n, c):
    t = t.reshape((t.shape[0], n, c) + t.shape[2:])
    return jnp.moveaxis(t, (1, 3), (0, 2))


def _from_blocks(o, L):
    o = jnp.moveaxis(o, (0, 2), (1, 3))
    return o.reshape((o.shape[0], o.shape[1] * o.shape[2]) + o.shape[3:])[:, :L]


def _rglru(x, h0, wr, br, wi, bi, lam, is_first):
    B, L, W = x.shape
    xf = x.astype(F32)
    xb = xf.reshape(B, L, LRU_BLOCKS, LRU_BW)
    r = jax.nn.sigmoid(jnp.einsum('blni,nij->blnj', xb, wr.astype(F32)).reshape(B, L, W) + br)
    i = jax.nn.sigmoid(jnp.einsum('blni,nij->blnj', xb, wi.astype(F32)).reshape(B, L, W) + bi)
    log_a = -LRU_C * r * jax.nn.softplus(-lam.astype(F32))
    a = jnp.exp(log_a)
    mult = jnp.where(is_first[None, :, None], 1.0, jnp.sqrt(-jnp.expm1(2.0 * log_a)))
    b = xf * i * mult
    b = b.at[:, 0].add(a[:, 0] * h0.astype(F32))
    combine = lambda p, q: (p[0] * q[0], q[0] * p[1] + q[1])
    _, h = lax.associative_scan(combine, (a, b), axis=1)
    return h, h[:, -1]


def _gdn(q, k, v, g, beta, S0):
    B, L, H, DK = q.shape
    DV = v.shape[-1]
    c, n, pad = _chunk_dims(L)
    q = q.astype(F32) * (DK ** -0.5)
    k, v, g, beta = k.astype(F32), v.astype(F32), g.astype(F32), beta.astype(F32)
    if pad:
        pw = ((0, 0), (0, pad), (0, 0), (0, 0))
        q, k, v = jnp.pad(q, pw), jnp.pad(k, pw), jnp.pad(v, pw)
        g, beta = jnp.pad(g, pw[:3]), jnp.pad(beta, pw[:3])
    q, k, v, g, beta = (_to_blocks(t, n, c) for t in (q, k, v, g, beta))
    gc = jnp.cumsum(g, axis=-1)
    idx = jnp.arange(c)
    incl = idx[:, None] >= idx[None, :]
    strict = idx[:, None] > idx[None, :]
    decay = jnp.exp(jnp.where(incl, gc[..., :, None] - gc[..., None, :], -jnp.inf))
    kb = k * beta[..., None]
    A = jnp.where(strict, jnp.einsum('nbhik,nbhjk->nbhij', kb, k) * decay, 0.0)
    rhs = jnp.concatenate([v * beta[..., None], kb * jnp.exp(gc)[..., None]], axis=-1)
    sol = lax.linalg.triangular_solve(A + jnp.eye(c, dtype=F32), rhs, left_side=True, lower=True)
    u, w = sol[..., :DV], sol[..., DV:]
    qk = jnp.where(incl, jnp.einsum('nbhik,nbhjk->nbhij', q, k) * decay, 0.0)

    def step(S, blk):
        q_i, k_i, u_i, w_i, gc_i, qk_i = blk
        v_new = u_i - jnp.einsum('bhck,bhkv->bhcv', w_i, S)
        o = (jnp.einsum('bhck,bhkv->bhcv', q_i * jnp.exp(gc_i)[..., None], S)
             + jnp.einsum('bhij,bhjv->bhiv', qk_i, v_new))
        g_last = gc_i[..., -1]
        S = (S * jnp.exp(g_last)[..., None, None]
             + jnp.einsum('bhck,bhcv->bhkv', k_i * jnp.exp(g_last[..., None] - gc_i)[..., None], v_new))
        return S, o

    S, o = lax.scan(step, S0.astype(F32), (q, k, u, w, gc, qk))
    return _from_blocks(o, L), S


def _rwkv7(r, logw, k, v, kk, a, S0):
    seq = tuple(jnp.moveaxis(t.astype(F32), 1, 0) for t in (r, logw, k, v, kk, a))

    def step(S, inp):
        r_t, lw_t, k_t, v_t, kk_t, a_t = inp
        sa = jnp.einsum('bhvk,bhk->bhv', S, -kk_t)
        S = (S * jnp.exp(lw_t)[:, :, None, :] + sa[..., None] * (kk_t * a_t)[:, :, None, :]
             + v_t[..., None] * k_t[:, :, None, :])
        return S, jnp.einsum('bhvk,bhk->bhv', S, r_t)

    S, y = lax.scan(step, S0.astype(F32), seq)
    return jnp.moveaxis(y, 0, 1), S


def _mlstm(q, k, v, log_i, log_f, C0, n0, m0):
    B, L, H, DK = q.shape
    c, n, pad = _chunk_dims(L)
    q, v = q.astype(F32), v.astype(F32)
    k = k.astype(F32) * (DK ** -0.5)
    log_i, log_f = log_i.astype(F32), log_f.astype(F32)
    if pad:
        pw = ((0, 0), (0, pad), (0, 0), (0, 0))
        q, k, v = jnp.pad(q, pw), jnp.pad(k, pw), jnp.pad(v, pw)
        log_f = jnp.pad(log_f, pw[:3])
        log_i = jnp.pad(log_i, pw[:3], constant_values=NEG_BIG)
    q, k, v, log_i, log_f = (_to_blocks(t, n, c) for t in (q, k, v, log_i, log_f))
    bcum = jnp.cumsum(log_f, axis=-1)
    idx = jnp.arange(c)
    incl = idx[:, None] >= idx[None, :]
    Dm = jnp.where(incl, bcum[..., :, None] - bcum[..., None, :] + log_i[..., None, :], -jnp.inf)
    qk = jnp.einsum('nbhtk,nbhsk->nbhts', q, k)

    def step(carry, blk):
        C, nv, m = carry
        q_i, k_i, v_i, b_i, li_i, D_i, qk_i = blk
        m_inter = b_i + m[..., None]
        m_t = jnp.maximum(m_inter, jnp.max(D_i, axis=-1))
        w_inter = jnp.exp(m_inter - m_t)
        Sc = qk_i * jnp.exp(D_i - m_t[..., None])
        num = (w_inter[..., None] * jnp.einsum('bhvk,bhck->bhcv', C, q_i)
               + jnp.einsum('bhts,bhsv->bhtv', Sc, v_i))
        den = w_inter * jnp.einsum('bhk,bhck->bhc', nv, q_i) + jnp.sum(Sc, axis=-1)
        h = num / jnp.maximum(jnp.abs(den), jnp.exp(-m_t))[..., None]
        m_new = m_t[..., -1]
        dec = jnp.exp(b_i[..., -1] + m - m_new)
        ws = jnp.exp(b_i[..., -1:] - b_i + li_i - m_new[..., None])
        C = dec[..., None, None] * C + jnp.einsum('bhc,bhcv,bhck->bhvk', ws, v_i, k_i)
        nv = dec[..., None] * nv + jnp.einsum('bhc,bhck->bhk', ws, k_i)
        return (C, nv, m_new), h

    (C, nv, m), h = lax.scan(step, (C0.astype(F32), n0.astype(F32), m0.astype(F32)),
                             (q, k, v, bcum, log_i, Dm, qk))
    return _from_blocks(h, L), C, nv, m


def _ab_mixer(h, lru_h0, lru_buf, gdn_S0, gdn_buf, pos0,
              w_in, w_out, lru_conv_w, lru_conv_b, lru_wr, lru_br, lru_wi, lru_bi, lru_lambda,
              gdn_conv_w, gdn_a_log, gdn_dt_bias, gdn_norm_w):
    B, L, _ = h.shape
    xa, ya, qkv, alpha, beta_pre, z = _split(h @ w_in, AB_SIZES)
    xa, lru_buf_new = _causal_conv(xa, lru_buf, lru_conv_w, lru_conv_b)
    is_first = (pos0 + jnp.arange(L)) == 0
    ha, lru_h_new = _rglru(xa, lru_h0, lru_wr, lru_br, lru_wi, lru_bi, lru_lambda, is_first)
    out_a = ha.astype(h.dtype) * jax.nn.gelu(ya)
    qkv, gdn_buf_new = _causal_conv(qkv, gdn_buf, gdn_conv_w)
    q, k, v = _split(jax.nn.silu(qkv), (GDN_QK, GDN_QK, GDN_W))
    q = _l2norm(q.reshape(B, L, GDN_H, GDN_DK))
    k = _l2norm(k.reshape(B, L, GDN_H, GDN_DK))
    v = v.reshape(B, L, GDN_H, GDN_DV)
    beta = jax.nn.sigmoid(beta_pre.astype(F32))
    g = -jnp.exp(gdn_a_log.astype(F32)) * jax.nn.softplus(alpha.astype(F32) + gdn_dt_bias)
    o, gdn_S_new = _gdn(q, k, v, g, beta, gdn_S0)
    o = _rms_norm(o, gdn_norm_w) * jax.nn.silu(z.astype(F32).reshape(B, L, GDN_H, GDN_DV))
    out_b = o.reshape(B, L, GDN_W).astype(h.dtype)
    out = jnp.concatenate([out_a, out_b], axis=-1) @ w_out
    return out, lru_h_new, lru_buf_new, gdn_S_new, gdn_buf_new


def _cd_mixer(h, rwkv_S0, rwkv_shift0, mC0, mn0, mm0,
              w_in, w_out, mix, w0, w2, a0, a2, g2, k_k, k_a, r_k, ln_w, ln_b, i_b, f_b, m_norm_w):
    B, L, _ = h.shape
    p = h @ w_in
    rw, mproj = p[..., :RWKV_PROJ_W], p[..., RWKV_PROJ_W:]
    prev = jnp.concatenate([rwkv_shift0[:, None].astype(rw.dtype), rw[:, :-1]], axis=1)
    xs = rw + (prev - rw) * mix
    r, k, v, wl, al, gl = _split(xs.astype(F32), RWKV_SIZES)
    w = -jax.nn.softplus(-(w0 + jnp.tanh(wl) @ w2)) - 0.5
    a = jax.nn.sigmoid(a0 + al @ a2)
    gate = jax.nn.sigmoid(gl) @ g2
    hs = (B, L, RWKV_H, RWKV_HD)
    r, k, v, a, logw = (t.reshape(hs) for t in (r, k, v, a, -jnp.exp(w)))
    kk = _l2norm(k * k_k.reshape(RWKV_H, RWKV_HD))
    k = k * (1.0 + (a - 1.0) * k_a.reshape(RWKV_H, RWKV_HD))
    y, rwkv_S_new = _rwkv7(r, logw, k, v, kk, a, rwkv_S0)
    y = _ln(y, ln_w.reshape(RWKV_H, RWKV_HD), ln_b.reshape(RWKV_H, RWKV_HD), RWKV_GN_EPS)
    y = y + jnp.sum(r * k * r_k, axis=-1, keepdims=True) * v
    out_c = (y.reshape(B, L, RWKV_W) * gate).astype(h.dtype)
    mq, mk, mv, ig, fg, og = _split(mproj.astype(F32), MLSTM_SIZES)
    log_i = ig + i_b
    log_f = jax.nn.log_sigmoid(fg + f_b)
    hm, mC, mn, mm = _mlstm(mq.reshape(B, L, MLSTM_H, MLSTM_DK), mk.reshape(B, L, MLSTM_H, MLSTM_DK),
                            mv.reshape(B, L, MLSTM_H, MLSTM_DV), log_i, log_f, mC0, mn0, mm0)
    hm = _ln(hm, m_norm_w) * jax.nn.sigmoid(og).reshape(B, L, MLSTM_H, MLSTM_DV)
    out_d = hm.reshape(B, L, MLSTM_W).astype(h.dtype)
    out = jnp.concatenate([out_c, out_d], axis=-1) @ w_out
    return out, rwkv_S_new, rw[:, -1], mC, mn, mm


def _swiglu(x, wg, wu, wd):
    return (jax.nn.silu(x @ wg) * (x @ wu)) @ wd


def _moe(x, router_w, wg, wu, wd):
    logits = (x @ router_w).astype(F32)
    top_v, top_i = lax.top_k(logits, TOP_K)
    gates = jax.nn.softmax(top_v, axis=-1)
    comb = jnp.einsum('tk,tke->te', gates, jax.nn.one_hot(top_i, N_EXPERTS, dtype=F32)).astype(x.dtype)
    y = jnp.zeros_like(x)
    for e in range(N_EXPERTS):
        y = y + comb[:, e:e + 1] * _swiglu(x, wg[e], wu[e], wd[e])
    return y


def setup_inputs(seed: int = 0) -> dict:
    key = jax.random.key(seed)
    ks = iter(jax.random.split(key, 128))
    D = D_MODEL

    def nrm(shape, scale):
        return jax.random.normal(next(ks), shape, F32) * scale

    def unif(shape, lo, hi):
        return jax.random.uniform(next(ks), shape, F32, lo, hi)

    inp = {}
    inp['x_prompt'] = nrm((BATCH, SEQ, D), 1.0)
    inp['x_sample'] = nrm((DEC_BATCH, DEC_SEQ, D), 1.0)
    inp['c_prompt'] = nrm((BATCH, D), 1.0)
    inp['c_sample'] = nrm((DEC_BATCH, D), 1.0)
    inp['state_lru_h'] = nrm((N_AB, DEC_BATCH, LRU_W), 0.5)
    inp['state_lru_conv'] = nrm((N_AB, DEC_BATCH, CONV_W - 1, LRU_W), 1.0)
    inp['state_gdn_S'] = nrm((N_AB, DEC_BATCH, GDN_H, GDN_DK, GDN_DV), 0.1)
    inp['state_gdn_conv'] = nrm((N_AB, DEC_BATCH, CONV_W - 1, GDN_QKV), 1.0)
    inp['state_rwkv_S'] = nrm((N_CD, DEC_BATCH, RWKV_H, RWKV_HD, RWKV_HD), 0.1)
    inp['state_rwkv_shift'] = nrm((N_CD, DEC_BATCH, RWKV_PROJ_W), 1.0)
    inp['state_mlstm_C'] = nrm((N_CD, DEC_BATCH, MLSTM_H, MLSTM_DV, MLSTM_DK), 0.1)
    inp['state_mlstm_n'] = nrm((N_CD, DEC_BATCH, MLSTM_H, MLSTM_DK), 0.1)
    inp['state_mlstm_m'] = nrm((N_CD, DEC_BATCH, MLSTM_H), 1.0)
    inp['mod_w'] = nrm((DEPTH, D, 6 * D), 0.2 * D ** -0.5)
    inp['mod_b'] = nrm((DEPTH, 6 * D), 0.02)
    inp['ln1_g'] = 1.0 + nrm((DEPTH, D), 0.02)
    inp['ln1_b'] = nrm((DEPTH, D), 0.02)
    inp['ln2_g'] = 1.0 + nrm((DEPTH, D), 0.02)
    inp['ln2_b'] = nrm((DEPTH, D), 0.02)
    inp['ab_w_in'] = nrm((N_AB, D, AB_IN), D ** -0.5)
    inp['ab_w_out'] = nrm((N_AB, AB_OUT, D), DN_BETA * AB_OUT ** -0.5)
    inp['lru_conv_w'] = nrm((N_AB, CONV_W, LRU_W), CONV_W ** -0.5)
    inp['lru_conv_b'] = nrm((N_AB, LRU_W), 0.02)
    inp['lru_wr'] = nrm((N_AB, LRU_BLOCKS, LRU_BW, LRU_BW), LRU_BW ** -0.5)
    inp['lru_br'] = nrm((N_AB, LRU_W), 0.02)
    inp['lru_wi'] = nrm((N_AB, LRU_BLOCKS, LRU_BW, LRU_BW), LRU_BW ** -0.5)
    inp['lru_bi'] = nrm((N_AB, LRU_W), 0.02)
    s = unif((N_AB, LRU_W), 0.9, 0.999) ** (1.0 / LRU_C)
    inp['lru_lambda'] = jnp.log(s) - jnp.log1p(-s)
    inp['gdn_conv_w'] = nrm((N_AB, CONV_W, GDN_QKV), CONV_W ** -0.5)
    inp['gdn_a_log'] = jnp.log(unif((N_AB, GDN_H), 1.0, 16.0))
    dt = jnp.exp(unif((N_AB, GDN_H), math.log(1e-3), math.log(1e-1)))
    inp['gdn_dt_bias'] = dt + jnp.log(-jnp.expm1(-dt))
    inp['gdn_norm_w'] = 1.0 + nrm((N_AB, GDN_DV), 0.02)
    inp['cd_w_in'] = nrm((N_CD, D, CD_IN), D ** -0.5)
    inp['cd_w_out'] = nrm((N_CD, CD_OUT, D), DN_BETA * CD_OUT ** -0.5)
    inp['rwkv_mix'] = unif((N_CD, RWKV_PROJ_W), 0.0, 1.0)
    inp['rwkv_w0'] = unif((N_CD, RWKV_W), -6.0, 1.0)
    inp['rwkv_w2'] = nrm((N_CD, RWKV_RW, RWKV_W), 0.5 * RWKV_RW ** -0.5)
    inp['rwkv_a0'] = nrm((N_CD, RWKV_W), 0.1)
    inp['rwkv_a2'] = nrm((N_CD, RWKV_RA, RWKV_W), RWKV_RA ** -0.5)
    inp['rwkv_g2'] = nrm((N_CD, RWKV_RG, RWKV_W), RWKV_RG ** -0.5)
    inp['rwkv_k_k'] = 0.85 + nrm((N_CD, RWKV_W), 0.02)
    inp['rwkv_k_a'] = 1.0 + nrm((N_CD, RWKV_W), 0.02)
    inp['rwkv_r_k'] = nrm((N_CD, RWKV_H, RWKV_HD), 0.1)
    inp['rwkv_ln_w'] = 1.0 + nrm((N_CD, RWKV_W), 0.02)
    inp['rwkv_ln_b'] = nrm((N_CD, RWKV_W), 0.02)
    inp['mlstm_i_b'] = nrm((N_CD, MLSTM_H), 0.1)
    inp['mlstm_f_b'] = jnp.linspace(3.0, 6.0, MLSTM_H, dtype=F32)[None, :] + nrm((N_CD, MLSTM_H), 0.1)
    inp['mlstm_norm_w'] = 1.0 + nrm((N_CD, MLSTM_DV), 0.02)
    inp['ffn_w_gate'] = nrm((N_AB, D, D_FF), D ** -0.5)
    inp['ffn_w_up'] = nrm((N_AB, D, D_FF), D ** -0.5)
    inp['ffn_w_down'] = nrm((N_AB, D_FF, D), DN_BETA * D_FF ** -0.5)
    inp['router_w'] = nrm((N_CD, D, N_EXPERTS), D ** -0.5)
    inp['moe_w_gate'] = nrm((N_CD, N_EXPERTS, D, D_FF_EXPERT), D ** -0.5)
    inp['moe_w_up'] = nrm((N_CD, N_EXPERTS, D, D_FF_EXPERT), D ** -0.5)
    inp['moe_w_down'] = nrm((N_CD, N_EXPERTS, D_FF_EXPERT, D), DN_BETA * D_FF_EXPERT ** -0.5)
    return inp


def reference(x_prompt, x_sample, c_prompt, c_sample,
              state_lru_h, state_lru_conv, state_gdn_S, state_gdn_conv,
              state_rwkv_S, state_rwkv_shift, state_mlstm_C, state_mlstm_n, state_mlstm_m,
              mod_w, mod_b, ln1_g, ln1_b, ln2_g, ln2_b,
              ab_w_in, ab_w_out, lru_conv_w, lru_conv_b, lru_wr, lru_br, lru_wi, lru_bi, lru_lambda,
              gdn_conv_w, gdn_a_log, gdn_dt_bias, gdn_norm_w,
              cd_w_in, cd_w_out, rwkv_mix, rwkv_w0, rwkv_w2, rwkv_a0, rwkv_a2, rwkv_g2,
              rwkv_k_k, rwkv_k_a, rwkv_r_k, rwkv_ln_w, rwkv_ln_b,
              mlstm_i_b, mlstm_f_b, mlstm_norm_w,
              ffn_w_gate, ffn_w_up, ffn_w_down,
              router_w, moe_w_gate, moe_w_up, moe_w_down):

    def trunk(x, c, lru_h, lru_conv, gdn_S, gdn_conv, rwkv_S, rwkv_shift, mC, mn, mm, pos0):
        B, L, D = x.shape
        n_lru_h, n_lru_conv, n_gdn_S, n_gdn_conv = [], [], [], []
        n_rwkv_S, n_rwkv_shift, n_mC, n_mn, n_mm = [], [], [], [], []
        cs = jax.nn.silu(c)
        for l in range(DEPTH):
            j = l // 2
            mod = (cs @ mod_w[l] + mod_b[l])[:, None, :]
            sh1, sc1, g1, sh2, sc2, g2 = _split(mod, (D,) * 6)
            h = x * (1.0 + sc1) + sh1
            if l % 2 == 0:
                mix, s0, s1, s2, s3 = _ab_mixer(
                    h, lru_h[j], lru_conv[j], gdn_S[j], gdn_conv[j], pos0,
                    ab_w_in[j], ab_w_out[j], lru_conv_w[j], lru_conv_b[j], lru_wr[j], lru_br[j],
                    lru_wi[j], lru_bi[j], lru_lambda[j], gdn_conv_w[j], gdn_a_log[j],
                    gdn_dt_bias[j], gdn_norm_w[j])
                n_lru_h.append(s0.astype(x.dtype))
                n_lru_conv.append(s1.astype(x.dtype))
                n_gdn_S.append(s2.astype(x.dtype))
                n_gdn_conv.append(s3.astype(x.dtype))
            else:
                mix, s0, s1, s2, s3, s4 = _cd_mixer(
                    h, rwkv_S[j], rwkv_shift[j], mC[j], mn[j], mm[j],
                    cd_w_in[j], cd_w_out[j], rwkv_mix[j], rwkv_w0[j], rwkv_w2[j], rwkv_a0[j],
                    rwkv_a2[j], rwkv_g2[j], rwkv_k_k[j], rwkv_k_a[j], rwkv_r_k[j], rwkv_ln_w[j],
                    rwkv_ln_b[j], mlstm_i_b[j], mlstm_f_b[j], mlstm_norm_w[j])
                n_rwkv_S.append(s0.astype(x.dtype))
                n_rwkv_shift.append(s1.astype(x.dtype))
                n_mC.append(s2.astype(x.dtype))
                n_mn.append(s3.astype(x.dtype))
                n_mm.append(s4.astype(x.dtype))
            x = _ln(DN_ALPHA * x + (1.0 + g1) * mix, ln1_g[l], ln1_b[l]).astype(x.dtype)
            h = x * (1.0 + sc2) + sh2
            if l % 2 == 0:
                f = _swiglu(h, ffn_w_gate[j], ffn_w_up[j], ffn_w_down[j])
            else:
                f = _moe(h.reshape(B * L, D), router_w[j], moe_w_gate[j], moe_w_up[j],
                         moe_w_down[j]).reshape(B, L, D)
            x = _ln(DN_ALPHA * x + (1.0 + g2) * f, ln2_g[l], ln2_b[l]).astype(x.dtype)
        return x, (jnp.stack(n_lru_h), jnp.stack(n_lru_conv), jnp.stack(n_gdn_S), jnp.stack(n_gdn_conv),
                   jnp.stack(n_rwkv_S), jnp.stack(n_rwkv_shift), jnp.stack(n_mC), jnp.stack(n_mn),
                   jnp.stack(n_mm))

    dt = x_prompt.dtype
    Bp = x_prompt.shape[0]
    y_p, (lru_h_p, lru_conv_p, gdn_S_p, gdn_conv_p, rwkv_S_p, rwkv_shift_p, mC_p, mn_p, mm_p) = trunk(
        x_prompt, c_prompt,
        jnp.zeros((N_AB, Bp, LRU_W), dt), jnp.zeros((N_AB, Bp, CONV_W - 1, LRU_W), dt),
        jnp.zeros((N_AB, Bp, GDN_H, GDN_DK, GDN_DV), dt), jnp.zeros((N_AB, Bp, CONV_W - 1, GDN_QKV), dt),
        jnp.zeros((N_CD, Bp, RWKV_H, RWKV_HD, RWKV_HD), dt), jnp.zeros((N_CD, Bp, RWKV_PROJ_W), dt),
        jnp.zeros((N_CD, Bp, MLSTM_H, MLSTM_DV, MLSTM_DK), dt), jnp.zeros((N_CD, Bp, MLSTM_H, MLSTM_DK), dt),
        jnp.zeros((N_CD, Bp, MLSTM_H), dt), 0)
    y_s, (lru_h_s, lru_conv_s, gdn_S_s, gdn_conv_s, rwkv_S_s, rwkv_shift_s, mC_s, mn_s, mm_s) = trunk(
        x_sample, c_sample, state_lru_h, state_lru_conv, state_gdn_S, state_gdn_conv,
        state_rwkv_S, state_rwkv_shift, state_mlstm_C, state_mlstm_n, state_mlstm_m, PAST_LEN)
    return (y_p, y_s, lru_h_p, lru_h_s, lru_conv_p, lru_conv_s, gdn_S_p, gdn_S_s, gdn_conv_p, gdn_conv_s,
            rwkv_S_p, rwkv_S_s, rwkv_shift_p, rwkv_shift_s, mC_p, mC_s, mn_p, mn_s, mm_p, mm_s)
```

```python
import functools
import math

import jax
import jax.numpy as jnp
from jax import lax
from jax.experimental import pallas as pl
from jax.experimental.pallas import tpu as pltpu

F32 = jnp.float32
BF16 = jnp.bfloat16

D_MODEL = 1024
DEPTH = 4
PAST_LEN = 16384
CONV_W = 4
LRU_W = D_MODEL // 2
LRU_BLOCKS = 8
LRU_BW = LRU_W // LRU_BLOCKS
LRU_C = 8.0
GDN_H = D_MODEL // 256
GDN_DK = 128
GDN_DV = 128
GDN_QK = GDN_H * GDN_DK
GDN_W = GDN_H * GDN_DV
GDN_QKV = 2 * GDN_QK + GDN_W
RWKV_HD = 64
RWKV_H = D_MODEL // 2 // RWKV_HD
RWKV_W = RWKV_H * RWKV_HD
RWKV_RW = 64
RWKV_RA = 64
RWKV_RG = 128
RWKV_PROJ_W = 3 * RWKV_W + RWKV_RW + RWKV_RA + RWKV_RG
RWKV_GN_EPS = 64e-5
MLSTM_H = D_MODEL // 256
MLSTM_DK = 128
MLSTM_DV = 128
MLSTM_QK = MLSTM_H * MLSTM_DK
MLSTM_W = MLSTM_H * MLSTM_DV
D_FF = 7 * D_MODEL // 2
N_EXPERTS = 8
LN_EPS = 1e-5
NEG_BIG = -1e30
DN_ALPHA = (2.0 * DEPTH) ** 0.25

LANES = 128
SUBLANES = 8
SEQ_CHUNK = 64
ROW_TILE = 512
FF_TILE = 512
VMEM_LIMIT = 48 * 1024 * 1024


def _cparams(*sem):
    return pltpu.CompilerParams(dimension_semantics=sem, vmem_limit_bytes=VMEM_LIMIT)


def _dot(a, b):
    return jnp.dot(a.astype(BF16), b.astype(BF16), preferred_element_type=F32)


def _dot_nt(a, b):
    return lax.dot_general(a.astype(BF16), b.astype(BF16), (((1,), (1,)), ((), ())),
                           preferred_element_type=F32)


def _dot_tn(a, b):
    return lax.dot_general(a.astype(BF16), b.astype(BF16), (((0,), (0,)), ((), ())),
                           preferred_element_type=F32)


def _split3(x):
    hi = x.astype(BF16)
    r1 = x - hi.astype(F32)
    mid = r1.astype(BF16)
    lo = (r1 - mid.astype(F32)).astype(BF16)
    return hi, mid, lo


def _dot_sel_lhs(t, x):
    tb = jnp.where(t, 1.0, 0.0).astype(BF16)
    hi, mid, lo = _split3(x)
    d = lambda p: jnp.dot(tb, p, preferred_element_type=F32)
    return d(hi) + d(mid) + d(lo)


def _dot_sel_rhs(x, t):
    tb = jnp.where(t, 1.0, 0.0).astype(BF16)
    hi, mid, lo = _split3(x)
    d = lambda p: jnp.dot(p, tb, preferred_element_type=F32)
    return d(hi) + d(mid) + d(lo)


def _dot3(a, b):
    ah = a.astype(BF16)
    al = (a - ah.astype(F32)).astype(BF16)
    bh = b.astype(BF16)
    bl = (b - bh.astype(F32)).astype(BF16)
    d = lambda p, q: jnp.dot(p, q, preferred_element_type=F32)
    return d(ah, bh) + d(ah, bl) + d(al, bh)


def _sigmoid(x):
    return 1.0 / (1.0 + jnp.exp(-x))


def _silu(x):
    return x * _sigmoid(x)


def _softplus(x):
    return jnp.maximum(x, 0.0) + jnp.log1p(jnp.exp(-jnp.abs(x)))


def _gelu_tanh(x):
    return 0.5 * x * (1.0 + jnp.tanh(math.sqrt(2.0 / math.pi) * (x + 0.044715 * (x * x * x))))


def _ln(y, g=None, b=None, eps=LN_EPS):
    mu = jnp.mean(y, axis=-1, keepdims=True)
    d = y - mu
    var = jnp.mean(d * d, axis=-1, keepdims=True)
    out = d * lax.rsqrt(var + eps)
    if g is not None:
        out = out * g
    if b is not None:
        out = out + b
    return out


def _l2norm(x, eps=1e-6):
    return x * lax.rsqrt(jnp.sum(x * x, axis=-1, keepdims=True) + eps)


def _tri(c, inclusive):
    t = lax.broadcasted_iota(jnp.int32, (c, c), 0)
    s = lax.broadcasted_iota(jnp.int32, (c, c), 1)
    return (s <= t) if inclusive else (s < t)


def _unit_lower_solve(n_mat, rhs, c):
    x = rhs
    p = n_mat
    steps = max(1, int(math.ceil(math.log2(c))))
    for i in range(steps):
        x = x + _dot3(p, x)
        if i + 1 < steps:
            p = _dot3(p, p)
    return x


def _dplr_chunk(a, b, k, v, r, state, *, c, lw=None, gi_col=None, gx_col=None, gi_row=None):
    incl = _tri(c, True)
    strict = _tri(c, False)
    vector = lw is not None
    if vector:
        gi = _dot_sel_lhs(incl, lw)
        gx = gi - lw
        gm = gi[c // 2:c // 2 + 1, :]
        gc = gi[c - 1:c, :]
        lhs = jnp.concatenate([a * jnp.exp(gx - gm), r * jnp.exp(gi - gm)], axis=0)
        em = jnp.exp(gm - gi)
        rhs = jnp.concatenate([b * em, k * em], axis=0)
        m = _dot_nt(lhs, rhs)
        a_ab = jnp.where(strict, m[:c, :c], 0.0)
        a_ak = jnp.where(strict, m[:c, c:], 0.0)
        r_b = jnp.where(incl, m[c:, :c], 0.0)
        r_k = jnp.where(incl, m[c:, c:], 0.0)
        a_hat = a * jnp.exp(gx)
        r_hat = r * jnp.exp(gi)
        e_end = jnp.exp(gc - gi)
        xh = _dot_nt(jnp.concatenate([a_hat, r_hat], axis=0), state)
    else:
        gc = gi_col[c - 1:c, :]
        dx = jnp.where(strict, jnp.exp(jnp.where(strict, gx_col - gi_row, 0.0)), 0.0)
        di = jnp.where(incl, jnp.exp(jnp.where(incl, gi_col - gi_row, 0.0)), 0.0)
        m = _dot_nt(jnp.concatenate([a, r], axis=0), jnp.concatenate([b, k], axis=0))
        a_ab = m[:c, :c] * dx
        a_ak = m[:c, c:] * dx
        r_b = m[c:, :c] * di
        r_k = m[c:, c:] * di
        a_hat = a * jnp.exp(gx_col)
        r_hat = r * jnp.exp(gi_col)
        e_end = jnp.exp(gc - gi_col)
        xh = _dot(jnp.concatenate([a_hat, r_hat], axis=0), state)
    u = _unit_lower_solve(a_ab, xh[:c] + _dot(a_ak, v), c)
    uv = jnp.concatenate([u, v], axis=0)
    o = xh[c:] + _dot(jnp.concatenate([r_b, r_k], axis=1), uv)
    bk = jnp.concatenate([b * e_end, k * e_end], axis=0)
    if vector:
        new_state = state * jnp.exp(gc) + _dot_tn(uv, bk)
    else:
        new_state = state * jnp.exp(gc) + _dot_tn(bk, uv)
    return o, new_state


def _valid_rows(c, n_valid):
    return lax.broadcasted_iota(jnp.int32, (c, 1), 0) < n_valid


def _mod_kernel(c_ref, w_ref, b_ref, o_ref):
    o_ref[0] = _dot(_silu(c_ref[...]), w_ref[0]) + b_ref[0]


def _modulation(c_all, mod_w, mod_b):
    n = c_all.shape[0]
    d = D_MODEL
    return pl.pallas_call(
        _mod_kernel,
        grid=(DEPTH, 6),
        in_specs=[pl.BlockSpec((n, d), lambda l, j: (0, 0)),
                  pl.BlockSpec((1, d, d), lambda l, j: (l, 0, j)),
                  pl.BlockSpec((1, 1, d), lambda l, j: (l, 0, j))],
        out_specs=pl.BlockSpec((1, n, d), lambda l, j: (l, 0, j)),
        out_shape=jax.ShapeDtypeStruct((DEPTH, n, 6 * d), F32),
        compiler_params=_cparams("parallel", "parallel"),
        name="modulation",
    )(c_all, mod_w, mod_b.reshape(DEPTH, 1, 6 * d))


def _mod_spec(mod, tm):
    if mod.shape[1] == 1:
        return pl.BlockSpec((1, 1, mod.shape[2]), lambda g, i, *_: (g, 0, 0))
    return pl.BlockSpec((1, tm, mod.shape[2]), lambda g, i, *_: (g, i, 0))


def _inproj_kernel(x_ref, sc_ref, sh_ref, w_ref, *o_refs, splits):
    h = (x_ref[0] * (1.0 + sc_ref[0]) + sh_ref[0]).astype(BF16)
    for o_ref, (s, n) in zip(o_refs, splits):
        o_ref[0] = jnp.dot(h, w_ref[:, s:s + n], preferred_element_type=F32)


def _inproj(x, sc, sh, w, splits):
    g, r, d = x.shape
    tm = min(r, ROW_TILE)
    n_all = w.shape[1]
    return pl.pallas_call(
        functools.partial(_inproj_kernel, splits=splits),
        grid=(g, r // tm),
        in_specs=[pl.BlockSpec((1, tm, d), lambda g, i: (g, i, 0)),
                  _mod_spec(sc, tm), _mod_spec(sh, tm),
                  pl.BlockSpec((d, n_all), lambda g, i: (0, 0))],
        out_specs=[pl.BlockSpec((1, tm, n), lambda g, i: (g, i, 0)) for _, n in splits],
        out_shape=[jax.ShapeDtypeStruct((g, r, n), F32) for _, n in splits],
        compiler_params=_cparams("parallel", "parallel"),
        name="inproj",
    )(x, sc, sh, w)


def _outproj_ln_kernel(x_ref, ma_ref, mb_ref, gate_ref, w_ref, lng_ref, lnb_ref, o_ref):
    half = ma_ref.shape[2]
    f = _dot(ma_ref[0], w_ref[0:half, :]) + _dot(mb_ref[0], w_ref[half:, :])
    y = DN_ALPHA * x_ref[0] + (1.0 + gate_ref[0]) * f
    o_ref[0] = _ln(y, lng_ref[...], lnb_ref[...])


def _outproj_ln(x, mix_a, mix_b, gate, w, ln_g, ln_b):
    g, r, d = x.shape
    tm = min(r, ROW_TILE)
    half = mix_a.shape[2]
    row = pl.BlockSpec((1, tm, d), lambda g, i: (g, i, 0))
    mrow = pl.BlockSpec((1, tm, half), lambda g, i: (g, i, 0))
    vec = pl.BlockSpec((1, d), lambda g, i: (0, 0))
    return pl.pallas_call(
        _outproj_ln_kernel,
        grid=(g, r // tm),
        in_specs=[row, mrow, mrow, _mod_spec(gate, tm),
                  pl.BlockSpec((2 * half, d), lambda g, i: (0, 0)), vec, vec],
        out_specs=row,
        out_shape=jax.ShapeDtypeStruct((g, r, d), F32),
        compiler_params=_cparams("parallel", "parallel"),
        name="outproj_ln",
    )(x, mix_a, mix_b, gate, w, ln_g.reshape(1, d), ln_b.reshape(1, d))


def _ffn_kernel(x_ref, sc_ref, sh_ref, gate_ref, wg_ref, wu_ref, wd_ref, lng_ref, lnb_ref,
                o_ref, h_scr, acc_scr):
    j = pl.program_id(2)

    @pl.when(j == 0)
    def _():
        h_scr[...] = (x_ref[0] * (1.0 + sc_ref[0]) + sh_ref[0]).astype(BF16)
        acc_scr[...] = jnp.zeros_like(acc_scr)

    h = h_scr[...]
    a = jnp.dot(h, wg_ref[...], preferred_element_type=F32)
    u = jnp.dot(h, wu_ref[...], preferred_element_type=F32)
    acc_scr[...] += _dot(_silu(a) * u, wd_ref[...])

    @pl.when(j == pl.num_programs(2) - 1)
    def _():
        y = DN_ALPHA * x_ref[0] + (1.0 + gate_ref[0]) * acc_scr[...]
        o_ref[0] = _ln(y, lng_ref[...], lnb_ref[...])


def _ffn_ln(x, sc, sh, gate, wg, wu, wd, ln_g, ln_b):
    g, r, d = x.shape
    tm = min(r, ROW_TILE)
    f = wg.shape[1]
    row = pl.BlockSpec((1, tm, d), lambda g, i, j: (g, i, 0))
    vec = pl.BlockSpec((1, d), lambda g, i, j: (0, 0))
    return pl.pallas_call(
        _ffn_kernel,
        grid=(g, r // tm, f // FF_TILE),
        in_specs=[row, _mod_spec(sc, tm), _mod_spec(sh, tm), _mod_spec(gate, tm),
                  pl.BlockSpec((d, FF_TILE), lambda g, i, j: (0, j)),
                  pl.BlockSpec((d, FF_TILE), lambda g, i, j: (0, j)),
                  pl.BlockSpec((FF_TILE, d), lambda g, i, j: (j, 0)), vec, vec],
        out_specs=row,
        out_shape=jax.ShapeDtypeStruct((g, r, d), F32),
        scratch_shapes=[pltpu.VMEM((tm, d), BF16), pltpu.VMEM((tm, d), F32)],
        compiler_params=_cparams("parallel", "parallel", "arbitrary"),
        name="ffn_ln",
    )(x, sc, sh, gate, wg, wu, wd, ln_g.reshape(1, d), ln_b.reshape(1, d))


def _moe_kernel(x_ref, sc_ref, sh_ref, gate_ref, rw_ref, wg_ref, wu_ref, wd_ref, lng_ref, lnb_ref,
                o_ref, h_scr, comb_scr, acc_scr):
    e = pl.program_id(2)
    j = pl.program_id(3)
    lane = lax.broadcasted_iota(jnp.int32, comb_scr.shape, 1)

    @pl.when((e == 0) & (j == 0))
    def _():
        h = x_ref[0] * (1.0 + sc_ref[0]) + sh_ref[0]
        h_scr[...] = h.astype(BF16)
        acc_scr[...] = jnp.zeros_like(acc_scr)
        logits = jnp.dot(h, rw_ref[...], preferred_element_type=F32, precision=lax.Precision.HIGHEST)
        logits = jnp.where(lane < N_EXPERTS, logits, -jnp.inf)
        m1 = jnp.max(logits, axis=-1, keepdims=True)
        i1 = jnp.min(jnp.where(logits == m1, lane, LANES), axis=-1, keepdims=True)
        rest = jnp.where(lane == i1, -jnp.inf, logits)
        m2 = jnp.max(rest, axis=-1, keepdims=True)
        i2 = jnp.min(jnp.where(rest == m2, lane, LANES), axis=-1, keepdims=True)
        e2 = jnp.exp(m2 - m1)
        g1 = 1.0 / (1.0 + e2)
        g2 = e2 / (1.0 + e2)
        comb_scr[...] = jnp.where(lane == i1, g1, 0.0) + jnp.where(lane == i2, g2, 0.0)

    h = h_scr[...]
    a = jnp.dot(h, wg_ref[0], preferred_element_type=F32)
    u = jnp.dot(h, wu_ref[0], preferred_element_type=F32)
    comb_e = jnp.sum(jnp.where(lane == e, comb_scr[...], 0.0), axis=-1, keepdims=True)
    acc_scr[...] += comb_e * _dot(_silu(a) * u, wd_ref[0])

    @pl.when((e == pl.num_programs(2) - 1) & (j == pl.num_programs(3) - 1))
    def _():
        y = DN_ALPHA * x_ref[0] + (1.0 + gate_ref[0]) * acc_scr[...]
        o_ref[0] = _ln(y, lng_ref[...], lnb_ref[...])


def _moe_ln(x, sc, sh, gate, router_w, wg, wu, wd, ln_g, ln_b):
    g, r, d = x.shape
    tm = min(r, ROW_TILE)
    ne, _, f = wg.shape
    row = pl.BlockSpec((1, tm, d), lambda g, i, e, j: (g, i, 0))
    vec = pl.BlockSpec((1, d), lambda g, i, e, j: (0, 0))
    rw = jnp.pad(router_w, ((0, 0), (0, LANES - ne)))
    return pl.pallas_call(
        _moe_kernel,
        grid=(g, r // tm, ne, f // FF_TILE),
        in_specs=[row, _mod_spec(sc, tm), _mod_spec(sh, tm), _mod_spec(gate, tm),
                  pl.BlockSpec((d, LANES), lambda g, i, e, j: (0, 0)),
                  pl.BlockSpec((1, d, FF_TILE), lambda g, i, e, j: (e, 0, j)),
                  pl.BlockSpec((1, d, FF_TILE), lambda g, i, e, j: (e, 0, j)),
                  pl.BlockSpec((1, FF_TILE, d), lambda g, i, e, j: (e, j, 0)), vec, vec],
        out_specs=row,
        out_shape=jax.ShapeDtypeStruct((g, r, d), F32),
        scratch_shapes=[pltpu.VMEM((tm, d), BF16), pltpu.VMEM((tm, LANES), F32),
                        pltpu.VMEM((tm, d), F32)],
        compiler_params=_cparams("parallel", "parallel", "arbitrary", "arbitrary"),
        name="moe_ln",
    )(x, sc, sh, gate, rw, wg, wu, wd, ln_g.reshape(1, d), ln_b.reshape(1, d))


def _seq_dims(l):
    if l >= SEQ_CHUNK:
        assert l % SEQ_CHUNK == 0
        return SEQ_CHUNK, l // SEQ_CHUNK, SEQ_CHUNK
    assert l <= SUBLANES
    return SUBLANES, 1, l


def _pad_seq(t, c, n):
    pad = c * n - t.shape[1]
    return t if pad == 0 else jnp.pad(t, ((0, 0), (0, pad), (0, 0)))


def _rows_layout(t, c, n, width):
    b = t.shape[0]
    return jnp.swapaxes(t[:, :, :width].reshape(b, n, c, width), 2, 3)


def _conv_window(scr, u, taps, c):
    scr[SUBLANES:SUBLANES + c, :] = u
    out = scr[5:5 + c, :] * taps[0:1, :]
    for j in range(1, CONV_W):
        out = out + scr[5 + j:5 + j + c, :] * taps[j:j + 1, :]
    return out


def _lru_kernel(xy_ref, buf_ref, h0_ref, cw_ref, cb_ref, wg_ref, bg_ref, lam_ref,
                out_ref, hnew_ref, bufnew_ref, xs_scr, a_scr, b_scr, h_scr, *, c, nv, pos0):
    i = pl.program_id(1)
    w = LRU_W

    @pl.when(i == 0)
    def _():
        xs_scr[0:SUBLANES, :] = jnp.zeros((SUBLANES, w), F32)
        xs_scr[5:8, :] = buf_ref[0]
        h_scr[...] = h0_ref[0]

    xc = _conv_window(xs_scr, xy_ref[0, :, 0:w], cw_ref[...], c) + cb_ref[...]
    gates = _dot(xc, wg_ref[...]) + bg_ref[...]
    r = _sigmoid(gates[:, 0:w])
    ig = _sigmoid(gates[:, w:2 * w])
    log_a = -LRU_C * r * _softplus(-lam_ref[...])
    mult = jnp.sqrt(-jnp.tanh(log_a) * (jnp.exp(2.0 * log_a) + 1.0))
    pos = lax.broadcasted_iota(jnp.int32, (c, 1), 0) + (i * c + pos0)
    mult = jnp.where(pos == 0, 1.0, mult)
    a_scr[...] = jnp.exp(log_a)
    b_scr[...] = xc * ig * mult

    def step(t, h):
        h = a_scr[pl.ds(t, 1), :] * h + b_scr[pl.ds(t, 1), :]
        b_scr[pl.ds(t, 1), :] = h
        return h

    h = lax.fori_loop(0, nv, step, h_scr[...], unroll=min(nv, SUBLANES))
    h_scr[...] = h
    out_ref[0] = b_scr[...] * _gelu_tanh(xy_ref[0, :, w:2 * w])
    tail = xs_scr[5 + nv:8 + nv, :]
    xs_scr[5:8, :] = tail

    @pl.when(i == pl.num_programs(1) - 1)
    def _():
        hnew_ref[0] = h
        bufnew_ref[0] = tail


def _lru_mixer(xy, buf, h0, conv_w, conv_b, wr, br, wi, bi, lam, pos0):
    b, l, _ = xy.shape
    c, n, nv = _seq_dims(l)
    w = LRU_W
    bd = lambda m: jax.scipy.linalg.block_diag(*[m[i] for i in range(LRU_BLOCKS)])
    wgate = jnp.concatenate([bd(wr), bd(wi)], axis=1).astype(BF16)
    bgate = jnp.concatenate([br, bi]).reshape(1, 2 * w)
    const = lambda shape: pl.BlockSpec(shape, lambda b, i: (0,) * len(shape))
    out, h_new, buf_new = pl.pallas_call(
        functools.partial(_lru_kernel, c=c, nv=nv, pos0=pos0),
        grid=(b, n),
        in_specs=[pl.BlockSpec((1, c, 2 * w), lambda b, i: (b, i, 0)),
                  pl.BlockSpec((1, CONV_W - 1, w), lambda b, i: (b, 0, 0)),
                  pl.BlockSpec((1, 1, w), lambda b, i: (b, 0, 0)),
                  const((CONV_W, w)), const((1, w)), const((w, 2 * w)), const((1, 2 * w)),
                  const((1, w))],
        out_specs=[pl.BlockSpec((1, c, w), lambda b, i: (b, i, 0)),
                   pl.BlockSpec((1, 1, w), lambda b, i: (b, 0, 0)),
                   pl.BlockSpec((1, CONV_W - 1, w), lambda b, i: (b, 0, 0))],
        out_shape=[jax.ShapeDtypeStruct((b, n * c, w), F32),
                   jax.ShapeDtypeStruct((b, 1, w), F32),
                   jax.ShapeDtypeStruct((b, CONV_W - 1, w), F32)],
        scratch_shapes=[pltpu.VMEM((SUBLANES + c, w), F32), pltpu.VMEM((c, w), F32),
                        pltpu.VMEM((c, w), F32), pltpu.VMEM((1, w), F32)],
        compiler_params=_cparams("parallel", "arbitrary"),
        name="rglru",
    )(_pad_seq(xy, c, n), buf, h0.reshape(b, 1, w), conv_w, conv_b.reshape(1, w), wgate, bgate,
      lam.reshape(1, w))
    return out[:, :l], h_new.reshape(b, w), buf_new


def _gdn_kernel(qkv_ref, z_ref, gcol_ref, grow_ref, buf_ref, s0_ref, cw_ref, pcol_ref, prow_ref,
                nw_ref, out_ref, snew_ref, bufnew_ref, xs_scr, s_scr, *, c, nv):
    i = pl.program_id(1)
    qk = GDN_QK

    @pl.when(i == 0)
    def _():
        xs_scr[0:SUBLANES, :] = jnp.zeros((SUBLANES, GDN_QKV), F32)
        xs_scr[5:8, :] = buf_ref[0]
        s_scr[...] = s0_ref[0]

    x = _silu(_conv_window(xs_scr, qkv_ref[0], cw_ref[...], c))
    valid = _valid_rows(c, nv)
    gcol = gcol_ref[0]
    grow = grow_ref[0, 0]
    alog_c, dtb_c = pcol_ref[0:1, :], pcol_ref[1:2, :]
    alog_r, dtb_r = prow_ref[:, 0:1], prow_ref[:, 1:2]
    g_col = jnp.where(valid, -jnp.exp(alog_c) * _softplus(gcol[:, 0:GDN_H] + dtb_c), 0.0)
    beta = jnp.where(valid, _sigmoid(gcol[:, GDN_H:2 * GDN_H]), 0.0)
    valid_r = lax.broadcasted_iota(jnp.int32, (1, c), 1) < nv
    g_row = jnp.where(valid_r, -jnp.exp(alog_r) * _softplus(grow[0:GDN_H, :] + dtb_r), 0.0)
    gi_cols = _dot_sel_lhs(_tri(c, True), g_col)
    upper = jnp.logical_not(_tri(c, False))
    gi_rows = _dot_sel_rhs(g_row, upper)
    for h in range(GDN_H):
        q = _l2norm(x[:, h * GDN_DK:(h + 1) * GDN_DK]) * (GDN_DK ** -0.5)
        k = _l2norm(x[:, qk + h * GDN_DK:qk + (h + 1) * GDN_DK])
        v = x[:, 2 * qk + h * GDN_DV:2 * qk + (h + 1) * GDN_DV]
        bh = beta[:, h:h + 1]
        gh = g_col[:, h:h + 1]
        gi = gi_cols[:, h:h + 1]
        kb = k * bh
        o, s_new = _dplr_chunk(k, -jnp.exp(gh) * kb, kb, v, q, s_scr[h], c=c,
                               gi_col=gi, gx_col=gi - gh, gi_row=gi_rows[h:h + 1, :])
        s_scr[h] = s_new
        zh = z_ref[0, :, h * GDN_DV:(h + 1) * GDN_DV]
        o = o * lax.rsqrt(jnp.mean(o * o, axis=-1, keepdims=True) + 1e-6) * nw_ref[...]
        out_ref[0, :, h * GDN_DV:(h + 1) * GDN_DV] = o * _silu(zh)
    tail = xs_scr[5 + nv:8 + nv, :]
    xs_scr[5:8, :] = tail

    @pl.when(i == pl.num_programs(1) - 1)
    def _():
        snew_ref[0] = s_scr[...]
        bufnew_ref[0] = tail


def _gdn_mixer(qkv, z, gates, buf, s0, conv_w, a_log, dt_bias, norm_w):
    b, l, _ = qkv.shape
    c, n, nv = _seq_dims(l)
    gates = _pad_seq(gates, c, n)
    pcol = jnp.stack([a_log, dt_bias])
    const = lambda shape: pl.BlockSpec(shape, lambda b, i: (0,) * len(shape))
    out, s_new, buf_new = pl.pallas_call(
        functools.partial(_gdn_kernel, c=c, nv=nv),
        grid=(b, n),
        in_specs=[pl.BlockSpec((1, c, GDN_QKV), lambda b, i: (b, i, 0)),
                  pl.BlockSpec((1, c, GDN_W), lambda b, i: (b, i, 0)),
                  pl.BlockSpec((1, c, LANES), lambda b, i: (b, i, 0)),
                  pl.BlockSpec((1, 1, SUBLANES, c), lambda b, i: (b, i, 0, 0)),
                  pl.BlockSpec((1, CONV_W - 1, GDN_QKV), lambda b, i: (b, 0, 0)),
                  pl.BlockSpec((1, GDN_H, GDN_DK, GDN_DV), lambda b, i: (b, 0, 0, 0)),
                  const((CONV_W, GDN_QKV)), const((2, GDN_H)), const((GDN_H, 2)),
                  const((1, GDN_DV))],
        out_specs=[pl.BlockSpec((1, c, GDN_W), lambda b, i: (b, i, 0)),
                   pl.BlockSpec((1, GDN_H, GDN_DK, GDN_DV), lambda b, i: (b, 0, 0, 0)),
                   pl.BlockSpec((1, CONV_W - 1, GDN_QKV), lambda b, i: (b, 0, 0))],
        out_shape=[jax.ShapeDtypeStruct((b, n * c, GDN_W), F32),
                   jax.ShapeDtypeStruct((b, GDN_H, GDN_DK, GDN_DV), F32),
                   jax.ShapeDtypeStruct((b, CONV_W - 1, GDN_QKV), F32)],
        scratch_shapes=[pltpu.VMEM((SUBLANES + c, GDN_QKV), F32),
                        pltpu.VMEM((GDN_H, GDN_DK, GDN_DV), F32)],
        compiler_params=_cparams("parallel", "arbitrary"),
        name="gdn",
    )(_pad_seq(qkv, c, n), _pad_seq(z, c, n), gates, _rows_layout(gates, c, n, SUBLANES), buf, s0,
      conv_w, pcol, pcol.T, norm_w.reshape(1, GDN_DV))
    return out[:, :l], s_new, buf_new


def _rwkv_kernel(rw_ref, prev_ref, s0_ref, mix_ref, w0_ref, w2_ref, a0_ref, a2_ref, g2_ref,
                 kk_ref, ka_ref, rk_ref, lnw_ref, lnb_ref, out_ref, snew_ref, xs_scr, s_scr, *, c, nv):
    i = pl.program_id(1)
    hd = RWKV_HD
    w = RWKV_W

    @pl.when(i == 0)
    def _():
        xs_scr[0:SUBLANES, :] = jnp.zeros((SUBLANES, RWKV_PROJ_W), F32)
        xs_scr[7:8, :] = prev_ref[0]
        s_scr[...] = s0_ref[0]

    rw = rw_ref[0]
    xs_scr[SUBLANES:SUBLANES + c, :] = rw
    prev = xs_scr[7:7 + c, :]
    xs_scr[7:8, :] = xs_scr[7 + nv:8 + nv, :]
    xs = rw + (prev - rw) * mix_ref[...]
    valid = _valid_rows(c, nv)
    r_all, k_all, v_all = xs[:, 0:w], xs[:, w:2 * w], xs[:, 2 * w:3 * w]
    o1 = 3 * w
    wl = xs[:, o1:o1 + RWKV_RW]
    al = xs[:, o1 + RWKV_RW:o1 + RWKV_RW + RWKV_RA]
    gl = xs[:, o1 + RWKV_RW + RWKV_RA:]
    wdec = -_softplus(-(w0_ref[...] + _dot(jnp.tanh(wl), w2_ref[...]))) - 0.5
    a_all = _sigmoid(a0_ref[...] + _dot(al, a2_ref[...]))
    gate = _dot(_sigmoid(gl), g2_ref[...])
    lw_all = jnp.where(valid, -jnp.exp(wdec), 0.0)
    kmod = jnp.where(valid, k_all * (1.0 + (a_all - 1.0) * ka_ref[...]), 0.0)
    kk_all = k_all * kk_ref[...]
    for h in range(RWKV_H):
        sl = slice(h * hd, (h + 1) * hd)
        kk = _l2norm(kk_all[:, sl])
        r, k, v = r_all[:, sl], kmod[:, sl], v_all[:, sl]
        b = jnp.where(valid, kk * a_all[:, sl], 0.0)
        o, s_new = _dplr_chunk(-kk, b, k, v, r, s_scr[h], c=c, lw=lw_all[:, sl])
        s_scr[h] = s_new
        y = _ln(o, lnw_ref[:, sl], lnb_ref[:, sl], RWKV_GN_EPS)
        y = y + jnp.sum(r * k * rk_ref[:, sl], axis=-1, keepdims=True) * v
        out_ref[0, :, sl] = y * gate[:, sl]

    @pl.when(i == pl.num_programs(1) - 1)
    def _():
        snew_ref[0] = s_scr[...]


def _rwkv_mixer(rw, shift0, s0, mix, w0, w2, a0, a2, g2, k_k, k_a, r_k, ln_w, ln_b):
    b, l, _ = rw.shape
    c, n, nv = _seq_dims(l)
    w = RWKV_W
    const = lambda shape: pl.BlockSpec(shape, lambda b, i: (0,) * len(shape))
    row = lambda t: t.reshape(1, -1)
    out, s_new = pl.pallas_call(
        functools.partial(_rwkv_kernel, c=c, nv=nv),
        grid=(b, n),
        in_specs=[pl.BlockSpec((1, c, RWKV_PROJ_W), lambda b, i: (b, i, 0)),
                  pl.BlockSpec((1, 1, RWKV_PROJ_W), lambda b, i: (b, 0, 0)),
                  pl.BlockSpec((1, RWKV_H, RWKV_HD, RWKV_HD), lambda b, i: (b, 0, 0, 0)),
                  const((1, RWKV_PROJ_W)), const((1, w)), const((RWKV_RW, w)), const((1, w)),
                  const((RWKV_RA, w)), const((RWKV_RG, w)), const((1, w)), const((1, w)),
                  const((1, w)), const((1, w)), const((1, w))],
        out_specs=[pl.BlockSpec((1, c, w), lambda b, i: (b, i, 0)),
                   pl.BlockSpec((1, RWKV_H, RWKV_HD, RWKV_HD), lambda b, i: (b, 0, 0, 0))],
        out_shape=[jax.ShapeDtypeStruct((b, n * c, w), F32),
                   jax.ShapeDtypeStruct((b, RWKV_H, RWKV_HD, RWKV_HD), F32)],
        scratch_shapes=[pltpu.VMEM((SUBLANES + c, RWKV_PROJ_W), F32),
                        pltpu.VMEM((RWKV_H, RWKV_HD, RWKV_HD), F32)],
        compiler_params=_cparams("parallel", "arbitrary"),
        name="rwkv7",
    )(_pad_seq(rw, c, n), shift0.reshape(b, 1, RWKV_PROJ_W), s0, row(mix), row(w0),
      w2.astype(BF16), row(a0), a2.astype(BF16), g2.astype(BF16), row(k_k), row(k_a), row(r_k),
      row(ln_w), row(ln_b))
    return out[:, :l], s_new


def _mlstm_kernel(p_ref, gcol_ref, grow_ref, c0_ref, n0_ref, m0_ref, bcol_ref, brow_ref, nw_ref,
                  out_ref, cnew_ref, nnew_ref, mnew_ref, c_scr, n_scr, m_scr, *, c, nv):
    i = pl.program_id(1)
    nh, dk, dv = MLSTM_H, MLSTM_DK, MLSTM_DV

    @pl.when(i == 0)
    def _():
        c_scr[...] = c0_ref[0]
        n_scr[...] = n0_ref[0]
        m_scr[...] = m0_ref[0]

    valid = _valid_rows(c, nv)
    valid_r = lax.broadcasted_iota(jnp.int32, (1, c), 1) < nv
    gcol = gcol_ref[0]
    grow = grow_ref[0, 0]
    li_col = jnp.where(valid, gcol[:, 0:nh] + bcol_ref[0:1, :], NEG_BIG)
    lf_col = jnp.where(valid, -_softplus(-(gcol[:, nh:2 * nh] + bcol_ref[1:2, :])), 0.0)
    li_row = jnp.where(valid_r, grow[0:nh, :] + brow_ref[:, 0:1], NEG_BIG)
    lf_row = jnp.where(valid_r, -_softplus(-(grow[nh:2 * nh, :] + brow_ref[:, 1:2])), 0.0)
    incl = _tri(c, True)
    b_cols = _dot_sel_lhs(incl, lf_col)
    b_rows = _dot_sel_rhs(lf_row, jnp.logical_not(_tri(c, False)))
    for h in range(nh):
        q = p_ref[0, :, h * dk:(h + 1) * dk]
        k = p_ref[0, :, nh * dk + h * dk:nh * dk + (h + 1) * dk] * (dk ** -0.5)
        v = p_ref[0, :, 2 * nh * dk + h * dv:2 * nh * dk + (h + 1) * dv]
        og = p_ref[0, :, 2 * nh * dk + nh * dv + h * dv:2 * nh * dk + nh * dv + (h + 1) * dv]
        cm, nvec, m_prev = c_scr[h], n_scr[h], m_scr[h]
        bc = b_cols[:, h:h + 1]
        dm = jnp.where(incl, bc - b_rows[h:h + 1, :] + li_row[h:h + 1, :], -jnp.inf)
        m_inter = bc + m_prev
        m_t = jnp.maximum(m_inter, jnp.max(dm, axis=-1, keepdims=True))
        w_inter = jnp.exp(m_inter - m_t)
        sc = _dot_nt(q, k) * jnp.exp(dm - m_t)
        num = w_inter * _dot_nt(q, cm) + _dot(sc, v)
        den = w_inter * jnp.sum(q * nvec, axis=-1, keepdims=True) + jnp.sum(sc, axis=-1, keepdims=True)
        hh = num / jnp.maximum(jnp.abs(den), jnp.exp(-m_t))
        m_new = m_t[c - 1:c, :]
        b_last = bc[c - 1:c, :]
        dec = jnp.exp(b_last + m_prev - m_new)
        ws = jnp.exp(b_last - bc + li_col[:, h:h + 1] - m_new)
        c_scr[h] = dec * cm + _dot_tn(v * ws, k)
        n_scr[h] = dec * nvec + jnp.sum(k * ws, axis=0, keepdims=True)
        m_scr[h] = m_new
        out_ref[0, :, h * dv:(h + 1) * dv] = _ln(hh, nw_ref[...]) * _sigmoid(og)

    @pl.when(i == pl.num_programs(1) - 1)
    def _():
        cnew_ref[0] = c_scr[...]
        nnew_ref[0] = n_scr[...]
        mnew_ref[0] = m_scr[...]


def _mlstm_mixer(p, gates, c0, n0, m0, i_b, f_b, norm_w):
    b, l, _ = p.shape
    c, n, nv = _seq_dims(l)
    nh, dk, dv = MLSTM_H, MLSTM_DK, MLSTM_DV
    gates = _pad_seq(gates, c, n)
    bcol = jnp.stack([i_b, f_b])
    const = lambda shape: pl.BlockSpec(shape, lambda b, i: (0,) * len(shape))
    st = lambda shape: pl.BlockSpec((1,) + shape, lambda b, i: (b,) + (0,) * len(shape))
    out, c_new, n_new, m_new = pl.pallas_call(
        functools.partial(_mlstm_kernel, c=c, nv=nv),
        grid=(b, n),
        in_specs=[pl.BlockSpec((1, c, p.shape[2]), lambda b, i: (b, i, 0)),
                  pl.BlockSpec((1, c, LANES), lambda b, i: (b, i, 0)),
                  pl.BlockSpec((1, 1, SUBLANES, c), lambda b, i: (b, i, 0, 0)),
                  st((nh, dv, dk)), st((nh, 1, dk)), st((nh, 1, 1)),
                  const((2, nh)), const((nh, 2)), const((1, dv))],
        out_specs=[pl.BlockSpec((1, c, nh * dv), lambda b, i: (b, i, 0)),
                   st((nh, dv, dk)), st((nh, 1, dk)), st((nh, 1, 1))],
        out_shape=[jax.ShapeDtypeStruct((b, n * c, nh * dv), F32),
                   jax.ShapeDtypeStruct((b, nh, dv, dk), F32),
                   jax.ShapeDtypeStruct((b, nh, 1, dk), F32),
                   jax.ShapeDtypeStruct((b, nh, 1, 1), F32)],
        scratch_shapes=[pltpu.VMEM((nh, dv, dk), F32), pltpu.VMEM((nh, 1, dk), F32),
                        pltpu.VMEM((nh, 1, 1), F32)],
        compiler_params=_cparams("parallel", "arbitrary"),
        name="mlstm",
    )(_pad_seq(p, c, n), gates, _rows_layout(gates, c, n, SUBLANES), c0,
      n0.reshape(b, nh, 1, dk), m0.reshape(b, nh, 1, 1), bcol, bcol.T, norm_w.reshape(1, dv))
    return out[:, :l], c_new, n_new.reshape(b, nh, dk), m_new.reshape(b, nh)


def _pad_cols(w, n):
    return jnp.pad(w, ((0, 0), (0, n - w.shape[1])))


def _ab_in_weight(w):
    o = 2 * LRU_W
    xy, qkv = w[:, :o], w[:, o:o + GDN_QKV]
    o += GDN_QKV
    ab, z = w[:, o:o + 2 * GDN_H], w[:, o + 2 * GDN_H:]
    return jnp.concatenate([xy, qkv, z, _pad_cols(ab, LANES)], axis=1).astype(BF16)


AB_SPLITS = ((0, 2 * LRU_W), (2 * LRU_W, GDN_QKV), (2 * LRU_W + GDN_QKV, GDN_W),
             (2 * LRU_W + GDN_QKV + GDN_W, LANES))


def _cd_in_weight(w):
    o = RWKV_PROJ_W
    rw, qkv = w[:, :o], w[:, o:o + 2 * MLSTM_QK + MLSTM_W]
    o += 2 * MLSTM_QK + MLSTM_W
    gates, og = w[:, o:o + 2 * MLSTM_H], w[:, o + 2 * MLSTM_H:]
    return jnp.concatenate([rw, qkv, og, _pad_cols(gates, LANES)], axis=1).astype(BF16)


CD_SPLITS = ((0, RWKV_PROJ_W), (RWKV_PROJ_W, 2 * MLSTM_QK + 2 * MLSTM_W),
             (RWKV_PROJ_W + 2 * MLSTM_QK + 2 * MLSTM_W, LANES))


def kernel(x_prompt, x_sample, c_prompt, c_sample,
           state_lru_h, state_lru_conv, state_gdn_S, state_gdn_conv,
           state_rwkv_S, state_rwkv_shift, state_mlstm_C, state_mlstm_n, state_mlstm_m,
           mod_w, mod_b, ln1_g, ln1_b, ln2_g, ln2_b,
           ab_w_in, ab_w_out, lru_conv_w, lru_conv_b, lru_wr, lru_br, lru_wi, lru_bi, lru_lambda,
           gdn_conv_w, gdn_a_log, gdn_dt_bias, gdn_norm_w,
           cd_w_in, cd_w_out, rwkv_mix, rwkv_w0, rwkv_w2, rwkv_a0, rwkv_a2, rwkv_g2,
           rwkv_k_k, rwkv_k_a, rwkv_r_k, rwkv_ln_w, rwkv_ln_b,
           mlstm_i_b, mlstm_f_b, mlstm_norm_w,
           ffn_w_gate, ffn_w_up, ffn_w_down,
           router_w, moe_w_gate, moe_w_up, moe_w_down):
    d = D_MODEL
    bp, lp, _ = x_prompt.shape
    bs, ls, _ = x_sample.shape
    mod = _modulation(jnp.concatenate([c_prompt, c_sample], axis=0), mod_w, mod_b)

    ab_in = [_ab_in_weight(ab_w_in[j]) for j in range(ab_w_in.shape[0])]
    cd_in = [_cd_in_weight(cd_w_in[j]) for j in range(cd_w_in.shape[0])]
    ab_out, cd_out = ab_w_out.astype(BF16), cd_w_out.astype(BF16)
    ffn_g, ffn_u, ffn_d = (t.astype(BF16) for t in (ffn_w_gate, ffn_w_up, ffn_w_down))
    moe_g, moe_u, moe_d = (t.astype(BF16) for t in (moe_w_gate, moe_w_up, moe_w_down))

    def trunk(x, mods, batch, length, states, pos0):
        lru_h, lru_conv, gdn_s, gdn_conv, rwkv_s, rwkv_shift, m_c, m_n, m_m = states
        new = [[] for _ in range(9)]
        seq = lambda t: t.reshape(batch, length, t.shape[-1])
        tok = lambda t: t.reshape(x.shape[0], x.shape[1], t.shape[-1])
        for l in range(DEPTH):
            j = l // 2
            sh1, sc1, g1, sh2, sc2, g2 = mods[l]
            if l % 2 == 0:
                xy, qkv, z, gates = _inproj(x, sc1, sh1, ab_in[j], AB_SPLITS)
                out_a, s0, s1 = _lru_mixer(seq(xy), lru_conv[j], lru_h[j], lru_conv_w[j], lru_conv_b[j],
                                           lru_wr[j], lru_br[j], lru_wi[j], lru_bi[j], lru_lambda[j], pos0)
                out_b, s2, s3 = _gdn_mixer(seq(qkv), seq(z), seq(gates), gdn_conv[j], gdn_s[j],
                                           gdn_conv_w[j], gdn_a_log[j], gdn_dt_bias[j], gdn_norm_w[j])
                for slot, s in zip((0, 1, 2, 3), (s0, s1, s2, s3)):
                    new[slot].append(s)
                x = _outproj_ln(x, tok(out_a), tok(out_b), g1, ab_out[j], ln1_g[l], ln1_b[l])
                x = _ffn_ln(x, sc2, sh2, g2, ffn_g[j], ffn_u[j], ffn_d[j], ln2_g[l], ln2_b[l])
            else:
                rw, mp, gates = _inproj(x, sc1, sh1, cd_in[j], CD_SPLITS)
                out_c, s0 = _rwkv_mixer(seq(rw), rwkv_shift[j], rwkv_s[j], rwkv_mix[j], rwkv_w0[j],
                                        rwkv_w2[j], rwkv_a0[j], rwkv_a2[j], rwkv_g2[j], rwkv_k_k[j],
                                        rwkv_k_a[j], rwkv_r_k[j], rwkv_ln_w[j], rwkv_ln_b[j])
                out_d, s2, s3, s4 = _mlstm_mixer(seq(mp), seq(gates), m_c[j], m_n[j], m_m[j],
                                                 mlstm_i_b[j], mlstm_f_b[j], mlstm_norm_w[j])
                for slot, s in zip((4, 5, 6, 7, 8), (s0, seq(rw)[:, -1], s2, s3, s4)):
                    new[slot].append(s)
                x = _outproj_ln(x, tok(out_c), tok(out_d), g1, cd_out[j], ln1_g[l], ln1_b[l])
                x = _moe_ln(x, sc2, sh2, g2, router_w[j], moe_g[j], moe_u[j], moe_d[j],
                            ln2_g[l], ln2_b[l])
        return x, tuple(jnp.stack(s) for s in new)

    def zeros(ref):
        return jnp.zeros((ref.shape[0], bp) + ref.shape[2:], F32)

    mods_p = [[mod[l, :bp, k * d:(k + 1) * d].reshape(bp, 1, d) for k in range(6)] for l in range(DEPTH)]
    mods_s = [[mod[l, bp:, k * d:(k + 1) * d].reshape(1, bs * ls, d) for k in range(6)] for l in range(DEPTH)]
    states_s = (state_lru_h, state_lru_conv, state_gdn_S, state_gdn_conv, state_rwkv_S,
                state_rwkv_shift, state_mlstm_C, state_mlstm_n, state_mlstm_m)
    y_p, new_p = trunk(x_prompt, mods_p, bp, lp, tuple(zeros(s) for s in states_s), 0)
    y_s, new_s = trunk(x_sample.reshape(1, bs * ls, d), mods_s, bs, ls, states_s, PAST_LEN)
    out = [y_p, y_s.reshape(bs, ls, d)]
    for p_leaf, s_leaf in zip(new_p, new_s):
        out += [p_leaf, s_leaf]
    return tuple(out)
```

```python
import functools
import math

import jax
import jax.numpy as jnp
from jax import lax
from jax.experimental import pallas as pl
from jax.experimental.pallas import tpu as pltpu

F32 = jnp.float32
BF16 = jnp.bfloat16

D_MODEL = 1024
DEPTH = 4
PAST_LEN = 16384
CONV_W = 4
LRU_W = D_MODEL // 2
LRU_BLOCKS = 8
LRU_BW = LRU_W // LRU_BLOCKS
LRU_C = 8.0
GDN_H = D_MODEL // 256
GDN_DK = 128
GDN_DV = 128
GDN_QK = GDN_H * GDN_DK
GDN_W = GDN_H * GDN_DV
GDN_QKV = 2 * GDN_QK + GDN_W
RWKV_HD = 64
RWKV_H = D_MODEL // 2 // RWKV_HD
RWKV_W = RWKV_H * RWKV_HD
RWKV_RW = 64
RWKV_RA = 64
RWKV_RG = 128
RWKV_PROJ_W = 3 * RWKV_W + RWKV_RW + RWKV_RA + RWKV_RG
RWKV_GN_EPS = 64e-5
MLSTM_H = D_MODEL // 256
MLSTM_DK = 128
MLSTM_DV = 128
MLSTM_QK = MLSTM_H * MLSTM_DK
MLSTM_W = MLSTM_H * MLSTM_DV
D_FF = 7 * D_MODEL // 2
N_EXPERTS = 8
LN_EPS = 1e-5
NEG_BIG = -1e30
DN_ALPHA = (2.0 * DEPTH) ** 0.25

LANES = 128
SUBLANES = 8
SEQ_CHUNK = 64
SHORT_SEQ_BLOCK = 8
ROW_TILE = 512
FF_TILE = 512
VMEM_LIMIT = 48 * 1024 * 1024


def _cparams(*sem):
    return pltpu.CompilerParams(dimension_semantics=sem, vmem_limit_bytes=VMEM_LIMIT)


def _dot(a, b):
    return jnp.dot(a.astype(BF16), b.astype(BF16), preferred_element_type=F32)


def _dot_nt(a, b):
    return lax.dot_general(a.astype(BF16), b.astype(BF16), (((1,), (1,)), ((), ())),
                           preferred_element_type=F32)


def _dot_tn(a, b):
    return lax.dot_general(a.astype(BF16), b.astype(BF16), (((0,), (0,)), ((), ())),
                           preferred_element_type=F32)


def _split3(x):
    hi = x.astype(BF16)
    r1 = x - hi.astype(F32)
    mid = r1.astype(BF16)
    lo = (r1 - mid.astype(F32)).astype(BF16)
    return hi, mid, lo


def _dot_sel_lhs(t, x):
    tb = jnp.where(t, 1.0, 0.0).astype(BF16)
    hi, mid, lo = _split3(x)
    d = lambda p: jnp.dot(tb, p, preferred_element_type=F32)
    return d(hi) + d(mid) + d(lo)


def _dot_sel_rhs(x, t):
    tb = jnp.where(t, 1.0, 0.0).astype(BF16)
    hi, mid, lo = _split3(x)
    d = lambda p: jnp.dot(p, tb, preferred_element_type=F32)
    return d(hi) + d(mid) + d(lo)


def _dot2(p, x):
    pb = p.astype(BF16)
    xh = x.astype(BF16)
    xl = (x - xh.astype(F32)).astype(BF16)
    return (jnp.dot(pb, xh, preferred_element_type=F32)
            + jnp.dot(pb, xl, preferred_element_type=F32))


def _sigmoid(x):
    return 1.0 / (1.0 + jnp.exp(-x))


def _silu(x):
    return x * _sigmoid(x)


def _softplus(x):
    return jnp.maximum(x, 0.0) + jnp.log1p(jnp.exp(-jnp.abs(x)))


def _gelu_tanh(x):
    return 0.5 * x * (1.0 + jnp.tanh(math.sqrt(2.0 / math.pi) * (x + 0.044715 * (x * x * x))))


def _ln(y, g=None, b=None, eps=LN_EPS):
    mu = jnp.mean(y, axis=-1, keepdims=True)
    d = y - mu
    var = jnp.mean(d * d, axis=-1, keepdims=True)
    out = d * lax.rsqrt(var + eps)
    if g is not None:
        out = out * g
    if b is not None:
        out = out + b
    return out


def _l2norm(x, eps=1e-6):
    return x * lax.rsqrt(jnp.sum(x * x, axis=-1, keepdims=True) + eps)


def _tri(c, inclusive):
    t = lax.broadcasted_iota(jnp.int32, (c, c), 0)
    s = lax.broadcasted_iota(jnp.int32, (c, c), 1)
    return (s <= t) if inclusive else (s < t)


def _unit_lower_solve(ns, xs, c):
    steps = max(1, int(math.ceil(math.log2(c))))
    for i in range(steps):
        xs = [x + _dot2(p, x) for p, x in zip(ns, xs)]
        if i + 1 < steps:
            ns = [_dot(p, p) for p in ns]
    return xs


def _dplr_heads(heads, states, c, nv, state_is_vk):
    incl = _tri(c, True)
    strict = _tri(c, False)
    ms = [_dot_nt(h["m_lhs"], h["m_rhs"]) for h in heads]
    if state_is_vk:
        xhs = [_dot_nt(h["x_lhs"], s) for h, s in zip(heads, states)]
    else:
        xhs = [_dot(h["x_lhs"], s) for h, s in zip(heads, states)]
    a_abs, a_aks, r_bks = [], [], []
    for h, m in zip(heads, ms):
        if "pair_x" in h:
            a_abs.append(m[:c, :c] * h["pair_x"])
            a_aks.append(m[:c, c:] * h["pair_x"])
            r_bks.append(jnp.concatenate([m[c:, :c] * h["pair_i"], m[c:, c:] * h["pair_i"]], axis=1))
        else:
            a_abs.append(jnp.where(strict, m[:c, :c], 0.0))
            a_aks.append(jnp.where(strict, m[:c, c:], 0.0))
            r_bks.append(jnp.concatenate([jnp.where(incl, m[c:, :c], 0.0),
                                          jnp.where(incl, m[c:, c:], 0.0)], axis=1))
    if nv == 1:
        us = [xh[:c] for xh in xhs]
    else:
        rhs = [xh[:c] + _dot(a_ak, h["v"]) for xh, a_ak, h in zip(xhs, a_aks, heads)]
        us = _unit_lower_solve(a_abs, rhs, c)
    uvs = [jnp.concatenate([u, h["v"]], axis=0) for u, h in zip(us, heads)]
    outs = [xh[c:] + _dot(r_bk, uv) for xh, r_bk, uv in zip(xhs, r_bks, uvs)]
    if state_is_vk:
        new = [s * h["s_decay"] + _dot_tn(uv, h["bk"]) for s, h, uv in zip(states, heads, uvs)]
    else:
        new = [s * h["s_decay"] + _dot_tn(h["bk"], uv) for s, h, uv in zip(states, heads, uvs)]
    return outs, new


def _valid_rows(c, n_valid):
    return lax.broadcasted_iota(jnp.int32, (c, 1), 0) < n_valid


def _mod_kernel(c_ref, w_ref, b_ref, o_ref):
    o_ref[0] = _dot(_silu(c_ref[...]), w_ref[0]) + b_ref[0]


def _modulation(c_all, mod_w, mod_b):
    n = c_all.shape[0]
    d = D_MODEL
    return pl.pallas_call(
        _mod_kernel,
        grid=(DEPTH, 6),
        in_specs=[pl.BlockSpec((n, d), lambda l, j: (0, 0)),
                  pl.BlockSpec((1, d, d), lambda l, j: (l, 0, j)),
                  pl.BlockSpec((1, 1, d), lambda l, j: (l, 0, j))],
        out_specs=pl.BlockSpec((1, n, d), lambda l, j: (l, 0, j)),
        out_shape=jax.ShapeDtypeStruct((DEPTH, n, 6 * d), F32),
        compiler_params=_cparams("parallel", "parallel"),
        name="modulation",
    )(c_all, mod_w, mod_b.reshape(DEPTH, 1, 6 * d))


def _mod_spec(mod, tm):
    if mod.shape[1] == 1:
        return pl.BlockSpec((1, 1, mod.shape[2]), lambda g, i, *_: (g, 0, 0))
    return pl.BlockSpec((1, tm, mod.shape[2]), lambda g, i, *_: (g, i, 0))


def _inproj_kernel(x_ref, sc_ref, sh_ref, w_ref, *o_refs, splits):
    h = (x_ref[0] * (1.0 + sc_ref[0]) + sh_ref[0]).astype(BF16)
    for o_ref, (s, n) in zip(o_refs, splits):
        o_ref[0] = jnp.dot(h, w_ref[:, s:s + n], preferred_element_type=F32)


def _inproj(x, sc, sh, w, splits):
    g, r, d = x.shape
    tm = min(r, ROW_TILE)
    n_all = w.shape[1]
    return pl.pallas_call(
        functools.partial(_inproj_kernel, splits=splits),
        grid=(g, r // tm),
        in_specs=[pl.BlockSpec((1, tm, d), lambda g, i: (g, i, 0)),
                  _mod_spec(sc, tm), _mod_spec(sh, tm),
                  pl.BlockSpec((d, n_all), lambda g, i: (0, 0))],
        out_specs=[pl.BlockSpec((1, tm, n), lambda g, i: (g, i, 0)) for _, n in splits],
        out_shape=[jax.ShapeDtypeStruct((g, r, n), F32) for _, n in splits],
        compiler_params=_cparams("parallel", "parallel"),
        name="inproj",
    )(x, sc, sh, w)


def _outproj_ln_kernel(x_ref, ma_ref, mb_ref, gate_ref, w_ref, lng_ref, lnb_ref, o_ref):
    half = ma_ref.shape[2]
    f = _dot(ma_ref[0], w_ref[0:half, :]) + _dot(mb_ref[0], w_ref[half:, :])
    y = DN_ALPHA * x_ref[0] + (1.0 + gate_ref[0]) * f
    o_ref[0] = _ln(y, lng_ref[...], lnb_ref[...])


def _outproj_ln(x, mix_a, mix_b, gate, w, ln_g, ln_b):
    g, r, d = x.shape
    tm = min(r, ROW_TILE)
    half = mix_a.shape[2]
    row = pl.BlockSpec((1, tm, d), lambda g, i: (g, i, 0))
    mrow = pl.BlockSpec((1, tm, half), lambda g, i: (g, i, 0))
    vec = pl.BlockSpec((1, d), lambda g, i: (0, 0))
    return pl.pallas_call(
        _outproj_ln_kernel,
        grid=(g, r // tm),
        in_specs=[row, mrow, mrow, _mod_spec(gate, tm),
                  pl.BlockSpec((2 * half, d), lambda g, i: (0, 0)), vec, vec],
        out_specs=row,
        out_shape=jax.ShapeDtypeStruct((g, r, d), F32),
        compiler_params=_cparams("parallel", "parallel"),
        name="outproj_ln",
    )(x, mix_a, mix_b, gate, w, ln_g.reshape(1, d), ln_b.reshape(1, d))


def _ffn_kernel(x_ref, sc_ref, sh_ref, gate_ref, wg_ref, wu_ref, wd_ref, lng_ref, lnb_ref,
                o_ref, h_scr, acc_scr):
    j = pl.program_id(2)

    @pl.when(j == 0)
    def _():
        h_scr[...] = (x_ref[0] * (1.0 + sc_ref[0]) + sh_ref[0]).astype(BF16)
        acc_scr[...] = jnp.zeros_like(acc_scr)

    h = h_scr[...]
    a = jnp.dot(h, wg_ref[...], preferred_element_type=F32)
    u = jnp.dot(h, wu_ref[...], preferred_element_type=F32)
    acc_scr[...] += _dot(_silu(a) * u, wd_ref[...])

    @pl.when(j == pl.num_programs(2) - 1)
    def _():
        y = DN_ALPHA * x_ref[0] + (1.0 + gate_ref[0]) * acc_scr[...]
        o_ref[0] = _ln(y, lng_ref[...], lnb_ref[...])


def _ffn_ln(x, sc, sh, gate, wg, wu, wd, ln_g, ln_b):
    g, r, d = x.shape
    tm = min(r, ROW_TILE)
    f = wg.shape[1]
    row = pl.BlockSpec((1, tm, d), lambda g, i, j: (g, i, 0))
    vec = pl.BlockSpec((1, d), lambda g, i, j: (0, 0))
    return pl.pallas_call(
        _ffn_kernel,
        grid=(g, r // tm, f // FF_TILE),
        in_specs=[row, _mod_spec(sc, tm), _mod_spec(sh, tm), _mod_spec(gate, tm),
                  pl.BlockSpec((d, FF_TILE), lambda g, i, j: (0, j)),
                  pl.BlockSpec((d, FF_TILE), lambda g, i, j: (0, j)),
                  pl.BlockSpec((FF_TILE, d), lambda g, i, j: (j, 0)), vec, vec],
        out_specs=row,
        out_shape=jax.ShapeDtypeStruct((g, r, d), F32),
        scratch_shapes=[pltpu.VMEM((tm, d), BF16), pltpu.VMEM((tm, d), F32)],
        compiler_params=_cparams("parallel", "parallel", "arbitrary"),
        name="ffn_ln",
    )(x, sc, sh, gate, wg, wu, wd, ln_g.reshape(1, d), ln_b.reshape(1, d))


def _moe_kernel(x_ref, sc_ref, sh_ref, gate_ref, rw_ref, wg_ref, wu_ref, wd_ref, lng_ref, lnb_ref,
                o_ref, h_scr, comb_scr, acc_scr):
    e = pl.program_id(2)
    j = pl.program_id(3)
    lane = lax.broadcasted_iota(jnp.int32, comb_scr.shape, 1)

    @pl.when((e == 0) & (j == 0))
    def _():
        h = x_ref[0] * (1.0 + sc_ref[0]) + sh_ref[0]
        h_scr[...] = h.astype(BF16)
        acc_scr[...] = jnp.zeros_like(acc_scr)
        logits = jnp.dot(h, rw_ref[...], preferred_element_type=F32, precision=lax.Precision.HIGHEST)
        logits = jnp.where(lane < N_EXPERTS, logits, -jnp.inf)
        m1 = jnp.max(logits, axis=-1, keepdims=True)
        i1 = jnp.min(jnp.where(logits == m1, lane, LANES), axis=-1, keepdims=True)
        rest = jnp.where(lane == i1, -jnp.inf, logits)
        m2 = jnp.max(rest, axis=-1, keepdims=True)
        i2 = jnp.min(jnp.where(rest == m2, lane, LANES), axis=-1, keepdims=True)
        e2 = jnp.exp(m2 - m1)
        g1 = 1.0 / (1.0 + e2)
        g2 = e2 / (1.0 + e2)
        comb_scr[...] = jnp.where(lane == i1, g1, 0.0) + jnp.where(lane == i2, g2, 0.0)

    h = h_scr[...]
    a = jnp.dot(h, wg_ref[0], preferred_element_type=F32)
    u = jnp.dot(h, wu_ref[0], preferred_element_type=F32)
    comb_e = jnp.sum(jnp.where(lane == e, comb_scr[...], 0.0), axis=-1, keepdims=True)
    acc_scr[...] += comb_e * _dot(_silu(a) * u, wd_ref[0])

    @pl.when((e == pl.num_programs(2) - 1) & (j == pl.num_programs(3) - 1))
    def _():
        y = DN_ALPHA * x_ref[0] + (1.0 + gate_ref[0]) * acc_scr[...]
        o_ref[0] = _ln(y, lng_ref[...], lnb_ref[...])


def _moe_ln(x, sc, sh, gate, router_w, wg, wu, wd, ln_g, ln_b):
    g, r, d = x.shape
    tm = min(r, ROW_TILE)
    ne, _, f = wg.shape
    row = pl.BlockSpec((1, tm, d), lambda g, i, e, j: (g, i, 0))
    vec = pl.BlockSpec((1, d), lambda g, i, e, j: (0, 0))
    rw = jnp.pad(router_w, ((0, 0), (0, LANES - ne)))
    return pl.pallas_call(
        _moe_kernel,
        grid=(g, r // tm, ne, f // FF_TILE),
        in_specs=[row, _mod_spec(sc, tm), _mod_spec(sh, tm), _mod_spec(gate, tm),
                  pl.BlockSpec((d, LANES), lambda g, i, e, j: (0, 0)),
                  pl.BlockSpec((1, d, FF_TILE), lambda g, i, e, j: (e, 0, j)),
                  pl.BlockSpec((1, d, FF_TILE), lambda g, i, e, j: (e, 0, j)),
                  pl.BlockSpec((1, FF_TILE, d), lambda g, i, e, j: (e, j, 0)), vec, vec],
        out_specs=row,
        out_shape=jax.ShapeDtypeStruct((g, r, d), F32),
        scratch_shapes=[pltpu.VMEM((tm, d), BF16), pltpu.VMEM((tm, LANES), F32),
                        pltpu.VMEM((tm, d), F32)],
        compiler_params=_cparams("parallel", "parallel", "arbitrary", "arbitrary"),
        name="moe_ln",
    )(x, sc, sh, gate, rw, wg, wu, wd, ln_g.reshape(1, d), ln_b.reshape(1, d))


def _seq_dims(b, l):
    if l >= SEQ_CHUNK:
        assert l % SEQ_CHUNK == 0
        return SEQ_CHUNK, l // SEQ_CHUNK, SEQ_CHUNK, 1
    assert l <= SUBLANES and b % SHORT_SEQ_BLOCK == 0
    return SUBLANES, 1, l, SHORT_SEQ_BLOCK


def _pad_seq(t, c, n):
    pad = c * n - t.shape[1]
    return t if pad == 0 else jnp.pad(t, ((0, 0), (0, pad), (0, 0)))


def _rows_layout(t, c, n, width):
    b = t.shape[0]
    return jnp.swapaxes(t[:, :, :width].reshape(b, n, c, width), 2, 3)


def _conv_window(scr, u, taps, c):
    scr[SUBLANES:SUBLANES + c, :] = u
    out = scr[5:5 + c, :] * taps[0:1, :]
    for j in range(1, CONV_W):
        out = out + scr[5 + j:5 + j + c, :] * taps[j:j + 1, :]
    return out


def _seq_spec(bb, c, width):
    return pl.BlockSpec((bb, c, width), lambda b, i: (b, i, 0))


def _state_spec(bb, shape):
    return pl.BlockSpec((bb,) + shape, lambda b, i: (b,) + (0,) * len(shape))


def _const_spec(shape):
    return pl.BlockSpec(shape, lambda b, i: (0,) * len(shape))


def _lru_kernel(xy_ref, buf_ref, h0_ref, cw_ref, cb_ref, wg_ref, bg_ref, lam_ref,
                out_ref, hnew_ref, bufnew_ref, xs_scr, a_scr, b_scr, h_scr, *, bb, c, nv, pos0):
    i = pl.program_id(1)
    w = LRU_W

    @pl.when(i == 0)
    def _():
        xs_scr[:, 0:SUBLANES, :] = jnp.zeros((bb, SUBLANES, w), F32)
        xs_scr[:, 5:8, :] = buf_ref[...]
        h_scr[...] = h0_ref[...]

    pos = lax.broadcasted_iota(jnp.int32, (c, 1), 0) + (i * c + pos0)
    for bi in range(bb):
        xc = _conv_window(xs_scr.at[bi], xy_ref[bi, :, 0:w], cw_ref[...], c) + cb_ref[...]
        gates = _dot(xc, wg_ref[...]) + bg_ref[...]
        r = _sigmoid(gates[:, 0:w])
        ig = _sigmoid(gates[:, w:2 * w])
        log_a = -LRU_C * r * _softplus(-lam_ref[...])
        mult = jnp.sqrt(-jnp.tanh(log_a) * (jnp.exp(2.0 * log_a) + 1.0))
        mult = jnp.where(pos == 0, 1.0, mult)
        a_scr[bi] = jnp.exp(log_a)
        b_scr[bi] = xc * ig * mult

    def step(t, h):
        h = a_scr[:, pl.ds(t, 1), :] * h + b_scr[:, pl.ds(t, 1), :]
        b_scr[:, pl.ds(t, 1), :] = h
        return h

    h = lax.fori_loop(0, nv, step, h_scr[...], unroll=min(nv, SUBLANES))
    h_scr[...] = h
    out_ref[...] = b_scr[...] * _gelu_tanh(xy_ref[:, :, w:2 * w])
    tail = xs_scr[:, 5 + nv:8 + nv, :]
    xs_scr[:, 5:8, :] = tail

    @pl.when(i == pl.num_programs(1) - 1)
    def _():
        hnew_ref[...] = h
        bufnew_ref[...] = tail


def _lru_mixer(xy, buf, h0, conv_w, conv_b, wr, br, wi, bi, lam, pos0):
    b, l, _ = xy.shape
    c, n, nv, bb = _seq_dims(b, l)
    w = LRU_W
    bd = lambda m: jax.scipy.linalg.block_diag(*[m[i] for i in range(LRU_BLOCKS)])
    wgate = jnp.concatenate([bd(wr), bd(wi)], axis=1).astype(BF16)
    bgate = jnp.concatenate([br, bi]).reshape(1, 2 * w)
    out, h_new, buf_new = pl.pallas_call(
        functools.partial(_lru_kernel, bb=bb, c=c, nv=nv, pos0=pos0),
        grid=(b // bb, n),
        in_specs=[_seq_spec(bb, c, 2 * w), _state_spec(bb, (CONV_W - 1, w)), _state_spec(bb, (1, w)),
                  _const_spec((CONV_W, w)), _const_spec((1, w)), _const_spec((w, 2 * w)),
                  _const_spec((1, 2 * w)), _const_spec((1, w))],
        out_specs=[_seq_spec(bb, c, w), _state_spec(bb, (1, w)), _state_spec(bb, (CONV_W - 1, w))],
        out_shape=[jax.ShapeDtypeStruct((b, n * c, w), F32),
                   jax.ShapeDtypeStruct((b, 1, w), F32),
                   jax.ShapeDtypeStruct((b, CONV_W - 1, w), F32)],
        scratch_shapes=[pltpu.VMEM((bb, SUBLANES + c, w), F32), pltpu.VMEM((bb, c, w), F32),
                        pltpu.VMEM((bb, c, w), F32), pltpu.VMEM((bb, 1, w), F32)],
        compiler_params=_cparams("parallel", "arbitrary"),
        name="rglru",
    )(_pad_seq(xy, c, n), buf, h0.reshape(b, 1, w), conv_w, conv_b.reshape(1, w), wgate, bgate,
      lam.reshape(1, w))
    return out[:, :l], h_new.reshape(b, w), buf_new


def _gdn_kernel(qkv_ref, z_ref, gcol_ref, grow_ref, buf_ref, s0_ref, cw_ref, pcol_ref, prow_ref,
                nw_ref, out_ref, snew_ref, bufnew_ref, xs_scr, s_scr, *, bb, c, nv):
    i = pl.program_id(1)
    qk = GDN_QK

    @pl.when(i == 0)
    def _():
        xs_scr[:, 0:SUBLANES, :] = jnp.zeros((bb, SUBLANES, GDN_QKV), F32)
        xs_scr[:, 5:8, :] = buf_ref[...]
        s_scr[...] = s0_ref[...]

    valid = _valid_rows(c, nv)
    valid_r = lax.broadcasted_iota(jnp.int32, (1, c), 1) < nv
    alog_c, dtb_c = pcol_ref[0:1, :], pcol_ref[1:2, :]
    alog_r, dtb_r = prow_ref[:, 0:1], prow_ref[:, 1:2]
    incl, strict = _tri(c, True), _tri(c, False)
    upper = jnp.logical_not(strict)
    heads = []
    for bi in range(bb):
        x = _silu(_conv_window(xs_scr.at[bi], qkv_ref[bi], cw_ref[...], c))
        gcol = gcol_ref[bi]
        grow = grow_ref[bi, 0]
        g_col = jnp.where(valid, -jnp.exp(alog_c) * _softplus(gcol[:, 0:GDN_H] + dtb_c), 0.0)
        beta = jnp.where(valid, _sigmoid(gcol[:, GDN_H:2 * GDN_H]), 0.0)
        g_row = jnp.where(valid_r, -jnp.exp(alog_r) * _softplus(grow[0:GDN_H, :] + dtb_r), 0.0)
        gi_cols = _dot_sel_lhs(incl, g_col)
        gi_rows = _dot_sel_rhs(g_row, upper)
        for h in range(GDN_H):
            q = _l2norm(x[:, h * GDN_DK:(h + 1) * GDN_DK]) * (GDN_DK ** -0.5)
            k = _l2norm(x[:, qk + h * GDN_DK:qk + (h + 1) * GDN_DK])
            gh = g_col[:, h:h + 1]
            gi = gi_cols[:, h:h + 1]
            gx = gi - gh
            gi_row = gi_rows[h:h + 1, :]
            g_end = gi[c - 1:c, :]
            kb = k * beta[:, h:h + 1]
            b = -jnp.exp(gh) * kb
            e_end = jnp.exp(g_end - gi)
            heads.append(dict(
                m_lhs=jnp.concatenate([k, q], axis=0), m_rhs=jnp.concatenate([b, kb], axis=0),
                pair_x=jnp.where(strict, jnp.exp(jnp.where(strict, gx - gi_row, 0.0)), 0.0),
                pair_i=jnp.where(incl, jnp.exp(jnp.where(incl, gi - gi_row, 0.0)), 0.0),
                x_lhs=jnp.concatenate([k * jnp.exp(gx), q * jnp.exp(gi)], axis=0),
                v=x[:, 2 * qk + h * GDN_DV:2 * qk + (h + 1) * GDN_DV],
                bk=jnp.concatenate([b * e_end, kb * e_end], axis=0), s_decay=jnp.exp(g_end)))
    pairs = [(bi, h) for bi in range(bb) for h in range(GDN_H)]
    outs, new_states = _dplr_heads(heads, [s_scr[bi, h] for bi, h in pairs], c, nv, state_is_vk=False)
    for (bi, h), o, s_new in zip(pairs, outs, new_states):
        s_scr[bi, h] = s_new
        zh = z_ref[bi, :, h * GDN_DV:(h + 1) * GDN_DV]
        o = o * lax.rsqrt(jnp.mean(o * o, axis=-1, keepdims=True) + 1e-6) * nw_ref[...]
        out_ref[bi, :, h * GDN_DV:(h + 1) * GDN_DV] = o * _silu(zh)
    tail = xs_scr[:, 5 + nv:8 + nv, :]
    xs_scr[:, 5:8, :] = tail

    @pl.when(i == pl.num_programs(1) - 1)
    def _():
        snew_ref[...] = s_scr[...]
        bufnew_ref[...] = tail


def _gdn_mixer(qkv, z, gates, buf, s0, conv_w, a_log, dt_bias, norm_w):
    b, l, _ = qkv.shape
    c, n, nv, bb = _seq_dims(b, l)
    gates = _pad_seq(gates, c, n)
    pcol = jnp.stack([a_log, dt_bias])
    out, s_new, buf_new = pl.pallas_call(
        functools.partial(_gdn_kernel, bb=bb, c=c, nv=nv),
        grid=(b // bb, n),
        in_specs=[_seq_spec(bb, c, GDN_QKV), _seq_spec(bb, c, GDN_W), _seq_spec(bb, c, LANES),
                  pl.BlockSpec((bb, 1, SUBLANES, c), lambda b, i: (b, i, 0, 0)),
                  _state_spec(bb, (CONV_W - 1, GDN_QKV)), _state_spec(bb, (GDN_H, GDN_DK, GDN_DV)),
                  _const_spec((CONV_W, GDN_QKV)), _const_spec((2, GDN_H)), _const_spec((GDN_H, 2)),
                  _const_spec((1, GDN_DV))],
        out_specs=[_seq_spec(bb, c, GDN_W), _state_spec(bb, (GDN_H, GDN_DK, GDN_DV)),
                   _state_spec(bb, (CONV_W - 1, GDN_QKV))],
        out_shape=[jax.ShapeDtypeStruct((b, n * c, GDN_W), F32),
                   jax.ShapeDtypeStruct((b, GDN_H, GDN_DK, GDN_DV), F32),
                   jax.ShapeDtypeStruct((b, CONV_W - 1, GDN_QKV), F32)],
        scratch_shapes=[pltpu.VMEM((bb, SUBLANES + c, GDN_QKV), F32),
                        pltpu.VMEM((bb, GDN_H, GDN_DK, GDN_DV), F32)],
        compiler_params=_cparams("parallel", "arbitrary"),
        name="gdn",
    )(_pad_seq(qkv, c, n), _pad_seq(z, c, n), gates, _rows_layout(gates, c, n, SUBLANES), buf, s0,
      conv_w, pcol, pcol.T, norm_w.reshape(1, GDN_DV))
    return out[:, :l], s_new, buf_new


def _rwkv_kernel(rw_ref, prev_ref, s0_ref, mix_ref, w0_ref, w2_ref, a0_ref, a2_ref, g2_ref,
                 kk_ref, ka_ref, rk_ref, lnw_ref, lnb_ref, out_ref, snew_ref, xs_scr, s_scr,
                 *, bb, c, nv):
    i = pl.program_id(1)
    hd = RWKV_HD
    w = RWKV_W

    @pl.when(i == 0)
    def _():
        xs_scr[:, 0:SUBLANES, :] = jnp.zeros((bb, SUBLANES, RWKV_PROJ_W), F32)
        xs_scr[:, 7:8, :] = prev_ref[...]
        s_scr[...] = s0_ref[...]

    valid = _valid_rows(c, nv)
    incl = _tri(c, True)
    heads, post = [], []
    for bi in range(bb):
        rw = rw_ref[bi]
        xs_scr[bi, SUBLANES:SUBLANES + c, :] = rw
        prev = xs_scr[bi, 7:7 + c, :]
        xs_scr[bi, 7:8, :] = xs_scr[bi, 7 + nv:8 + nv, :]
        xs = rw + (prev - rw) * mix_ref[...]
        r_all, k_all, v_all = xs[:, 0:w], xs[:, w:2 * w], xs[:, 2 * w:3 * w]
        o1 = 3 * w
        wl = xs[:, o1:o1 + RWKV_RW]
        al = xs[:, o1 + RWKV_RW:o1 + RWKV_RW + RWKV_RA]
        gl = xs[:, o1 + RWKV_RW + RWKV_RA:]
        wdec = -_softplus(-(w0_ref[...] + _dot(jnp.tanh(wl), w2_ref[...]))) - 0.5
        a_all = _sigmoid(a0_ref[...] + _dot(al, a2_ref[...]))
        gate = _dot(_sigmoid(gl), g2_ref[...])
        lw_all = jnp.where(valid, -jnp.exp(wdec), 0.0)
        kmod = jnp.where(valid, k_all * (1.0 + (a_all - 1.0) * ka_ref[...]), 0.0)
        kk_all = k_all * kk_ref[...]
        gi = _dot_sel_lhs(incl, lw_all)
        gm = gi[c // 2:c // 2 + 1, :]
        g_end = gi[c - 1:c, :]
        e_nlw = jnp.exp(-lw_all)
        e_r = jnp.exp(gi - gm)
        e_a = e_r * e_nlw
        e_m = jnp.exp(gm - gi)
        e_gi = jnp.exp(gi)
        e_gx = e_gi * e_nlw
        e_end = jnp.exp(g_end - gm) * e_m
        s_dec = jnp.exp(g_end)
        for h in range(RWKV_H):
            sl = slice(h * hd, (h + 1) * hd)
            kk = _l2norm(kk_all[:, sl])
            r, k = r_all[:, sl], kmod[:, sl]
            a = -kk
            b = jnp.where(valid, kk * a_all[:, sl], 0.0)
            heads.append(dict(
                m_lhs=jnp.concatenate([a * e_a[:, sl], r * e_r[:, sl]], axis=0),
                m_rhs=jnp.concatenate([b * e_m[:, sl], k * e_m[:, sl]], axis=0),
                x_lhs=jnp.concatenate([a * e_gx[:, sl], r * e_gi[:, sl]], axis=0),
                v=v_all[:, sl],
                bk=jnp.concatenate([b * e_end[:, sl], k * e_end[:, sl]], axis=0), s_decay=s_dec[:, sl]))
            post.append((bi, h, r, k, v_all[:, sl], gate[:, sl]))
    outs, new_states = _dplr_heads(heads, [s_scr[bi, h] for bi, h, *_ in post], c, nv, state_is_vk=True)
    for (bi, h, r, k, v, gate_h), o, s_new in zip(post, outs, new_states):
        sl = slice(h * hd, (h + 1) * hd)
        s_scr[bi, h] = s_new
        y = _ln(o, lnw_ref[:, sl], lnb_ref[:, sl], RWKV_GN_EPS)
        y = y + jnp.sum(r * k * rk_ref[:, sl], axis=-1, keepdims=True) * v
        out_ref[bi, :, sl] = y * gate_h

    @pl.when(i == pl.num_programs(1) - 1)
    def _():
        snew_ref[...] = s_scr[...]


def _rwkv_mixer(rw, shift0, s0, mix, w0, w2, a0, a2, g2, k_k, k_a, r_k, ln_w, ln_b):
    b, l, _ = rw.shape
    c, n, nv, bb = _seq_dims(b, l)
    w = RWKV_W
    row = lambda t: t.reshape(1, -1)
    state = (RWKV_H, RWKV_HD, RWKV_HD)
    out, s_new = pl.pallas_call(
        functools.partial(_rwkv_kernel, bb=bb, c=c, nv=nv),
        grid=(b // bb, n),
        in_specs=[_seq_spec(bb, c, RWKV_PROJ_W), _state_spec(bb, (1, RWKV_PROJ_W)), _state_spec(bb, state),
                  _const_spec((1, RWKV_PROJ_W)), _const_spec((1, w)), _const_spec((RWKV_RW, w)),
                  _const_spec((1, w)), _const_spec((RWKV_RA, w)), _const_spec((RWKV_RG, w)),
                  _const_spec((1, w)), _const_spec((1, w)), _const_spec((1, w)), _const_spec((1, w)),
                  _const_spec((1, w))],
        out_specs=[_seq_spec(bb, c, w), _state_spec(bb, state)],
        out_shape=[jax.ShapeDtypeStruct((b, n * c, w), F32),
                   jax.ShapeDtypeStruct((b,) + state, F32)],
        scratch_shapes=[pltpu.VMEM((bb, SUBLANES + c, RWKV_PROJ_W), F32), pltpu.VMEM((bb,) + state, F32)],
        compiler_params=_cparams("parallel", "arbitrary"),
        name="rwkv7",
    )(_pad_seq(rw, c, n), shift0.reshape(b, 1, RWKV_PROJ_W), s0, row(mix), row(w0),
      w2.astype(BF16), row(a0), a2.astype(BF16), g2.astype(BF16), row(k_k), row(k_a), row(r_k),
      row(ln_w), row(ln_b))
    return out[:, :l], s_new


def _mlstm_kernel(p_ref, gcol_ref, grow_ref, c0_ref, n0_ref, m0_ref, bcol_ref, brow_ref, nw_ref,
                  out_ref, cnew_ref, nnew_ref, mnew_ref, c_scr, n_scr, m_scr, *, bb, c, nv):
    i = pl.program_id(1)
    nh, dk, dv = MLSTM_H, MLSTM_DK, MLSTM_DV

    @pl.when(i == 0)
    def _():
        c_scr[...] = c0_ref[...]
        n_scr[...] = n0_ref[...]
        m_scr[...] = m0_ref[...]

    valid = _valid_rows(c, nv)
    valid_r = lax.broadcasted_iota(jnp.int32, (1, c), 1) < nv
    incl = _tri(c, True)
    upper = jnp.logical_not(_tri(c, False))
    ps = [(bi, h) for bi in range(bb) for h in range(nh)]
    gate_cols = {}
    for bi in range(bb):
        gcol = gcol_ref[bi]
        grow = grow_ref[bi, 0]
        li_col = jnp.where(valid, gcol[:, 0:nh] + bcol_ref[0:1, :], NEG_BIG)
        lf_col = jnp.where(valid, -_softplus(-(gcol[:, nh:2 * nh] + bcol_ref[1:2, :])), 0.0)
        li_row = jnp.where(valid_r, grow[0:nh, :] + brow_ref[:, 0:1], NEG_BIG)
        lf_row = jnp.where(valid_r, -_softplus(-(grow[nh:2 * nh, :] + brow_ref[:, 1:2])), 0.0)
        b_cols = _dot_sel_lhs(incl, lf_col)
        b_rows = _dot_sel_rhs(lf_row, upper)
        gate_cols[bi] = (li_col, li_row, b_cols, b_rows)
    qs = [p_ref[bi, :, h * dk:(h + 1) * dk] for bi, h in ps]
    ks = [p_ref[bi, :, nh * dk + h * dk:nh * dk + (h + 1) * dk] * (dk ** -0.5) for bi, h in ps]
    vs = [p_ref[bi, :, 2 * nh * dk + h * dv:2 * nh * dk + (h + 1) * dv] for bi, h in ps]
    cms = [c_scr[bi, h] for bi, h in ps]
    nvecs = [n_scr[bi, h] for bi, h in ps]
    m_prevs = [m_scr[bi, h] for bi, h in ps]
    qks = [_dot_nt(q, k) for q, k in zip(qs, ks)]
    qcs = [_dot_nt(q, cm) for q, cm in zip(qs, cms)]
    bcs = [gate_cols[bi][2][:, h:h + 1] for bi, h in ps]
    dms = [jnp.where(incl, bc - gate_cols[bi][3][h:h + 1, :] + gate_cols[bi][1][h:h + 1, :], -jnp.inf)
           for (bi, h), bc in zip(ps, bcs)]
    m_inters = [bc + m_prev for bc, m_prev in zip(bcs, m_prevs)]
    m_ts = [jnp.maximum(mi, jnp.max(dm, axis=-1, keepdims=True)) for mi, dm in zip(m_inters, dms)]
    w_inters = [jnp.exp(mi - mt) for mi, mt in zip(m_inters, m_ts)]
    scs = [qk * jnp.exp(dm - mt) for qk, dm, mt in zip(qks, dms, m_ts)]
    scvs = [_dot(sc, v) for sc, v in zip(scs, vs)]
    m_news = [mt[c - 1:c, :] for mt in m_ts]
    b_lasts = [bc[c - 1:c, :] for bc in bcs]
    wss = [jnp.exp(b_last - bc + gate_cols[bi][0][:, h:h + 1] - m_new)
           for (bi, h), b_last, bc, m_new in zip(ps, b_lasts, bcs, m_news)]
    upds = [_dot_tn(v * ws, k) for v, ws, k in zip(vs, wss, ks)]
    for j, (bi, h) in enumerate(ps):
        num = w_inters[j] * qcs[j] + scvs[j]
        den = (w_inters[j] * jnp.sum(qs[j] * nvecs[j], axis=-1, keepdims=True)
               + jnp.sum(scs[j], axis=-1, keepdims=True))
        hh = num / jnp.maximum(jnp.abs(den), jnp.exp(-m_ts[j]))
        dec = jnp.exp(b_lasts[j] + m_prevs[j] - m_news[j])
        c_scr[bi, h] = dec * cms[j] + upds[j]
        n_scr[bi, h] = dec * nvecs[j] + jnp.sum(ks[j] * wss[j], axis=0, keepdims=True)
        m_scr[bi, h] = m_news[j]
        og = p_ref[bi, :, 2 * nh * dk + nh * dv + h * dv:2 * nh * dk + nh * dv + (h + 1) * dv]
        out_ref[bi, :, h * dv:(h + 1) * dv] = _ln(hh, nw_ref[...]) * _sigmoid(og)

    @pl.when(i == pl.num_programs(1) - 1)
    def _():
        cnew_ref[...] = c_scr[...]
        nnew_ref[...] = n_scr[...]
        mnew_ref[...] = m_scr[...]


def _mlstm_mixer(p, gates, c0, n0, m0, i_b, f_b, norm_w):
    b, l, _ = p.shape
    c, n, nv, bb = _seq_dims(b, l)
    nh, dk, dv = MLSTM_H, MLSTM_DK, MLSTM_DV
    gates = _pad_seq(gates, c, n)
    bcol = jnp.stack([i_b, f_b])
    out, c_new, n_new, m_new = pl.pallas_call(
        functools.partial(_mlstm_kernel, bb=bb, c=c, nv=nv),
        grid=(b // bb, n),
        in_specs=[_seq_spec(bb, c, p.shape[2]), _seq_spec(bb, c, LANES),
                  pl.BlockSpec((bb, 1, SUBLANES, c), lambda b, i: (b, i, 0, 0)),
                  _state_spec(bb, (nh, dv, dk)), _state_spec(bb, (nh, 1, dk)), _state_spec(bb, (nh, 1, 1)),
                  _const_spec((2, nh)), _const_spec((nh, 2)), _const_spec((1, dv))],
        out_specs=[_seq_spec(bb, c, nh * dv), _state_spec(bb, (nh, dv, dk)),
                   _state_spec(bb, (nh, 1, dk)), _state_spec(bb, (nh, 1, 1))],
        out_shape=[jax.ShapeDtypeStruct((b, n * c, nh * dv), F32),
                   jax.ShapeDtypeStruct((b, nh, dv, dk), F32),
                   jax.ShapeDtypeStruct((b, nh, 1, dk), F32),
                   jax.ShapeDtypeStruct((b, nh, 1, 1), F32)],
        scratch_shapes=[pltpu.VMEM((bb, nh, dv, dk), F32), pltpu.VMEM((bb, nh, 1, dk), F32),
                        pltpu.VMEM((bb, nh, 1, 1), F32)],
        compiler_params=_cparams("parallel", "arbitrary"),
        name="mlstm",
    )(_pad_seq(p, c, n), gates, _rows_layout(gates, c, n, SUBLANES), c0,
      n0.reshape(b, nh, 1, dk), m0.reshape(b, nh, 1, 1), bcol, bcol.T, norm_w.reshape(1, dv))
    return out[:, :l], c_new, n_new.reshape(b, nh, dk), m_new.reshape(b, nh)


def _pad_cols(w, n):
    return jnp.pad(w, ((0, 0), (0, n - w.shape[1])))


def _ab_in_weight(w):
    o = 2 * LRU_W
    xy, qkv = w[:, :o], w[:, o:o + GDN_QKV]
    o += GDN_QKV
    ab, z = w[:, o:o + 2 * GDN_H], w[:, o + 2 * GDN_H:]
    return jnp.concatenate([xy, qkv, z, _pad_cols(ab, LANES)], axis=1).astype(BF16)


AB_SPLITS = ((0, 2 * LRU_W), (2 * LRU_W, GDN_QKV), (2 * LRU_W + GDN_QKV, GDN_W),
             (2 * LRU_W + GDN_QKV + GDN_W, LANES))


def _cd_in_weight(w):
    o = RWKV_PROJ_W
    rw, qkv = w[:, :o], w[:, o:o + 2 * MLSTM_QK + MLSTM_W]
    o += 2 * MLSTM_QK + MLSTM_W
    gates, og = w[:, o:o + 2 * MLSTM_H], w[:, o + 2 * MLSTM_H:]
    return jnp.concatenate([rw, qkv, og, _pad_cols(gates, LANES)], axis=1).astype(BF16)


CD_SPLITS = ((0, RWKV_PROJ_W), (RWKV_PROJ_W, 2 * MLSTM_QK + 2 * MLSTM_W),
             (RWKV_PROJ_W + 2 * MLSTM_QK + 2 * MLSTM_W, LANES))


def kernel(x_prompt, x_sample, c_prompt, c_sample,
           state_lru_h, state_lru_conv, state_gdn_S, state_gdn_conv,
           state_rwkv_S, state_rwkv_shift, state_mlstm_C, state_mlstm_n, state_mlstm_m,
           mod_w, mod_b, ln1_g, ln1_b, ln2_g, ln2_b,
           ab_w_in, ab_w_out, lru_conv_w, lru_conv_b, lru_wr, lru_br, lru_wi, lru_bi, lru_lambda,
           gdn_conv_w, gdn_a_log, gdn_dt_bias, gdn_norm_w,
           cd_w_in, cd_w_out, rwkv_mix, rwkv_w0, rwkv_w2, rwkv_a0, rwkv_a2, rwkv_g2,
           rwkv_k_k, rwkv_k_a, rwkv_r_k, rwkv_ln_w, rwkv_ln_b,
           mlstm_i_b, mlstm_f_b, mlstm_norm_w,
           ffn_w_gate, ffn_w_up, ffn_w_down,
           router_w, moe_w_gate, moe_w_up, moe_w_down):
    d = D_MODEL
    bp, lp, _ = x_prompt.shape
    bs, ls, _ = x_sample.shape
    mod = _modulation(jnp.concatenate([c_prompt, c_sample], axis=0), mod_w, mod_b)

    ab_in = [_ab_in_weight(ab_w_in[j]) for j in range(ab_w_in.shape[0])]
    cd_in = [_cd_in_weight(cd_w_in[j]) for j in range(cd_w_in.shape[0])]
    ab_out, cd_out = ab_w_out.astype(BF16), cd_w_out.astype(BF16)
    ffn_g, ffn_u, ffn_d = (t.astype(BF16) for t in (ffn_w_gate, ffn_w_up, ffn_w_down))
    moe_g, moe_u, moe_d = (t.astype(BF16) for t in (moe_w_gate, moe_w_up, moe_w_down))

    def trunk(x, mods, batch, length, states, pos0):
        lru_h, lru_conv, gdn_s, gdn_conv, rwkv_s, rwkv_shift, m_c, m_n, m_m = states
        new = [[] for _ in range(9)]
        seq = lambda t: t.reshape(batch, length, t.shape[-1])
        tok = lambda t: t.reshape(x.shape[0], x.shape[1], t.shape[-1])
        for l in range(DEPTH):
            j = l // 2
            sh1, sc1, g1, sh2, sc2, g2 = mods[l]
            if l % 2 == 0:
                xy, qkv, z, gates = _inproj(x, sc1, sh1, ab_in[j], AB_SPLITS)
                out_a, s0, s1 = _lru_mixer(seq(xy), lru_conv[j], lru_h[j], lru_conv_w[j], lru_conv_b[j],
                                           lru_wr[j], lru_br[j], lru_wi[j], lru_bi[j], lru_lambda[j], pos0)
                out_b, s2, s3 = _gdn_mixer(seq(qkv), seq(z), seq(gates), gdn_conv[j], gdn_s[j],
                                           gdn_conv_w[j], gdn_a_log[j], gdn_dt_bias[j], gdn_norm_w[j])
                for slot, s in zip((0, 1, 2, 3), (s0, s1, s2, s3)):
                    new[slot].append(s)
                x = _outproj_ln(x, tok(out_a), tok(out_b), g1, ab_out[j], ln1_g[l], ln1_b[l])
                x = _ffn_ln(x, sc2, sh2, g2, ffn_g[j], ffn_u[j], ffn_d[j], ln2_g[l], ln2_b[l])
            else:
                rw, mp, gates = _inproj(x, sc1, sh1, cd_in[j], CD_SPLITS)
                out_c, s0 = _rwkv_mixer(seq(rw), rwkv_shift[j], rwkv_s[j], rwkv_mix[j], rwkv_w0[j],
                                        rwkv_w2[j], rwkv_a0[j], rwkv_a2[j], rwkv_g2[j], rwkv_k_k[j],
                                        rwkv_k_a[j], rwkv_r_k[j], rwkv_ln_w[j], rwkv_ln_b[j])
                out_d, s2, s3, s4 = _mlstm_mixer(seq(mp), seq(gates), m_c[j], m_n[j], m_m[j],
                                                 mlstm_i_b[j], mlstm_f_b[j], mlstm_norm_w[j])
                for slot, s in zip((4, 5, 6, 7, 8), (s0, seq(rw)[:, -1], s2, s3, s4)):
                    new[slot].append(s)
                x = _outproj_ln(x, tok(out_c), tok(out_d), g1, cd_out[j], ln1_g[l], ln1_b[l])
                x = _moe_ln(x, sc2, sh2, g2, router_w[j], moe_g[j], moe_u[j], moe_d[j],
                            ln2_g[l], ln2_b[l])
        return x, tuple(jnp.stack(s) for s in new)

    def zeros(ref):
        return jnp.zeros((ref.shape[0], bp) + ref.shape[2:], F32)

    mods_p = [[mod[l, :bp, k * d:(k + 1) * d].reshape(bp, 1, d) for k in range(6)] for l in range(DEPTH)]
    mods_s = [[mod[l, bp:, k * d:(k + 1) * d].reshape(1, bs * ls, d) for k in range(6)] for l in range(DEPTH)]
    states_s = (state_lru_h, state_lru_conv, state_gdn_S, state_gdn_conv, state_rwkv_S,
                state_rwkv_shift, state_mlstm_C, state_mlstm_n, state_mlstm_m)
    y_p, new_p = trunk(x_prompt, mods_p, bp, lp, tuple(zeros(s) for s in states_s), 0)
    y_s, new_s = trunk(x_sample.reshape(1, bs * ls, d), mods_s, bs, ls, states_s, PAST_LEN)
    out = [y_p, y_s.reshape(bs, ls, d)]
    for p_leaf, s_leaf in zip(new_p, new_s):
        out += [p_leaf, s_leaf]
    return tuple(out)
```

```python
import functools
import math

import jax
import jax.numpy as jnp
from jax import lax
from jax.experimental import pallas as pl
from jax.experimental.pallas import tpu as pltpu

F32 = jnp.float32
BF16 = jnp.bfloat16

D_MODEL = 1024
DEPTH = 4
PAST_LEN = 16384
CONV_W = 4
LRU_W = D_MODEL // 2
LRU_BLOCKS = 8
LRU_BW = LRU_W // LRU_BLOCKS
LRU_C = 8.0
GDN_H = D_MODEL // 256
GDN_DK = 128
GDN_DV = 128
GDN_QK = GDN_H * GDN_DK
GDN_W = GDN_H * GDN_DV
GDN_QKV = 2 * GDN_QK + GDN_W
RWKV_HD = 64
RWKV_H = D_MODEL // 2 // RWKV_HD
RWKV_W = RWKV_H * RWKV_HD
RWKV_RW = 64
RWKV_RA = 64
RWKV_RG = 128
RWKV_PROJ_W = 3 * RWKV_W + RWKV_RW + RWKV_RA + RWKV_RG
RWKV_GN_EPS = 64e-5
MLSTM_H = D_MODEL // 256
MLSTM_DK = 128
MLSTM_DV = 128
MLSTM_QK = MLSTM_H * MLSTM_DK
MLSTM_W = MLSTM_H * MLSTM_DV
D_FF = 7 * D_MODEL // 2
N_EXPERTS = 8
LN_EPS = 1e-5
NEG_BIG = -1e30
DN_ALPHA = (2.0 * DEPTH) ** 0.25

LANES = 128
SUBLANES = 8
SEQ_CHUNK = 64
SHORT_SEQ_BLOCK = 8
ROW_TILE = 512
FF_TILE = 512
MOE_TOKEN_BLOCK = 512
MOE_SLOT_CHUNK = 128
MOE_FF_SPLIT = 2
VMEM_LIMIT = 48 * 1024 * 1024


def _cparams(*sem):
    return pltpu.CompilerParams(dimension_semantics=sem, vmem_limit_bytes=VMEM_LIMIT)


def _dot(a, b):
    return jnp.dot(a.astype(BF16), b.astype(BF16), preferred_element_type=F32)


def _dot_nt(a, b):
    return lax.dot_general(a.astype(BF16), b.astype(BF16), (((1,), (1,)), ((), ())),
                           preferred_element_type=F32)


def _dot_tn(a, b):
    return lax.dot_general(a.astype(BF16), b.astype(BF16), (((0,), (0,)), ((), ())),
                           preferred_element_type=F32)


def _split3(x):
    hi = x.astype(BF16)
    r1 = x - hi.astype(F32)
    mid = r1.astype(BF16)
    lo = (r1 - mid.astype(F32)).astype(BF16)
    return hi, mid, lo


def _dot_sel_lhs(t, x):
    tb = jnp.where(t, 1.0, 0.0).astype(BF16)
    hi, mid, lo = _split3(x)
    d = lambda p: jnp.dot(tb, p, preferred_element_type=F32)
    return d(hi) + d(mid) + d(lo)


def _dot_sel_rhs(x, t):
    tb = jnp.where(t, 1.0, 0.0).astype(BF16)
    hi, mid, lo = _split3(x)
    d = lambda p: jnp.dot(p, tb, preferred_element_type=F32)
    return d(hi) + d(mid) + d(lo)


def _dot2(p, x):
    pb = p.astype(BF16)
    xh = x.astype(BF16)
    xl = (x - xh.astype(F32)).astype(BF16)
    return (jnp.dot(pb, xh, preferred_element_type=F32)
            + jnp.dot(pb, xl, preferred_element_type=F32))


def _sigmoid(x):
    return 1.0 / (1.0 + jnp.exp(-x))


def _silu(x):
    return x * _sigmoid(x)


def _softplus(x):
    return jnp.maximum(x, 0.0) + jnp.log1p(jnp.exp(-jnp.abs(x)))


def _gelu_tanh(x):
    return 0.5 * x * (1.0 + jnp.tanh(math.sqrt(2.0 / math.pi) * (x + 0.044715 * (x * x * x))))


def _ln(y, g=None, b=None, eps=LN_EPS):
    mu = jnp.mean(y, axis=-1, keepdims=True)
    d = y - mu
    var = jnp.mean(d * d, axis=-1, keepdims=True)
    out = d * lax.rsqrt(var + eps)
    if g is not None:
        out = out * g
    if b is not None:
        out = out + b
    return out


def _l2norm(x, eps=1e-6):
    return x * lax.rsqrt(jnp.sum(x * x, axis=-1, keepdims=True) + eps)


def _tri(c, inclusive):
    t = lax.broadcasted_iota(jnp.int32, (c, c), 0)
    s = lax.broadcasted_iota(jnp.int32, (c, c), 1)
    return (s <= t) if inclusive else (s < t)


def _unit_lower_solve(ns, xs, c):
    steps = max(1, int(math.ceil(math.log2(c))))
    for i in range(steps):
        xs = [x + _dot2(p, x) for p, x in zip(ns, xs)]
        if i + 1 < steps:
            ns = [_dot(p, p) for p in ns]
    return xs


def _dplr_heads(heads, states, c, nv, state_is_vk):
    incl = _tri(c, True)
    strict = _tri(c, False)
    ms = [_dot_nt(h["m_lhs"], h["m_rhs"]) for h in heads]
    if state_is_vk:
        xhs = [_dot_nt(h["x_lhs"], s) for h, s in zip(heads, states)]
    else:
        xhs = [_dot(h["x_lhs"], s) for h, s in zip(heads, states)]
    a_abs, a_aks, r_bks = [], [], []
    for h, m in zip(heads, ms):
        if "pair_x" in h:
            a_abs.append(m[:c, :c] * h["pair_x"])
            a_aks.append(m[:c, c:] * h["pair_x"])
            r_bks.append(jnp.concatenate([m[c:, :c] * h["pair_i"], m[c:, c:] * h["pair_i"]], axis=1))
        else:
            a_abs.append(jnp.where(strict, m[:c, :c], 0.0))
            a_aks.append(jnp.where(strict, m[:c, c:], 0.0))
            r_bks.append(jnp.concatenate([jnp.where(incl, m[c:, :c], 0.0),
                                          jnp.where(incl, m[c:, c:], 0.0)], axis=1))
    if nv == 1:
        us = [xh[:c] for xh in xhs]
    else:
        rhs = [xh[:c] + _dot(a_ak, h["v"]) for xh, a_ak, h in zip(xhs, a_aks, heads)]
        us = _unit_lower_solve(a_abs, rhs, c)
    uvs = [jnp.concatenate([u, h["v"]], axis=0) for u, h in zip(us, heads)]
    outs = [xh[c:] + _dot(r_bk, uv) for xh, r_bk, uv in zip(xhs, r_bks, uvs)]
    if state_is_vk:
        new = [s * h["s_decay"] + _dot_tn(uv, h["bk"]) for s, h, uv in zip(states, heads, uvs)]
    else:
        new = [s * h["s_decay"] + _dot_tn(h["bk"], uv) for s, h, uv in zip(states, heads, uvs)]
    return outs, new


def _valid_rows(c, n_valid):
    return lax.broadcasted_iota(jnp.int32, (c, 1), 0) < n_valid


def _mod_kernel(c_ref, w_ref, b_ref, o_ref):
    o_ref[0] = _dot(_silu(c_ref[...]), w_ref[0]) + b_ref[0]


def _modulation(c_all, mod_w, mod_b):
    n = c_all.shape[0]
    d = D_MODEL
    return pl.pallas_call(
        _mod_kernel,
        grid=(DEPTH, 6),
        in_specs=[pl.BlockSpec((n, d), lambda l, j: (0, 0)),
                  pl.BlockSpec((1, d, d), lambda l, j: (l, 0, j)),
                  pl.BlockSpec((1, 1, d), lambda l, j: (l, 0, j))],
        out_specs=pl.BlockSpec((1, n, d), lambda l, j: (l, 0, j)),
        out_shape=jax.ShapeDtypeStruct((DEPTH, n, 6 * d), F32),
        compiler_params=_cparams("parallel", "parallel"),
        name="modulation",
    )(c_all, mod_w, mod_b.reshape(DEPTH, 1, 6 * d))


def _mod_spec(mod, tm):
    if mod.shape[1] == 1:
        return pl.BlockSpec((1, 1, mod.shape[2]), lambda g, i, *_: (g, 0, 0))
    return pl.BlockSpec((1, tm, mod.shape[2]), lambda g, i, *_: (g, i, 0))


def _inproj_kernel(x_ref, sc_ref, sh_ref, w_ref, *o_refs, splits):
    h = (x_ref[0] * (1.0 + sc_ref[0]) + sh_ref[0]).astype(BF16)
    for o_ref, (s, n) in zip(o_refs, splits):
        o_ref[0] = jnp.dot(h, w_ref[:, s:s + n], preferred_element_type=F32)


def _inproj(x, sc, sh, w, splits):
    g, r, d = x.shape
    tm = min(r, ROW_TILE)
    n_all = w.shape[1]
    return pl.pallas_call(
        functools.partial(_inproj_kernel, splits=splits),
        grid=(g, r // tm),
        in_specs=[pl.BlockSpec((1, tm, d), lambda g, i: (g, i, 0)),
                  _mod_spec(sc, tm), _mod_spec(sh, tm),
                  pl.BlockSpec((d, n_all), lambda g, i: (0, 0))],
        out_specs=[pl.BlockSpec((1, tm, n), lambda g, i: (g, i, 0)) for _, n in splits],
        out_shape=[jax.ShapeDtypeStruct((g, r, n), F32) for _, n in splits],
        compiler_params=_cparams("parallel", "parallel"),
        name="inproj",
    )(x, sc, sh, w)


def _outproj_ln_kernel(x_ref, ma_ref, mb_ref, gate_ref, w_ref, lng_ref, lnb_ref, o_ref):
    half = ma_ref.shape[2]
    f = _dot(ma_ref[0], w_ref[0:half, :]) + _dot(mb_ref[0], w_ref[half:, :])
    y = DN_ALPHA * x_ref[0] + (1.0 + gate_ref[0]) * f
    o_ref[0] = _ln(y, lng_ref[...], lnb_ref[...])


def _outproj_ln(x, mix_a, mix_b, gate, w, ln_g, ln_b):
    g, r, d = x.shape
    tm = min(r, ROW_TILE)
    half = mix_a.shape[2]
    row = pl.BlockSpec((1, tm, d), lambda g, i: (g, i, 0))
    mrow = pl.BlockSpec((1, tm, half), lambda g, i: (g, i, 0))
    vec = pl.BlockSpec((1, d), lambda g, i: (0, 0))
    return pl.pallas_call(
        _outproj_ln_kernel,
        grid=(g, r // tm),
        in_specs=[row, mrow, mrow, _mod_spec(gate, tm),
                  pl.BlockSpec((2 * half, d), lambda g, i: (0, 0)), vec, vec],
        out_specs=row,
        out_shape=jax.ShapeDtypeStruct((g, r, d), F32),
        compiler_params=_cparams("parallel", "parallel"),
        name="outproj_ln",
    )(x, mix_a, mix_b, gate, w, ln_g.reshape(1, d), ln_b.reshape(1, d))


def _ffn_kernel(x_ref, sc_ref, sh_ref, gate_ref, wg_ref, wu_ref, wd_ref, lng_ref, lnb_ref,
                o_ref, h_scr, acc_scr):
    j = pl.program_id(2)

    @pl.when(j == 0)
    def _():
        h_scr[...] = (x_ref[0] * (1.0 + sc_ref[0]) + sh_ref[0]).astype(BF16)
        acc_scr[...] = jnp.zeros_like(acc_scr)

    h = h_scr[...]
    a = jnp.dot(h, wg_ref[...], preferred_element_type=F32)
    u = jnp.dot(h, wu_ref[...], preferred_element_type=F32)
    acc_scr[...] += _dot(_silu(a) * u, wd_ref[...])

    @pl.when(j == pl.num_programs(2) - 1)
    def _():
        y = DN_ALPHA * x_ref[0] + (1.0 + gate_ref[0]) * acc_scr[...]
        o_ref[0] = _ln(y, lng_ref[...], lnb_ref[...])


def _ffn_ln(x, sc, sh, gate, wg, wu, wd, ln_g, ln_b):
    g, r, d = x.shape
    tm = min(r, ROW_TILE)
    f = wg.shape[1]
    row = pl.BlockSpec((1, tm, d), lambda g, i, j: (g, i, 0))
    vec = pl.BlockSpec((1, d), lambda g, i, j: (0, 0))
    return pl.pallas_call(
        _ffn_kernel,
        grid=(g, r // tm, f // FF_TILE),
        in_specs=[row, _mod_spec(sc, tm), _mod_spec(sh, tm), _mod_spec(gate, tm),
                  pl.BlockSpec((d, FF_TILE), lambda g, i, j: (0, j)),
                  pl.BlockSpec((d, FF_TILE), lambda g, i, j: (0, j)),
                  pl.BlockSpec((FF_TILE, d), lambda g, i, j: (j, 0)), vec, vec],
        out_specs=row,
        out_shape=jax.ShapeDtypeStruct((g, r, d), F32),
        scratch_shapes=[pltpu.VMEM((tm, d), BF16), pltpu.VMEM((tm, d), F32)],
        compiler_params=_cparams("parallel", "parallel", "arbitrary"),
        name="ffn_ln",
    )(x, sc, sh, gate, wg, wu, wd, ln_g.reshape(1, d), ln_b.reshape(1, d))


def _top2_route(h, rw, lane):
    logits = jnp.dot(h, rw, preferred_element_type=F32, precision=lax.Precision.HIGHEST)
    logits = jnp.where(lane < N_EXPERTS, logits, -jnp.inf)
    m1 = jnp.max(logits, axis=-1, keepdims=True)
    i1 = jnp.min(jnp.where(logits == m1, lane, LANES), axis=-1, keepdims=True)
    rest = jnp.where(lane == i1, -jnp.inf, logits)
    m2 = jnp.max(rest, axis=-1, keepdims=True)
    i2 = jnp.min(jnp.where(rest == m2, lane, LANES), axis=-1, keepdims=True)
    e2 = jnp.exp(m2 - m1)
    g1 = 1.0 / (1.0 + e2)
    g2 = e2 / (1.0 + e2)
    sel = (lane == i1) | (lane == i2)
    return sel, jnp.where(lane == i1, g1, 0.0) + jnp.where(lane == i2, g2, 0.0)


def _moe_kernel(x_ref, sc_ref, sh_ref, gate_ref, rw_ref, wg_ref, wu_ref, wd_ref, lng_ref, lnb_ref,
                o_ref, h_scr, comb_scr, acc_scr):
    e = pl.program_id(2)
    j = pl.program_id(3)
    lane = lax.broadcasted_iota(jnp.int32, comb_scr.shape, 1)

    @pl.when((e == 0) & (j == 0))
    def _():
        h = x_ref[0] * (1.0 + sc_ref[0]) + sh_ref[0]
        h_scr[...] = h.astype(BF16)
        acc_scr[...] = jnp.zeros_like(acc_scr)
        _, comb_scr[...] = _top2_route(h, rw_ref[...], lane)

    h = h_scr[...]
    a = jnp.dot(h, wg_ref[0], preferred_element_type=F32)
    u = jnp.dot(h, wu_ref[0], preferred_element_type=F32)
    comb_e = jnp.sum(jnp.where(lane == e, comb_scr[...], 0.0), axis=-1, keepdims=True)
    acc_scr[...] += comb_e * _dot(_silu(a) * u, wd_ref[0])

    @pl.when((e == pl.num_programs(2) - 1) & (j == pl.num_programs(3) - 1))
    def _():
        y = DN_ALPHA * x_ref[0] + (1.0 + gate_ref[0]) * acc_scr[...]
        o_ref[0] = _ln(y, lng_ref[...], lnb_ref[...])


def _moe_ln(x, sc, sh, gate, router_w, wg, wu, wd, ln_g, ln_b):
    g, r, d = x.shape
    tm = min(r, ROW_TILE)
    ne, _, f = wg.shape
    row = pl.BlockSpec((1, tm, d), lambda g, i, e, j: (g, i, 0))
    vec = pl.BlockSpec((1, d), lambda g, i, e, j: (0, 0))
    rw = jnp.pad(router_w, ((0, 0), (0, LANES - ne)))
    return pl.pallas_call(
        _moe_kernel,
        grid=(g, r // tm, ne, f // FF_TILE),
        in_specs=[row, _mod_spec(sc, tm), _mod_spec(sh, tm), _mod_spec(gate, tm),
                  pl.BlockSpec((d, LANES), lambda g, i, e, j: (0, 0)),
                  pl.BlockSpec((1, d, FF_TILE), lambda g, i, e, j: (e, 0, j)),
                  pl.BlockSpec((1, d, FF_TILE), lambda g, i, e, j: (e, 0, j)),
                  pl.BlockSpec((1, FF_TILE, d), lambda g, i, e, j: (e, j, 0)), vec, vec],
        out_specs=row,
        out_shape=jax.ShapeDtypeStruct((g, r, d), F32),
        scratch_shapes=[pltpu.VMEM((tm, d), BF16), pltpu.VMEM((tm, LANES), F32),
                        pltpu.VMEM((tm, d), F32)],
        compiler_params=_cparams("parallel", "parallel", "arbitrary", "arbitrary"),
        name="moe_ln",
    )(x, sc, sh, gate, rw, wg, wu, wd, ln_g.reshape(1, d), ln_b.reshape(1, d))


def _moe_route_kernel(x_ref, sc_ref, sh_ref, rw_ref, hb_ref, comb_ref, rank_ref, rankt_ref, cnt_ref,
                      run_scr):
    @pl.when(pl.program_id(0) == 0)
    def _():
        run_scr[...] = jnp.zeros_like(run_scr)

    h = x_ref[0] * (1.0 + sc_ref[0]) + sh_ref[0]
    hb_ref[0] = h.astype(BF16)
    tb = h.shape[0]
    lane = lax.broadcasted_iota(jnp.int32, (tb, LANES), 1)
    sel, comb = _top2_route(h, rw_ref[...], lane)
    comb_ref[0] = comb
    ones = jnp.where(sel, 1.0, 0.0)
    before = jnp.dot(jnp.where(_tri(tb, False), 1.0, 0.0).astype(BF16), ones.astype(BF16),
                     preferred_element_type=F32)
    rank = jnp.where(sel, before + run_scr[...], -1.0)
    rank_ref[0] = rank
    rankt_ref[0] = rank.T[0:SUBLANES, :]
    cnt = jnp.sum(ones, axis=0, keepdims=True)
    cnt_ref[0] = cnt
    run_scr[...] += cnt


def _moe_route(xb, sc, sh, router_w, blocks_per_seq):
    nb, tb, d = xb.shape
    seq = lambda j: (j // blocks_per_seq, 0, 0)
    blk = lambda width: pl.BlockSpec((1, tb, width), lambda j: (j, 0, 0))
    return pl.pallas_call(
        _moe_route_kernel,
        grid=(nb,),
        in_specs=[blk(d), pl.BlockSpec((1, 1, d), seq), pl.BlockSpec((1, 1, d), seq),
                  pl.BlockSpec((d, LANES), lambda j: (0, 0))],
        out_specs=[blk(d), blk(LANES), blk(LANES),
                   pl.BlockSpec((1, SUBLANES, tb), lambda j: (j, 0, 0)),
                   pl.BlockSpec((1, 1, LANES), lambda j: (j, 0, 0))],
        out_shape=[jax.ShapeDtypeStruct((nb, tb, d), BF16),
                   jax.ShapeDtypeStruct((nb, tb, LANES), F32),
                   jax.ShapeDtypeStruct((nb, tb, LANES), F32),
                   jax.ShapeDtypeStruct((nb, SUBLANES, tb), F32),
                   jax.ShapeDtypeStruct((nb, 1, LANES), F32)],
        scratch_shapes=[pltpu.VMEM((1, LANES), F32)],
        compiler_params=_cparams("arbitrary"),
        name="moe_route",
    )(xb, sc, sh, jnp.pad(router_w, ((0, 0), (0, LANES - router_w.shape[1]))))


def _moe_tables(cnt, n_chunks, n_items):
    nb, ne = cnt.shape
    c = MOE_SLOT_CHUNK
    off = jnp.cumsum(cnt, axis=0) - cnt
    total = jnp.sum(cnt, axis=0)
    nch = (total + c - 1) // c
    ends = jnp.cumsum(nch)
    k = jnp.arange(n_chunks, dtype=jnp.int32)
    ce = jnp.minimum(jnp.searchsorted(ends, k, side="right"), ne - 1).astype(jnp.int32)
    cvalid = k < ends[-1]
    r0 = (k - (ends - nch)[ce]) * c
    lo = jnp.maximum(r0[:, None], off.T[ce])
    hi = jnp.minimum(r0[:, None] + c, (off + cnt).T[ce])
    overlap = cvalid[:, None] & (lo < hi)
    n_pairs = jnp.sum(overlap)
    pos = jnp.arange(n_items, dtype=jnp.int32)

    def work_list(mat, group_is_row):
        flat = jnp.nonzero(mat.reshape(-1), size=n_items, fill_value=0)[0].astype(jnp.int32)
        flat = jnp.where(pos < n_pairs, flat, flat[jnp.maximum(n_pairs - 1, 0)])
        row, col = flat // mat.shape[1], flat % mat.shape[1]
        chunk, block = (row, col) if group_is_row else (col, row)
        group = row
        valid = pos < n_pairs
        first = valid & ((pos == 0) | (group != jnp.roll(group, 1)))
        last = valid & ((pos == n_pairs - 1) | (group != jnp.roll(group, -1)))
        i32 = lambda t: t.astype(jnp.int32)
        return dict(chunk=chunk, block=block, expert=ce[chunk], r0=r0[chunk], first=i32(first),
                    last=i32(last), valid=i32(valid))

    return ce, cvalid.astype(jnp.int32), work_list(overlap, True), work_list(overlap.T, False)


def _moe_gather_kernel(chunk_ref, block_ref, expert_ref, r0_ref, first_ref, last_ref, valid_ref,
                       hb_ref, rankt_ref, xs_ref, acc_scr):
    w = pl.program_id(0)

    @pl.when(first_ref[w] == 1)
    def _():
        acc_scr[...] = jnp.zeros_like(acc_scr)

    @pl.when(valid_ref[w] == 1)
    def _():
        c = acc_scr.shape[0]
        rank = rankt_ref[0, pl.ds(expert_ref[w], 1), :]
        slot = lax.broadcasted_iota(jnp.int32, (c, 1), 0).astype(F32) + r0_ref[w].astype(F32)
        pick = jnp.where(rank == slot, 1.0, 0.0).astype(BF16)
        acc_scr[...] += jnp.dot(pick, hb_ref[0], preferred_element_type=F32)

    @pl.when(last_ref[w] == 1)
    def _():
        xs_ref[0] = acc_scr[...].astype(BF16)


def _moe_gather(items, hb, rankt, n_chunks):
    nb, tb, d = hb.shape
    c = MOE_SLOT_CHUNK
    names = ("chunk", "block", "expert", "r0", "first", "last", "valid")
    by_block = lambda shape: pl.BlockSpec(shape, lambda w, ch, bl, *_: (bl[w], 0, 0))
    return pl.pallas_call(
        _moe_gather_kernel,
        grid_spec=pltpu.PrefetchScalarGridSpec(
            num_scalar_prefetch=len(names), grid=(items["chunk"].shape[0],),
            in_specs=[by_block((1, tb, d)), by_block((1, SUBLANES, tb))],
            out_specs=pl.BlockSpec((1, c, d), lambda w, ch, *_: (ch[w], 0, 0)),
            scratch_shapes=[pltpu.VMEM((c, d), F32)]),
        out_shape=jax.ShapeDtypeStruct((n_chunks, c, d), BF16),
        compiler_params=_cparams("arbitrary"),
        name="moe_gather",
    )(*[items[n] for n in names], hb, rankt)


def _moe_expert_kernel(ce_ref, cvalid_ref, xs_ref, wg_ref, wu_ref, wd_ref, *rest):
    o_ref = rest[-1]
    k = pl.program_id(0)

    @pl.when(cvalid_ref[k] == 1)
    def _():
        x = xs_ref[0]
        a = jnp.dot(x, wg_ref[0], preferred_element_type=F32)
        u = jnp.dot(x, wu_ref[0], preferred_element_type=F32)
        y = _dot(_silu(a) * u, wd_ref[0])
        o_ref[0] = y if len(rest) == 1 else rest[0][0] + y

    @pl.when(cvalid_ref[k] == 0)
    def _():
        o_ref[0] = jnp.zeros(o_ref.shape[1:], F32)


def _moe_experts(ce, cvalid, xs, wg, wu, wd):
    n_chunks, c, d = xs.shape
    fh = wg.shape[2] // MOE_FF_SPLIT
    y = None
    for half in range(MOE_FF_SPLIT):
        row = pl.BlockSpec((1, c, d), lambda k, ce, cv: (k, 0, 0))
        in_specs = [row,
                    pl.BlockSpec((1, d, fh), lambda k, ce, cv, half=half: (ce[k], 0, half)),
                    pl.BlockSpec((1, d, fh), lambda k, ce, cv, half=half: (ce[k], 0, half)),
                    pl.BlockSpec((1, fh, d), lambda k, ce, cv, half=half: (ce[k], half, 0))]
        args = [ce, cvalid, xs, wg, wu, wd]
        aliases = {}
        if y is not None:
            in_specs.append(row)
            args.append(y)
            aliases = {len(args) - 1: 0}
        y = pl.pallas_call(
            _moe_expert_kernel,
            grid_spec=pltpu.PrefetchScalarGridSpec(
                num_scalar_prefetch=2, grid=(n_chunks,), in_specs=in_specs, out_specs=row),
            out_shape=jax.ShapeDtypeStruct((n_chunks, c, d), F32),
            input_output_aliases=aliases,
            compiler_params=_cparams("arbitrary"),
            name="moe_experts",
        )(*args)
    return y


def _moe_combine_kernel(chunk_ref, block_ref, expert_ref, r0_ref, first_ref, last_ref, valid_ref,
                        ys_ref, rank_ref, comb_ref, x_ref, gate_ref, lng_ref, lnb_ref, o_ref, acc_scr):
    w = pl.program_id(0)

    @pl.when(first_ref[w] == 1)
    def _():
        acc_scr[...] = jnp.zeros_like(acc_scr)

    @pl.when(valid_ref[w] == 1)
    def _():
        tb = acc_scr.shape[0]
        c = ys_ref.shape[1]
        lane = lax.broadcasted_iota(jnp.int32, (tb, LANES), 1)
        mine = lane == expert_ref[w]
        rank = jnp.sum(jnp.where(mine, rank_ref[0], 0.0), axis=-1, keepdims=True)
        comb = jnp.sum(jnp.where(mine, comb_ref[0], 0.0), axis=-1, keepdims=True)
        slot = lax.broadcasted_iota(jnp.int32, (1, c), 1).astype(F32) + r0_ref[w].astype(F32)
        pick = jnp.where(rank == slot, 1.0, 0.0).astype(BF16)
        y = ys_ref[0]
        y_hi = y.astype(BF16)
        y_lo = (y - y_hi.astype(F32)).astype(BF16)
        rows = (jnp.dot(pick, y_hi, preferred_element_type=F32)
                + jnp.dot(pick, y_lo, preferred_element_type=F32))
        acc_scr[...] += comb * rows

    @pl.when(last_ref[w] == 1)
    def _():
        y = DN_ALPHA * x_ref[0] + (1.0 + gate_ref[0]) * acc_scr[...]
        o_ref[0] = _ln(y, lng_ref[...], lnb_ref[...])


def _moe_combine_ln(items, ys, rank, comb, xb, gate, ln_g, ln_b, blocks_per_seq):
    nb, tb, d = xb.shape
    c = MOE_SLOT_CHUNK
    names = ("chunk", "block", "expert", "r0", "first", "last", "valid")
    by_block = lambda width: pl.BlockSpec((1, tb, width), lambda w, ch, bl, *_: (bl[w], 0, 0))
    vec = pl.BlockSpec((1, d), lambda w, *_: (0, 0))
    return pl.pallas_call(
        _moe_combine_kernel,
        grid_spec=pltpu.PrefetchScalarGridSpec(
            num_scalar_prefetch=len(names), grid=(items["chunk"].shape[0],),
            in_specs=[pl.BlockSpec((1, c, d), lambda w, ch, *_: (ch[w], 0, 0)),
                      by_block(LANES), by_block(LANES), by_block(d),
                      pl.BlockSpec((1, 1, d), lambda w, ch, bl, *_: (bl[w] // blocks_per_seq, 0, 0)),
                      vec, vec],
            out_specs=by_block(d),
            scratch_shapes=[pltpu.VMEM((tb, d), F32)]),
        out_shape=jax.ShapeDtypeStruct((nb, tb, d), F32),
        compiler_params=_cparams("arbitrary"),
        name="moe_combine_ln",
    )(*[items[n] for n in names], ys, rank, comb, xb, gate, ln_g.reshape(1, d), ln_b.reshape(1, d))


def _moe_sparse_ln(x, sc, sh, gate, router_w, wg, wu, wd, ln_g, ln_b):
    g, r, d = x.shape
    tb, c, ne = MOE_TOKEN_BLOCK, MOE_SLOT_CHUNK, wg.shape[0]
    assert sc.shape[1] == 1 and r % tb == 0
    nb = g * r // tb
    n_chunks = 2 * g * r // c + ne
    n_items = n_chunks + ne * nb
    xb = x.reshape(nb, tb, d)
    hb, comb, rank, rankt, cnt = _moe_route(xb, sc, sh, router_w, r // tb)
    ce, cvalid, by_chunk, by_block = _moe_tables(cnt[:, 0, :ne].astype(jnp.int32), n_chunks, n_items)
    xs = _moe_gather(by_chunk, hb, rankt, n_chunks)
    ys = _moe_experts(ce, cvalid, xs, wg, wu, wd)
    out = _moe_combine_ln(by_block, ys, rank, comb, xb, gate, ln_g, ln_b, r // tb)
    return out.reshape(g, r, d)


def _seq_dims(b, l):
    if l >= SEQ_CHUNK:
        assert l % SEQ_CHUNK == 0
        return SEQ_CHUNK, l // SEQ_CHUNK, SEQ_CHUNK, 1
    assert l <= SUBLANES and b % SHORT_SEQ_BLOCK == 0
    return SUBLANES, 1, l, SHORT_SEQ_BLOCK


def _pad_seq(t, c, n):
    pad = c * n - t.shape[1]
    return t if pad == 0 else jnp.pad(t, ((0, 0), (0, pad), (0, 0)))


def _rows_layout(t, c, n, width):
    b = t.shape[0]
    return jnp.swapaxes(t[:, :, :width].reshape(b, n, c, width), 2, 3)


def _conv_window(scr, u, taps, c):
    scr[SUBLANES:SUBLANES + c, :] = u
    out = scr[5:5 + c, :] * taps[0:1, :]
    for j in range(1, CONV_W):
        out = out + scr[5 + j:5 + j + c, :] * taps[j:j + 1, :]
    return out


def _seq_spec(bb, c, width):
    return pl.BlockSpec((bb, c, width), lambda b, i: (b, i, 0))


def _state_spec(bb, shape):
    return pl.BlockSpec((bb,) + shape, lambda b, i: (b,) + (0,) * len(shape))


def _const_spec(shape):
    return pl.BlockSpec(shape, lambda b, i: (0,) * len(shape))


def _lru_kernel(xy_ref, buf_ref, h0_ref, cw_ref, cb_ref, wg_ref, bg_ref, lam_ref,
                out_ref, hnew_ref, bufnew_ref, xs_scr, a_scr, b_scr, h_scr, *, bb, c, nv, pos0):
    i = pl.program_id(1)
    w = LRU_W

    @pl.when(i == 0)
    def _():
        xs_scr[:, 0:SUBLANES, :] = jnp.zeros((bb, SUBLANES, w), F32)
        xs_scr[:, 5:8, :] = buf_ref[...]
        h_scr[...] = h0_ref[...]

    pos = lax.broadcasted_iota(jnp.int32, (c, 1), 0) + (i * c + pos0)
    for bi in range(bb):
        xc = _conv_window(xs_scr.at[bi], xy_ref[bi, :, 0:w], cw_ref[...], c) + cb_ref[...]
        gates = _dot(xc, wg_ref[...]) + bg_ref[...]
        r = _sigmoid(gates[:, 0:w])
        ig = _sigmoid(gates[:, w:2 * w])
        log_a = -LRU_C * r * _softplus(-lam_ref[...])
        mult = jnp.sqrt(-jnp.tanh(log_a) * (jnp.exp(2.0 * log_a) + 1.0))
        mult = jnp.where(pos == 0, 1.0, mult)
        a_scr[bi] = jnp.exp(log_a)
        b_scr[bi] = xc * ig * mult

    def step(t, h):
        h = a_scr[:, pl.ds(t, 1), :] * h + b_scr[:, pl.ds(t, 1), :]
        b_scr[:, pl.ds(t, 1), :] = h
        return h

    h = lax.fori_loop(0, nv, step, h_scr[...], unroll=min(nv, SUBLANES))
    h_scr[...] = h
    out_ref[...] = b_scr[...] * _gelu_tanh(xy_ref[:, :, w:2 * w])
    tail = xs_scr[:, 5 + nv:8 + nv, :]
    xs_scr[:, 5:8, :] = tail

    @pl.when(i == pl.num_programs(1) - 1)
    def _():
        hnew_ref[...] = h
        bufnew_ref[...] = tail


def _lru_mixer(xy, buf, h0, conv_w, conv_b, wr, br, wi, bi, lam, pos0):
    b, l, _ = xy.shape
    c, n, nv, bb = _seq_dims(b, l)
    w = LRU_W
    bd = lambda m: jax.scipy.linalg.block_diag(*[m[i] for i in range(LRU_BLOCKS)])
    wgate = jnp.concatenate([bd(wr), bd(wi)], axis=1).astype(BF16)
    bgate = jnp.concatenate([br, bi]).reshape(1, 2 * w)
    out, h_new, buf_new = pl.pallas_call(
        functools.partial(_lru_kernel, bb=bb, c=c, nv=nv, pos0=pos0),
        grid=(b // bb, n),
        in_specs=[_seq_spec(bb, c, 2 * w), _state_spec(bb, (CONV_W - 1, w)), _state_spec(bb, (1, w)),
                  _const_spec((CONV_W, w)), _const_spec((1, w)), _const_spec((w, 2 * w)),
                  _const_spec((1, 2 * w)), _const_spec((1, w))],
        out_specs=[_seq_spec(bb, c, w), _state_spec(bb, (1, w)), _state_spec(bb, (CONV_W - 1, w))],
        out_shape=[jax.ShapeDtypeStruct((b, n * c, w), F32),
                   jax.ShapeDtypeStruct((b, 1, w), F32),
                   jax.ShapeDtypeStruct((b, CONV_W - 1, w), F32)],
        scratch_shapes=[pltpu.VMEM((bb, SUBLANES + c, w), F32), pltpu.VMEM((bb, c, w), F32),
                        pltpu.VMEM((bb, c, w), F32), pltpu.VMEM((bb, 1, w), F32)],
        compiler_params=_cparams("parallel", "arbitrary"),
        name="rglru",
    )(_pad_seq(xy, c, n), buf, h0.reshape(b, 1, w), conv_w, conv_b.reshape(1, w), wgate, bgate,
      lam.reshape(1, w))
    return out[:, :l], h_new.reshape(b, w), buf_new


def _gdn_kernel(qkv_ref, z_ref, gcol_ref, grow_ref, buf_ref, s0_ref, cw_ref, pcol_ref, prow_ref,
                nw_ref, out_ref, snew_ref, bufnew_ref, xs_scr, s_scr, *, bb, c, nv):
    i = pl.program_id(1)
    qk = GDN_QK

    @pl.when(i == 0)
    def _():
        xs_scr[:, 0:SUBLANES, :] = jnp.zeros((bb, SUBLANES, GDN_QKV), F32)
        xs_scr[:, 5:8, :] = buf_ref[...]
        s_scr[...] = s0_ref[...]

    valid = _valid_rows(c, nv)
    valid_r = lax.broadcasted_iota(jnp.int32, (1, c), 1) < nv
    alog_c, dtb_c = pcol_ref[0:1, :], pcol_ref[1:2, :]
    alog_r, dtb_r = prow_ref[:, 0:1], prow_ref[:, 1:2]
    incl, strict = _tri(c, True), _tri(c, False)
    upper = jnp.logical_not(strict)
    heads = []
    for bi in range(bb):
        x = _silu(_conv_window(xs_scr.at[bi], qkv_ref[bi], cw_ref[...], c))
        gcol = gcol_ref[bi]
        grow = grow_ref[bi, 0]
        g_col = jnp.where(valid, -jnp.exp(alog_c) * _softplus(gcol[:, 0:GDN_H] + dtb_c), 0.0)
        beta = jnp.where(valid, _sigmoid(gcol[:, GDN_H:2 * GDN_H]), 0.0)
        g_row = jnp.where(valid_r, -jnp.exp(alog_r) * _softplus(grow[0:GDN_H, :] + dtb_r), 0.0)
        gi_cols = _dot_sel_lhs(incl, g_col)
        gi_rows = _dot_sel_rhs(g_row, upper)
        for h in range(GDN_H):
            q = _l2norm(x[:, h * GDN_DK:(h + 1) * GDN_DK]) * (GDN_DK ** -0.5)
            k = _l2norm(x[:, qk + h * GDN_DK:qk + (h + 1) * GDN_DK])
            gh = g_col[:, h:h + 1]
            gi = gi_cols[:, h:h + 1]
            gx = gi - gh
            gi_row = gi_rows[h:h + 1, :]
            g_end = gi[c - 1:c, :]
            kb = k * beta[:, h:h + 1]
            b = -jnp.exp(gh) * kb
            e_end = jnp.exp(g_end - gi)
            heads.append(dict(
                m_lhs=jnp.concatenate([k, q], axis=0), m_rhs=jnp.concatenate([b, kb], axis=0),
                pair_x=jnp.where(strict, jnp.exp(jnp.where(strict, gx - gi_row, 0.0)), 0.0),
                pair_i=jnp.where(incl, jnp.exp(jnp.where(incl, gi - gi_row, 0.0)), 0.0),
                x_lhs=jnp.concatenate([k * jnp.exp(gx), q * jnp.exp(gi)], axis=0),
                v=x[:, 2 * qk + h * GDN_DV:2 * qk + (h + 1) * GDN_DV],
                bk=jnp.concatenate([b * e_end, kb * e_end], axis=0), s_decay=jnp.exp(g_end)))
    pairs = [(bi, h) for bi in range(bb) for h in range(GDN_H)]
    outs, new_states = _dplr_heads(heads, [s_scr[bi, h] for bi, h in pairs], c, nv, state_is_vk=False)
    for (bi, h), o, s_new in zip(pairs, outs, new_states):
        s_scr[bi, h] = s_new
        zh = z_ref[bi, :, h * GDN_DV:(h + 1) * GDN_DV]
        o = o * lax.rsqrt(jnp.mean(o * o, axis=-1, keepdims=True) + 1e-6) * nw_ref[...]
        out_ref[bi, :, h * GDN_DV:(h + 1) * GDN_DV] = o * _silu(zh)
    tail = xs_scr[:, 5 + nv:8 + nv, :]
    xs_scr[:, 5:8, :] = tail

    @pl.when(i == pl.num_programs(1) - 1)
    def _():
        snew_ref[...] = s_scr[...]
        bufnew_ref[...] = tail


def _gdn_mixer(qkv, z, gates, buf, s0, conv_w, a_log, dt_bias, norm_w):
    b, l, _ = qkv.shape
    c, n, nv, bb = _seq_dims(b, l)
    gates = _pad_seq(gates, c, n)
    pcol = jnp.stack([a_log, dt_bias])
    out, s_new, buf_new = pl.pallas_call(
        functools.partial(_gdn_kernel, bb=bb, c=c, nv=nv),
        grid=(b // bb, n),
        in_specs=[_seq_spec(bb, c, GDN_QKV), _seq_spec(bb, c, GDN_W), _seq_spec(bb, c, LANES),
                  pl.BlockSpec((bb, 1, SUBLANES, c), lambda b, i: (b, i, 0, 0)),
                  _state_spec(bb, (CONV_W - 1, GDN_QKV)), _state_spec(bb, (GDN_H, GDN_DK, GDN_DV)),
                  _const_spec((CONV_W, GDN_QKV)), _const_spec((2, GDN_H)), _const_spec((GDN_H, 2)),
                  _const_spec((1, GDN_DV))],
        out_specs=[_seq_spec(bb, c, GDN_W), _state_spec(bb, (GDN_H, GDN_DK, GDN_DV)),
                   _state_spec(bb, (CONV_W - 1, GDN_QKV))],
        out_shape=[jax.ShapeDtypeStruct((b, n * c, GDN_W), F32),
                   jax.ShapeDtypeStruct((b, GDN_H, GDN_DK, GDN_DV), F32),
                   jax.ShapeDtypeStruct((b, CONV_W - 1, GDN_QKV), F32)],
        scratch_shapes=[pltpu.VMEM((bb, SUBLANES + c, GDN_QKV), F32),
                        pltpu.VMEM((bb, GDN_H, GDN_DK, GDN_DV), F32)],
        compiler_params=_cparams("parallel", "arbitrary"),
        name="gdn",
    )(_pad_seq(qkv, c, n), _pad_seq(z, c, n), gates, _rows_layout(gates, c, n, SUBLANES), buf, s0,
      conv_w, pcol, pcol.T, norm_w.reshape(1, GDN_DV))
    return out[:, :l], s_new, buf_new


def _rwkv_kernel(rw_ref, prev_ref, s0_ref, mix_ref, w0_ref, w2_ref, a0_ref, a2_ref, g2_ref,
                 kk_ref, ka_ref, rk_ref, lnw_ref, lnb_ref, out_ref, snew_ref, xs_scr, s_scr,
                 *, bb, c, nv):
    i = pl.program_id(1)
    hd = RWKV_HD
    w = RWKV_W

    @pl.when(i == 0)
    def _():
        xs_scr[:, 0:SUBLANES, :] = jnp.zeros((bb, SUBLANES, RWKV_PROJ_W), F32)
        xs_scr[:, 7:8, :] = prev_ref[...]
        s_scr[...] = s0_ref[...]

    valid = _valid_rows(c, nv)
    incl = _tri(c, True)
    heads, post = [], []
    for bi in range(bb):
        rw = rw_ref[bi]
        xs_scr[bi, SUBLANES:SUBLANES + c, :] = rw
        prev = xs_scr[bi, 7:7 + c, :]
        xs_scr[bi, 7:8, :] = xs_scr[bi, 7 + nv:8 + nv, :]
        xs = rw + (prev - rw) * mix_ref[...]
        r_all, k_all, v_all = xs[:, 0:w], xs[:, w:2 * w], xs[:, 2 * w:3 * w]
        o1 = 3 * w
        wl = xs[:, o1:o1 + RWKV_RW]
        al = xs[:, o1 + RWKV_RW:o1 + RWKV_RW + RWKV_RA]
        gl = xs[:, o1 + RWKV_RW + RWKV_RA:]
        wdec = -_softplus(-(w0_ref[...] + _dot(jnp.tanh(wl), w2_ref[...]))) - 0.5
        a_all = _sigmoid(a0_ref[...] + _dot(al, a2_ref[...]))
        gate = _dot(_sigmoid(gl), g2_ref[...])
        lw_all = jnp.where(valid, -jnp.exp(wdec), 0.0)
        kmod = jnp.where(valid, k_all * (1.0 + (a_all - 1.0) * ka_ref[...]), 0.0)
        kk_all = k_all * kk_ref[...]
        gi = _dot_sel_lhs(incl, lw_all)
        gm = gi[c // 2:c // 2 + 1, :]
        g_end = gi[c - 1:c, :]
        e_nlw = jnp.exp(-lw_all)
        e_r = jnp.exp(gi - gm)
        e_a = e_r * e_nlw
        e_m = jnp.exp(gm - gi)
        e_gi = jnp.exp(gi)
        e_gx = e_gi * e_nlw
        e_end = jnp.exp(g_end - gm) * e_m
        s_dec = jnp.exp(g_end)
        for h in range(RWKV_H):
            sl = slice(h * hd, (h + 1) * hd)
            kk = _l2norm(kk_all[:, sl])
            r, k = r_all[:, sl], kmod[:, sl]
            a = -kk
            b = jnp.where(valid, kk * a_all[:, sl], 0.0)
            heads.append(dict(
                m_lhs=jnp.concatenate([a * e_a[:, sl], r * e_r[:, sl]], axis=0),
                m_rhs=jnp.concatenate([b * e_m[:, sl], k * e_m[:, sl]], axis=0),
                x_lhs=jnp.concatenate([a * e_gx[:, sl], r * e_gi[:, sl]], axis=0),
                v=v_all[:, sl],
                bk=jnp.concatenate([b * e_end[:, sl], k * e_end[:, sl]], axis=0), s_decay=s_dec[:, sl]))
            post.append((bi, h, r, k, v_all[:, sl], gate[:, sl]))
    outs, new_states = _dplr_heads(heads, [s_scr[bi, h] for bi, h, *_ in post], c, nv, state_is_vk=True)
    for (bi, h, r, k, v, gate_h), o, s_new in zip(post, outs, new_states):
        sl = slice(h * hd, (h + 1) * hd)
        s_scr[bi, h] = s_new
        y = _ln(o, lnw_ref[:, sl], lnb_ref[:, sl], RWKV_GN_EPS)
        y = y + jnp.sum(r * k * rk_ref[:, sl], axis=-1, keepdims=True) * v
        out_ref[bi, :, sl] = y * gate_h

    @pl.when(i == pl.num_programs(1) - 1)
    def _():
        snew_ref[...] = s_scr[...]


def _rwkv_mixer(rw, shift0, s0, mix, w0, w2, a0, a2, g2, k_k, k_a, r_k, ln_w, ln_b):
    b, l, _ = rw.shape
    c, n, nv, bb = _seq_dims(b, l)
    w = RWKV_W
    row = lambda t: t.reshape(1, -1)
    state = (RWKV_H, RWKV_HD, RWKV_HD)
    out, s_new = pl.pallas_call(
        functools.partial(_rwkv_kernel, bb=bb, c=c, nv=nv),
        grid=(b // bb, n),
        in_specs=[_seq_spec(bb, c, RWKV_PROJ_W), _state_spec(bb, (1, RWKV_PROJ_W)), _state_spec(bb, state),
                  _const_spec((1, RWKV_PROJ_W)), _const_spec((1, w)), _const_spec((RWKV_RW, w)),
                  _const_spec((1, w)), _const_spec((RWKV_RA, w)), _const_spec((RWKV_RG, w)),
                  _const_spec((1, w)), _const_spec((1, w)), _const_spec((1, w)), _const_spec((1, w)),
                  _const_spec((1, w))],
        out_specs=[_seq_spec(bb, c, w), _state_spec(bb, state)],
        out_shape=[jax.ShapeDtypeStruct((b, n * c, w), F32),
                   jax.ShapeDtypeStruct((b,) + state, F32)],
        scratch_shapes=[pltpu.VMEM((bb, SUBLANES + c, RWKV_PROJ_W), F32), pltpu.VMEM((bb,) + state, F32)],
        compiler_params=_cparams("parallel", "arbitrary"),
        name="rwkv7",
    )(_pad_seq(rw, c, n), shift0.reshape(b, 1, RWKV_PROJ_W), s0, row(mix), row(w0),
      w2.astype(BF16), row(a0), a2.astype(BF16), g2.astype(BF16), row(k_k), row(k_a), row(r_k),
      row(ln_w), row(ln_b))
    return out[:, :l], s_new


def _mlstm_kernel(p_ref, gcol_ref, grow_ref, c0_ref, n0_ref, m0_ref, bcol_ref, brow_ref, nw_ref,
                  out_ref, cnew_ref, nnew_ref, mnew_ref, c_scr, n_scr, m_scr, *, bb, c, nv):
    i = pl.program_id(1)
    nh, dk, dv = MLSTM_H, MLSTM_DK, MLSTM_DV

    @pl.when(i == 0)
    def _():
        c_scr[...] = c0_ref[...]
        n_scr[...] = n0_ref[...]
        m_scr[...] = m0_ref[...]

    valid = _valid_rows(c, nv)
    valid_r = lax.broadcasted_iota(jnp.int32, (1, c), 1) < nv
    incl = _tri(c, True)
    upper = jnp.logical_not(_tri(c, False))
    ps = [(bi, h) for bi in range(bb) for h in range(nh)]
    gate_cols = {}
    for bi in range(bb):
        gcol = gcol_ref[bi]
        grow = grow_ref[bi, 0]
        li_col = jnp.where(valid, gcol[:, 0:nh] + bcol_ref[0:1, :], NEG_BIG)
        lf_col = jnp.where(valid, -_softplus(-(gcol[:, nh:2 * nh] + bcol_ref[1:2, :])), 0.0)
        li_row = jnp.where(valid_r, grow[0:nh, :] + brow_ref[:, 0:1], NEG_BIG)
        lf_row = jnp.where(valid_r, -_softplus(-(grow[nh:2 * nh, :] + brow_ref[:, 1:2])), 0.0)
        b_cols = _dot_sel_lhs(incl, lf_col)
        b_rows = _dot_sel_rhs(lf_row, upper)
        gate_cols[bi] = (li_col, li_row, b_cols, b_rows)
    qs = [p_ref[bi, :, h * dk:(h + 1) * dk] for bi, h in ps]
    ks = [p_ref[bi, :, nh * dk + h * dk:nh * dk + (h + 1) * dk] * (dk ** -0.5) for bi, h in ps]
    vs = [p_ref[bi, :, 2 * nh * dk + h * dv:2 * nh * dk + (h + 1) * dv] for bi, h in ps]
    cms = [c_scr[bi, h] for bi, h in ps]
    nvecs = [n_scr[bi, h] for bi, h in ps]
    m_prevs = [m_scr[bi, h] for bi, h in ps]
    qks = [_dot_nt(q, k) for q, k in zip(qs, ks)]
    qcs = [_dot_nt(q, cm) for q, cm in zip(qs, cms)]
    bcs = [gate_cols[bi][2][:, h:h + 1] for bi, h in ps]
    dms = [jnp.where(incl, bc - gate_cols[bi][3][h:h + 1, :] + gate_cols[bi][1][h:h + 1, :], -jnp.inf)
           for (bi, h), bc in zip(ps, bcs)]
    m_inters = [bc + m_prev for bc, m_prev in zip(bcs, m_prevs)]
    m_ts = [jnp.maximum(mi, jnp.max(dm, axis=-1, keepdims=True)) for mi, dm in zip(m_inters, dms)]
    w_inters = [jnp.exp(mi - mt) for mi, mt in zip(m_inters, m_ts)]
    scs = [qk * jnp.exp(dm - mt) for qk, dm, mt in zip(qks, dms, m_ts)]
    scvs = [_dot(sc, v) for sc, v in zip(scs, vs)]
    m_news = [mt[c - 1:c, :] for mt in m_ts]
    b_lasts = [bc[c - 1:c, :] for bc in bcs]
    wss = [jnp.exp(b_last - bc + gate_cols[bi][0][:, h:h + 1] - m_new)
           for (bi, h), b_last, bc, m_new in zip(ps, b_lasts, bcs, m_news)]
    upds = [_dot_tn(v * ws, k) for v, ws, k in zip(vs, wss, ks)]
    for j, (bi, h) in enumerate(ps):
        num = w_inters[j] * qcs[j] + scvs[j]
        den = (w_inters[j] * jnp.sum(qs[j] * nvecs[j], axis=-1, keepdims=True)
               + jnp.sum(scs[j], axis=-1, keepdims=True))
        hh = num / jnp.maximum(jnp.abs(den), jnp.exp(-m_ts[j]))
        dec = jnp.exp(b_lasts[j] + m_prevs[j] - m_news[j])
        c_scr[bi, h] = dec * cms[j] + upds[j]
        n_scr[bi, h] = dec * nvecs[j] + jnp.sum(ks[j] * wss[j], axis=0, keepdims=True)
        m_scr[bi, h] = m_news[j]
        og = p_ref[bi, :, 2 * nh * dk + nh * dv + h * dv:2 * nh * dk + nh * dv + (h + 1) * dv]
        out_ref[bi, :, h * dv:(h + 1) * dv] = _ln(hh, nw_ref[...]) * _sigmoid(og)

    @pl.when(i == pl.num_programs(1) - 1)
    def _():
        cnew_ref[...] = c_scr[...]
        nnew_ref[...] = n_scr[...]
        mnew_ref[...] = m_scr[...]


def _mlstm_mixer(p, gates, c0, n0, m0, i_b, f_b, norm_w):
    b, l, _ = p.shape
    c, n, nv, bb = _seq_dims(b, l)
    nh, dk, dv = MLSTM_H, MLSTM_DK, MLSTM_DV
    gates = _pad_seq(gates, c, n)
    bcol = jnp.stack([i_b, f_b])
    out, c_new, n_new, m_new = pl.pallas_call(
        functools.partial(_mlstm_kernel, bb=bb, c=c, nv=nv),
        grid=(b // bb, n),
        in_specs=[_seq_spec(bb, c, p.shape[2]), _seq_spec(bb, c, LANES),
                  pl.BlockSpec((bb, 1, SUBLANES, c), lambda b, i: (b, i, 0, 0)),
                  _state_spec(bb, (nh, dv, dk)), _state_spec(bb, (nh, 1, dk)), _state_spec(bb, (nh, 1, 1)),
                  _const_spec((2, nh)), _const_spec((nh, 2)), _const_spec((1, dv))],
        out_specs=[_seq_spec(bb, c, nh * dv), _state_spec(bb, (nh, dv, dk)),
                   _state_spec(bb, (nh, 1, dk)), _state_spec(bb, (nh, 1, 1))],
        out_shape=[jax.ShapeDtypeStruct((b, n * c, nh * dv), F32),
                   jax.ShapeDtypeStruct((b, nh, dv, dk), F32),
                   jax.ShapeDtypeStruct((b, nh, 1, dk), F32),
                   jax.ShapeDtypeStruct((b, nh, 1, 1), F32)],
        scratch_shapes=[pltpu.VMEM((bb, nh, dv, dk), F32), pltpu.VMEM((bb, nh, 1, dk), F32),
                        pltpu.VMEM((bb, nh, 1, 1), F32)],
        compiler_params=_cparams("parallel", "arbitrary"),
        name="mlstm",
    )(_pad_seq(p, c, n), gates, _rows_layout(gates, c, n, SUBLANES), c0,
      n0.reshape(b, nh, 1, dk), m0.reshape(b, nh, 1, 1), bcol, bcol.T, norm_w.reshape(1, dv))
    return out[:, :l], c_new, n_new.reshape(b, nh, dk), m_new.reshape(b, nh)


def _pad_cols(w, n):
    return jnp.pad(w, ((0, 0), (0, n - w.shape[1])))


def _ab_in_weight(w):
    o = 2 * LRU_W
    xy, qkv = w[:, :o], w[:, o:o + GDN_QKV]
    o += GDN_QKV
    ab, z = w[:, o:o + 2 * GDN_H], w[:, o + 2 * GDN_H:]
    return jnp.concatenate([xy, qkv, z, _pad_cols(ab, LANES)], axis=1).astype(BF16)


AB_SPLITS = ((0, 2 * LRU_W), (2 * LRU_W, GDN_QKV), (2 * LRU_W + GDN_QKV, GDN_W),
             (2 * LRU_W + GDN_QKV + GDN_W, LANES))


def _cd_in_weight(w):
    o = RWKV_PROJ_W
    rw, qkv = w[:, :o], w[:, o:o + 2 * MLSTM_QK + MLSTM_W]
    o += 2 * MLSTM_QK + MLSTM_W
    gates, og = w[:, o:o + 2 * MLSTM_H], w[:, o + 2 * MLSTM_H:]
    return jnp.concatenate([rw, qkv, og, _pad_cols(gates, LANES)], axis=1).astype(BF16)


CD_SPLITS = ((0, RWKV_PROJ_W), (RWKV_PROJ_W, 2 * MLSTM_QK + 2 * MLSTM_W),
             (RWKV_PROJ_W + 2 * MLSTM_QK + 2 * MLSTM_W, LANES))


def kernel(x_prompt, x_sample, c_prompt, c_sample,
           state_lru_h, state_lru_conv, state_gdn_S, state_gdn_conv,
           state_rwkv_S, state_rwkv_shift, state_mlstm_C, state_mlstm_n, state_mlstm_m,
           mod_w, mod_b, ln1_g, ln1_b, ln2_g, ln2_b,
           ab_w_in, ab_w_out, lru_conv_w, lru_conv_b, lru_wr, lru_br, lru_wi, lru_bi, lru_lambda,
           gdn_conv_w, gdn_a_log, gdn_dt_bias, gdn_norm_w,
           cd_w_in, cd_w_out, rwkv_mix, rwkv_w0, rwkv_w2, rwkv_a0, rwkv_a2, rwkv_g2,
           rwkv_k_k, rwkv_k_a, rwkv_r_k, rwkv_ln_w, rwkv_ln_b,
           mlstm_i_b, mlstm_f_b, mlstm_norm_w,
           ffn_w_gate, ffn_w_up, ffn_w_down,
           router_w, moe_w_gate, moe_w_up, moe_w_down):
    d = D_MODEL
    bp, lp, _ = x_prompt.shape
    bs, ls, _ = x_sample.shape
    mod = _modulation(jnp.concatenate([c_prompt, c_sample], axis=0), mod_w, mod_b)

    ab_in = [_ab_in_weight(ab_w_in[j]) for j in range(ab_w_in.shape[0])]
    cd_in = [_cd_in_weight(cd_w_in[j]) for j in range(cd_w_in.shape[0])]
    ab_out, cd_out = ab_w_out.astype(BF16), cd_w_out.astype(BF16)
    ffn_g, ffn_u, ffn_d = (t.astype(BF16) for t in (ffn_w_gate, ffn_w_up, ffn_w_down))
    moe_g, moe_u, moe_d = (t.astype(BF16) for t in (moe_w_gate, moe_w_up, moe_w_down))

    def trunk(x, mods, batch, length, states, pos0):
        lru_h, lru_conv, gdn_s, gdn_conv, rwkv_s, rwkv_shift, m_c, m_n, m_m = states
        new = [[] for _ in range(9)]
        seq = lambda t: t.reshape(batch, length, t.shape[-1])
        tok = lambda t: t.reshape(x.shape[0], x.shape[1], t.shape[-1])
        for l in range(DEPTH):
            j = l // 2
            sh1, sc1, g1, sh2, sc2, g2 = mods[l]
            if l % 2 == 0:
                xy, qkv, z, gates = _inproj(x, sc1, sh1, ab_in[j], AB_SPLITS)
                out_a, s0, s1 = _lru_mixer(seq(xy), lru_conv[j], lru_h[j], lru_conv_w[j], lru_conv_b[j],
                                           lru_wr[j], lru_br[j], lru_wi[j], lru_bi[j], lru_lambda[j], pos0)
                out_b, s2, s3 = _gdn_mixer(seq(qkv), seq(z), seq(gates), gdn_conv[j], gdn_s[j],
                                           gdn_conv_w[j], gdn_a_log[j], gdn_dt_bias[j], gdn_norm_w[j])
                for slot, s in zip((0, 1, 2, 3), (s0, s1, s2, s3)):
                    new[slot].append(s)
                x = _outproj_ln(x, tok(out_a), tok(out_b), g1, ab_out[j], ln1_g[l], ln1_b[l])
                x = _ffn_ln(x, sc2, sh2, g2, ffn_g[j], ffn_u[j], ffn_d[j], ln2_g[l], ln2_b[l])
            else:
                rw, mp, gates = _inproj(x, sc1, sh1, cd_in[j], CD_SPLITS)
                out_c, s0 = _rwkv_mixer(seq(rw), rwkv_shift[j], rwkv_s[j], rwkv_mix[j], rwkv_w0[j],
                                        rwkv_w2[j], rwkv_a0[j], rwkv_a2[j], rwkv_g2[j], rwkv_k_k[j],
                                        rwkv_k_a[j], rwkv_r_k[j], rwkv_ln_w[j], rwkv_ln_b[j])
                out_d, s2, s3, s4 = _mlstm_mixer(seq(mp), seq(gates), m_c[j], m_n[j], m_m[j],
                                                 mlstm_i_b[j], mlstm_f_b[j], mlstm_norm_w[j])
                for slot, s in zip((4, 5, 6, 7, 8), (s0, seq(rw)[:, -1], s2, s3, s4)):
                    new[slot].append(s)
                x = _outproj_ln(x, tok(out_c), tok(out_d), g1, cd_out[j], ln1_g[l], ln1_b[l])
                moe = _moe_sparse_ln if x.shape[1] % MOE_TOKEN_BLOCK == 0 else _moe_ln
                x = moe(x, sc2, sh2, g2, router_w[j], moe_g[j], moe_u[j], moe_d[j], ln2_g[l], ln2_b[l])
        return x, tuple(jnp.stack(s) for s in new)

    def zeros(ref):
        return jnp.zeros((ref.shape[0], bp) + ref.shape[2:], F32)

    mods_p = [[mod[l, :bp, k * d:(k + 1) * d].reshape(bp, 1, d) for k in range(6)] for l in range(DEPTH)]
    mods_s = [[mod[l, bp:, k * d:(k + 1) * d].reshape(1, bs * ls, d) for k in range(6)] for l in range(DEPTH)]
    states_s = (state_lru_h, state_lru_conv, state_gdn_S, state_gdn_conv, state_rwkv_S,
                state_rwkv_shift, state_mlstm_C, state_mlstm_n, state_mlstm_m)
    y_p, new_p = trunk(x_prompt, mods_p, bp, lp, tuple(zeros(s) for s in states_s), 0)
    y_s, new_s = trunk(x_sample.reshape(1, bs * ls, d), mods_s, bs, ls, states_s, PAST_LEN)
    out = [y_p, y_s.reshape(bs, ls, d)]
    for p_leaf, s_leaf in zip(new_p, new_s):
        out += [p_leaf, s_leaf]
    return tuple(out)
```

```python
import functools
import math

import jax
import jax.numpy as jnp
from jax import lax
from jax.experimental import pallas as pl
from jax.experimental.pallas import tpu as pltpu

F32 = jnp.float32
BF16 = jnp.bfloat16

D_MODEL = 1024
DEPTH = 4
PAST_LEN = 16384
CONV_W = 4
LRU_W = D_MODEL // 2
LRU_BLOCKS = 8
LRU_BW = LRU_W // LRU_BLOCKS
LRU_C = 8.0
GDN_H = D_MODEL // 256
GDN_DK = 128
GDN_DV = 128
GDN_QK = GDN_H * GDN_DK
GDN_W = GDN_H * GDN_DV
GDN_QKV = 2 * GDN_QK + GDN_W
RWKV_HD = 64
RWKV_H = D_MODEL // 2 // RWKV_HD
RWKV_W = RWKV_H * RWKV_HD
RWKV_RW = 64
RWKV_RA = 64
RWKV_RG = 128
RWKV_PROJ_W = 3 * RWKV_W + RWKV_RW + RWKV_RA + RWKV_RG
RWKV_GN_EPS = 64e-5
MLSTM_H = D_MODEL // 256
MLSTM_DK = 128
MLSTM_DV = 128
MLSTM_QK = MLSTM_H * MLSTM_DK
MLSTM_W = MLSTM_H * MLSTM_DV
D_FF = 7 * D_MODEL // 2
N_EXPERTS = 8
LN_EPS = 1e-5
NEG_BIG = -1e30
DN_ALPHA = (2.0 * DEPTH) ** 0.25

LANES = 128
SUBLANES = 8
SEQ_CHUNK = 64
SHORT_SEQ_BLOCK = 8
LRU_SEQS_PER_STEP = 8
GDN_SEQS_PER_STEP = 4
RWKV_SEQS_PER_STEP = 2
MLSTM_SEQS_PER_STEP = 1
ROW_TILE = 512
FF_TILE = 512
MOE_TOKEN_BLOCK = 512
MOE_SLOT_CHUNK = 128
MOE_FF_SPLIT = 2
MOE_WINDOW = 256
MOE_WINDOWS = 3
VMEM_LIMIT = 48 * 1024 * 1024


def _cparams(*sem):
    return pltpu.CompilerParams(dimension_semantics=sem, vmem_limit_bytes=VMEM_LIMIT)


def _dot(a, b):
    return jnp.dot(a.astype(BF16), b.astype(BF16), preferred_element_type=F32)


def _dot_nt(a, b):
    return lax.dot_general(a.astype(BF16), b.astype(BF16), (((1,), (1,)), ((), ())),
                           preferred_element_type=F32)


def _dot_tn(a, b):
    return lax.dot_general(a.astype(BF16), b.astype(BF16), (((0,), (0,)), ((), ())),
                           preferred_element_type=F32)


def _split3(x):
    hi = x.astype(BF16)
    r1 = x - hi.astype(F32)
    mid = r1.astype(BF16)
    lo = (r1 - mid.astype(F32)).astype(BF16)
    return hi, mid, lo


def _dot_sel_lhs(t, x):
    tb = jnp.where(t, 1.0, 0.0).astype(BF16)
    hi, mid, lo = _split3(x)
    d = lambda p: jnp.dot(tb, p, preferred_element_type=F32)
    return d(hi) + d(mid) + d(lo)


def _dot_sel_rhs(x, t):
    tb = jnp.where(t, 1.0, 0.0).astype(BF16)
    hi, mid, lo = _split3(x)
    d = lambda p: jnp.dot(p, tb, preferred_element_type=F32)
    return d(hi) + d(mid) + d(lo)


def _dot2(p, x):
    pb = p.astype(BF16)
    xh = x.astype(BF16)
    xl = (x - xh.astype(F32)).astype(BF16)
    return (jnp.dot(pb, xh, preferred_element_type=F32)
            + jnp.dot(pb, xl, preferred_element_type=F32))


def _sigmoid(x):
    return 1.0 / (1.0 + jnp.exp(-x))


def _silu(x):
    return x * _sigmoid(x)


def _softplus(x):
    return jnp.maximum(x, 0.0) + jnp.log1p(jnp.exp(-jnp.abs(x)))


def _gelu_tanh(x):
    return 0.5 * x * (1.0 + jnp.tanh(math.sqrt(2.0 / math.pi) * (x + 0.044715 * (x * x * x))))


def _ln(y, g=None, b=None, eps=LN_EPS):
    mu = jnp.mean(y, axis=-1, keepdims=True)
    d = y - mu
    var = jnp.mean(d * d, axis=-1, keepdims=True)
    out = d * lax.rsqrt(var + eps)
    if g is not None:
        out = out * g
    if b is not None:
        out = out + b
    return out


def _l2norm(x, eps=1e-6):
    return x * lax.rsqrt(jnp.sum(x * x, axis=-1, keepdims=True) + eps)


def _tri(c, inclusive):
    t = lax.broadcasted_iota(jnp.int32, (c, c), 0)
    s = lax.broadcasted_iota(jnp.int32, (c, c), 1)
    return (s <= t) if inclusive else (s < t)


def _unit_lower_solve(ns, xs, c):
    steps = max(1, int(math.ceil(math.log2(c))))
    for i in range(steps):
        xs = [x + _dot2(p, x) for p, x in zip(ns, xs)]
        if i + 1 < steps:
            ns = [_dot(p, p) for p in ns]
    return xs


def _dplr_heads(heads, states, c, nv, state_is_vk):
    incl = _tri(c, True)
    strict = _tri(c, False)
    ms = [_dot_nt(h["m_lhs"], h["m_rhs"]) for h in heads]
    if state_is_vk:
        xhs = [_dot_nt(h["x_lhs"], s) for h, s in zip(heads, states)]
    else:
        xhs = [_dot(h["x_lhs"], s) for h, s in zip(heads, states)]
    a_abs, a_aks, r_bks = [], [], []
    for h, m in zip(heads, ms):
        if "pair_x" in h:
            a_abs.append(m[:c, :c] * h["pair_x"])
            a_aks.append(m[:c, c:] * h["pair_x"])
            r_bks.append(jnp.concatenate([m[c:, :c] * h["pair_i"], m[c:, c:] * h["pair_i"]], axis=1))
        else:
            a_abs.append(jnp.where(strict, m[:c, :c], 0.0))
            a_aks.append(jnp.where(strict, m[:c, c:], 0.0))
            r_bks.append(jnp.concatenate([jnp.where(incl, m[c:, :c], 0.0),
                                          jnp.where(incl, m[c:, c:], 0.0)], axis=1))
    if nv == 1:
        us = [xh[:c] for xh in xhs]
    else:
        rhs = [xh[:c] + _dot(a_ak, h["v"]) for xh, a_ak, h in zip(xhs, a_aks, heads)]
        us = _unit_lower_solve(a_abs, rhs, c)
    uvs = [jnp.concatenate([u, h["v"]], axis=0) for u, h in zip(us, heads)]
    outs = [xh[c:] + _dot(r_bk, uv) for xh, r_bk, uv in zip(xhs, r_bks, uvs)]
    if state_is_vk:
        new = [s * h["s_decay"] + _dot_tn(uv, h["bk"]) for s, h, uv in zip(states, heads, uvs)]
    else:
        new = [s * h["s_decay"] + _dot_tn(h["bk"], uv) for s, h, uv in zip(states, heads, uvs)]
    return outs, new


def _valid_rows(c, n_valid):
    return lax.broadcasted_iota(jnp.int32, (c, 1), 0) < n_valid


def _mod_kernel(c_ref, w_ref, b_ref, o_ref):
    o_ref[0] = _dot(_silu(c_ref[...]), w_ref[0]) + b_ref[0]


def _modulation(c_all, mod_w, mod_b):
    n = c_all.shape[0]
    d = D_MODEL
    return pl.pallas_call(
        _mod_kernel,
        grid=(DEPTH, 6),
        in_specs=[pl.BlockSpec((n, d), lambda l, j: (0, 0)),
                  pl.BlockSpec((1, d, d), lambda l, j: (l, 0, j)),
                  pl.BlockSpec((1, 1, d), lambda l, j: (l, 0, j))],
        out_specs=pl.BlockSpec((1, n, d), lambda l, j: (l, 0, j)),
        out_shape=jax.ShapeDtypeStruct((DEPTH, n, 6 * d), F32),
        compiler_params=_cparams("parallel", "parallel"),
        name="modulation",
    )(c_all, mod_w, mod_b.reshape(DEPTH, 1, 6 * d))


def _mod_spec(mod, tm):
    if mod.shape[1] == 1:
        return pl.BlockSpec((1, 1, mod.shape[2]), lambda g, i, *_: (g, 0, 0))
    return pl.BlockSpec((1, tm, mod.shape[2]), lambda g, i, *_: (g, i, 0))


def _inproj_kernel(x_ref, sc_ref, sh_ref, w_ref, *o_refs, splits):
    h = (x_ref[0] * (1.0 + sc_ref[0]) + sh_ref[0]).astype(BF16)
    for o_ref, (s, n) in zip(o_refs, splits):
        o_ref[0] = jnp.dot(h, w_ref[:, s:s + n], preferred_element_type=F32)


def _inproj(x, sc, sh, w, splits):
    g, r, d = x.shape
    tm = min(r, ROW_TILE)
    n_all = w.shape[1]
    return pl.pallas_call(
        functools.partial(_inproj_kernel, splits=splits),
        grid=(g, r // tm),
        in_specs=[pl.BlockSpec((1, tm, d), lambda g, i: (g, i, 0)),
                  _mod_spec(sc, tm), _mod_spec(sh, tm),
                  pl.BlockSpec((d, n_all), lambda g, i: (0, 0))],
        out_specs=[pl.BlockSpec((1, tm, n), lambda g, i: (g, i, 0)) for _, n in splits],
        out_shape=[jax.ShapeDtypeStruct((g, r, n), F32) for _, n in splits],
        compiler_params=_cparams("parallel", "parallel"),
        name="inproj",
    )(x, sc, sh, w)


def _outproj_ln_kernel(x_ref, ma_ref, mb_ref, gate_ref, w_ref, lng_ref, lnb_ref, o_ref):
    half = ma_ref.shape[2]
    f = _dot(ma_ref[0], w_ref[0:half, :]) + _dot(mb_ref[0], w_ref[half:, :])
    y = DN_ALPHA * x_ref[0] + (1.0 + gate_ref[0]) * f
    o_ref[0] = _ln(y, lng_ref[...], lnb_ref[...])


def _outproj_ln(x, mix_a, mix_b, gate, w, layer, ln_g, ln_b):
    g, r, d = x.shape
    tm = min(r, ROW_TILE)
    half = mix_a.shape[2]
    row = pl.BlockSpec((1, tm, d), lambda g, i: (g, i, 0))
    mrow = pl.BlockSpec((1, tm, half), lambda g, i: (g, i, 0))
    vec = pl.BlockSpec((1, d), lambda g, i: (0, 0))
    return pl.pallas_call(
        _outproj_ln_kernel,
        grid=(g, r // tm),
        in_specs=[row, mrow, mrow, _mod_spec(gate, tm),
                  pl.BlockSpec((None, 2 * half, d), lambda g, i: (layer, 0, 0)), vec, vec],
        out_specs=row,
        out_shape=jax.ShapeDtypeStruct((g, r, d), F32),
        compiler_params=_cparams("parallel", "parallel"),
        name="outproj_ln",
    )(x, mix_a, mix_b, gate, w, ln_g.reshape(1, d), ln_b.reshape(1, d))


def _ffn_kernel(x_ref, sc_ref, sh_ref, gate_ref, wg_ref, wu_ref, wd_ref, lng_ref, lnb_ref,
                o_ref, h_scr, acc_scr):
    j = pl.program_id(2)

    @pl.when(j == 0)
    def _():
        h_scr[...] = (x_ref[0] * (1.0 + sc_ref[0]) + sh_ref[0]).astype(BF16)
        acc_scr[...] = jnp.zeros_like(acc_scr)

    h = h_scr[...]
    a = jnp.dot(h, wg_ref[...], preferred_element_type=F32)
    u = jnp.dot(h, wu_ref[...], preferred_element_type=F32)
    acc_scr[...] += _dot(_silu(a) * u, wd_ref[...])

    @pl.when(j == pl.num_programs(2) - 1)
    def _():
        y = DN_ALPHA * x_ref[0] + (1.0 + gate_ref[0]) * acc_scr[...]
        o_ref[0] = _ln(y, lng_ref[...], lnb_ref[...])


def _ffn_ln(x, sc, sh, gate, wg, wu, wd, layer, ln_g, ln_b):
    g, r, d = x.shape
    tm = min(r, ROW_TILE)
    f = wg.shape[2]
    row = pl.BlockSpec((1, tm, d), lambda g, i, j: (g, i, 0))
    vec = pl.BlockSpec((1, d), lambda g, i, j: (0, 0))
    return pl.pallas_call(
        _ffn_kernel,
        grid=(g, r // tm, f // FF_TILE),
        in_specs=[row, _mod_spec(sc, tm), _mod_spec(sh, tm), _mod_spec(gate, tm),
                  pl.BlockSpec((None, d, FF_TILE), lambda g, i, j: (layer, 0, j)),
                  pl.BlockSpec((None, d, FF_TILE), lambda g, i, j: (layer, 0, j)),
                  pl.BlockSpec((None, FF_TILE, d), lambda g, i, j: (layer, j, 0)), vec, vec],
        out_specs=row,
        out_shape=jax.ShapeDtypeStruct((g, r, d), F32),
        scratch_shapes=[pltpu.VMEM((tm, d), BF16), pltpu.VMEM((tm, d), F32)],
        compiler_params=_cparams("parallel", "parallel", "arbitrary"),
        name="ffn_ln",
    )(x, sc, sh, gate, wg, wu, wd, ln_g.reshape(1, d), ln_b.reshape(1, d))


def _top2_route(h, rw, lane):
    logits = jnp.dot(h, rw, preferred_element_type=F32, precision=lax.Precision.HIGHEST)
    logits = jnp.where(lane < N_EXPERTS, logits, -jnp.inf)
    m1 = jnp.max(logits, axis=-1, keepdims=True)
    i1 = jnp.min(jnp.where(logits == m1, lane, LANES), axis=-1, keepdims=True)
    rest = jnp.where(lane == i1, -jnp.inf, logits)
    m2 = jnp.max(rest, axis=-1, keepdims=True)
    i2 = jnp.min(jnp.where(rest == m2, lane, LANES), axis=-1, keepdims=True)
    e2 = jnp.exp(m2 - m1)
    g1 = 1.0 / (1.0 + e2)
    g2 = e2 / (1.0 + e2)
    sel = (lane == i1) | (lane == i2)
    return sel, jnp.where(lane == i1, g1, 0.0) + jnp.where(lane == i2, g2, 0.0)


def _moe_kernel(x_ref, sc_ref, sh_ref, gate_ref, rw_ref, wg_ref, wu_ref, wd_ref, lng_ref, lnb_ref,
                o_ref, h_scr, comb_scr, acc_scr):
    e = pl.program_id(2)
    j = pl.program_id(3)
    lane = lax.broadcasted_iota(jnp.int32, comb_scr.shape, 1)

    @pl.when((e == 0) & (j == 0))
    def _():
        h = x_ref[0] * (1.0 + sc_ref[0]) + sh_ref[0]
        h_scr[...] = h.astype(BF16)
        acc_scr[...] = jnp.zeros_like(acc_scr)
        _, comb_scr[...] = _top2_route(h, rw_ref[...], lane)

    h = h_scr[...]
    a = jnp.dot(h, wg_ref[0], preferred_element_type=F32)
    u = jnp.dot(h, wu_ref[0], preferred_element_type=F32)
    comb_e = jnp.sum(jnp.where(lane == e, comb_scr[...], 0.0), axis=-1, keepdims=True)
    acc_scr[...] += comb_e * _dot(_silu(a) * u, wd_ref[0])

    @pl.when((e == pl.num_programs(2) - 1) & (j == pl.num_programs(3) - 1))
    def _():
        y = DN_ALPHA * x_ref[0] + (1.0 + gate_ref[0]) * acc_scr[...]
        o_ref[0] = _ln(y, lng_ref[...], lnb_ref[...])


def _moe_ln(x, sc, sh, gate, router_w, wg, wu, wd, layer, ln_g, ln_b):
    g, r, d = x.shape
    tm = min(r, ROW_TILE)
    _, ne, _, f = wg.shape
    row = pl.BlockSpec((1, tm, d), lambda g, i, e, j: (g, i, 0))
    vec = pl.BlockSpec((1, d), lambda g, i, e, j: (0, 0))
    rw = jnp.pad(router_w, ((0, 0), (0, LANES - ne)))
    return pl.pallas_call(
        _moe_kernel,
        grid=(g, r // tm, ne, f // FF_TILE),
        in_specs=[row, _mod_spec(sc, tm), _mod_spec(sh, tm), _mod_spec(gate, tm),
                  pl.BlockSpec((d, LANES), lambda g, i, e, j: (0, 0)),
                  pl.BlockSpec((None, 1, d, FF_TILE), lambda g, i, e, j: (layer, e, 0, j)),
                  pl.BlockSpec((None, 1, d, FF_TILE), lambda g, i, e, j: (layer, e, 0, j)),
                  pl.BlockSpec((None, 1, FF_TILE, d), lambda g, i, e, j: (layer, e, j, 0)), vec, vec],
        out_specs=row,
        out_shape=jax.ShapeDtypeStruct((g, r, d), F32),
        scratch_shapes=[pltpu.VMEM((tm, d), BF16), pltpu.VMEM((tm, LANES), F32),
                        pltpu.VMEM((tm, d), F32)],
        compiler_params=_cparams("parallel", "parallel", "arbitrary", "arbitrary"),
        name="moe_ln",
    )(x, sc, sh, gate, rw, wg, wu, wd, ln_g.reshape(1, d), ln_b.reshape(1, d))


def _moe_route_kernel(x_ref, sc_ref, sh_ref, rw_ref, hb_ref, comb_ref, rank_ref, rankt_ref, cnt_ref,
                      run_scr):
    @pl.when(pl.program_id(0) == 0)
    def _():
        run_scr[...] = jnp.zeros_like(run_scr)

    h = x_ref[0] * (1.0 + sc_ref[0]) + sh_ref[0]
    hb_ref[0] = h.astype(BF16)
    tb = h.shape[0]
    lane = lax.broadcasted_iota(jnp.int32, (tb, LANES), 1)
    sel, comb = _top2_route(h, rw_ref[...], lane)
    comb_ref[0] = comb
    ones = jnp.where(sel, 1.0, 0.0)
    before = jnp.dot(jnp.where(_tri(tb, False), 1.0, 0.0).astype(BF16), ones.astype(BF16),
                     preferred_element_type=F32)
    rank = jnp.where(sel, before + run_scr[...], -1.0)
    rank_ref[0] = rank
    rankt_ref[0] = rank.T[0:SUBLANES, :]
    cnt = jnp.sum(ones, axis=0, keepdims=True)
    cnt_ref[0] = cnt
    run_scr[...] += cnt


def _moe_route(xb, sc, sh, router_w, blocks_per_seq):
    nb, tb, d = xb.shape
    seq = lambda j: (j // blocks_per_seq, 0, 0)
    blk = lambda width: pl.BlockSpec((1, tb, width), lambda j: (j, 0, 0))
    return pl.pallas_call(
        _moe_route_kernel,
        grid=(nb,),
        in_specs=[blk(d), pl.BlockSpec((1, 1, d), seq), pl.BlockSpec((1, 1, d), seq),
                  pl.BlockSpec((d, LANES), lambda j: (0, 0))],
        out_specs=[blk(d), blk(LANES), blk(LANES),
                   pl.BlockSpec((1, SUBLANES, tb), lambda j: (j, 0, 0)),
                   pl.BlockSpec((1, 1, LANES), lambda j: (j, 0, 0))],
        out_shape=[jax.ShapeDtypeStruct((nb, tb, d), BF16),
                   jax.ShapeDtypeStruct((nb, tb, LANES), F32),
                   jax.ShapeDtypeStruct((nb, tb, LANES), F32),
                   jax.ShapeDtypeStruct((nb, SUBLANES, tb), F32),
                   jax.ShapeDtypeStruct((nb, 1, LANES), F32)],
        scratch_shapes=[pltpu.VMEM((1, LANES), F32)],
        compiler_params=_cparams("arbitrary"),
        name="moe_route",
    )(xb, sc, sh, jnp.pad(router_w, ((0, 0), (0, LANES - router_w.shape[1]))))


def _moe_tables(cnt, n_chunks, n_items):
    nb, ne = cnt.shape
    c = MOE_SLOT_CHUNK
    off = jnp.cumsum(cnt, axis=0) - cnt
    total = jnp.sum(cnt, axis=0)
    nch = (total + c - 1) // c
    ends = jnp.cumsum(nch)
    k = jnp.arange(n_chunks, dtype=jnp.int32)
    ce = jnp.minimum(jnp.searchsorted(ends, k, side="right"), ne - 1).astype(jnp.int32)
    cvalid = k < ends[-1]
    r0 = (k - (ends - nch)[ce]) * c
    lo = jnp.maximum(r0[:, None], off.T[ce])
    hi = jnp.minimum(r0[:, None] + c, (off + cnt).T[ce])
    overlap = cvalid[:, None] & (lo < hi)
    n_pairs = jnp.sum(overlap)
    pos = jnp.arange(n_items, dtype=jnp.int32)

    flat = jnp.nonzero(overlap.reshape(-1), size=n_items, fill_value=0)[0].astype(jnp.int32)
    flat = jnp.where(pos < n_pairs, flat, flat[jnp.maximum(n_pairs - 1, 0)])
    chunk, block = flat // nb, flat % nb
    valid = pos < n_pairs
    first = valid & ((pos == 0) | (chunk != jnp.roll(chunk, 1)))
    last = valid & ((pos == n_pairs - 1) | (chunk != jnp.roll(chunk, -1)))
    spare_chunk = ends[-1] + (pos - n_pairs)
    fill = (~valid) & (spare_chunk < n_chunks)
    chunk = jnp.where(valid, chunk, jnp.minimum(spare_chunk, n_chunks - 1)).astype(jnp.int32)
    i32 = lambda t: t.astype(jnp.int32)
    by_chunk = dict(chunk=chunk, block=block, expert=ce[chunk], r0=r0[chunk], first=i32(first | fill),
                    last=i32(last | fill), valid=i32(valid))
    region = (ends - nch) * c
    start8 = (region[None, :] + off) // SUBLANES
    r0_window = start8 * SUBLANES - region[None, :]
    rows_used = jnp.where(cnt > 0, off + cnt - r0_window, 0)
    need = (rows_used + MOE_WINDOW - 1) // MOE_WINDOW
    return (ce, i32(cvalid), by_chunk, i32(start8).reshape(-1), i32(r0_window).reshape(-1),
            i32(need).reshape(-1))


def _moe_gather_kernel(chunk_ref, block_ref, expert_ref, r0_ref, first_ref, last_ref, valid_ref,
                       hb_ref, rankt_ref, xs_ref, acc_scr):
    w = pl.program_id(0)

    @pl.when(first_ref[w] == 1)
    def _():
        acc_scr[...] = jnp.zeros_like(acc_scr)

    @pl.when(valid_ref[w] == 1)
    def _():
        c = acc_scr.shape[0]
        rank = rankt_ref[0, pl.ds(expert_ref[w], 1), :]
        slot = lax.broadcasted_iota(jnp.int32, (c, 1), 0).astype(F32) + r0_ref[w].astype(F32)
        pick = jnp.where(rank == slot, 1.0, 0.0).astype(BF16)
        acc_scr[...] += jnp.dot(pick, hb_ref[0], preferred_element_type=F32)

    @pl.when(last_ref[w] == 1)
    def _():
        xs_ref[0] = acc_scr[...].astype(BF16)


def _moe_gather(items, hb, rankt, n_chunks):
    nb, tb, d = hb.shape
    c = MOE_SLOT_CHUNK
    names = ("chunk", "block", "expert", "r0", "first", "last", "valid")
    by_block = lambda shape: pl.BlockSpec(shape, lambda w, ch, bl, *_: (bl[w], 0, 0))
    return pl.pallas_call(
        _moe_gather_kernel,
        grid_spec=pltpu.PrefetchScalarGridSpec(
            num_scalar_prefetch=len(names), grid=(items["chunk"].shape[0],),
            in_specs=[by_block((1, tb, d)), by_block((1, SUBLANES, tb))],
            out_specs=pl.BlockSpec((1, c, d), lambda w, ch, *_: (ch[w], 0, 0)),
            scratch_shapes=[pltpu.VMEM((c, d), F32)]),
        out_shape=jax.ShapeDtypeStruct((n_chunks, c, d), BF16),
        compiler_params=_cparams("arbitrary"),
        name="moe_gather",
    )(*[items[n] for n in names], hb, rankt)


def _moe_expert_kernel(ce_ref, cvalid_ref, xs_ref, wg_ref, wu_ref, wd_ref, *rest):
    o_ref = rest[-1]
    k = pl.program_id(0)

    @pl.when(cvalid_ref[k] == 1)
    def _():
        x = xs_ref[0]
        a = jnp.dot(x, wg_ref[0], preferred_element_type=F32)
        u = jnp.dot(x, wu_ref[0], preferred_element_type=F32)
        y = _dot(_silu(a) * u, wd_ref[0])
        o_ref[0] = y if len(rest) == 1 else rest[0][0] + y

    @pl.when(cvalid_ref[k] == 0)
    def _():
        o_ref[0] = jnp.zeros(o_ref.shape[1:], F32)


def _moe_experts(ce, cvalid, xs, wg, wu, wd, layer):
    n_chunks, c, d = xs.shape
    fh = wg.shape[3] // MOE_FF_SPLIT
    y = None
    for half in range(MOE_FF_SPLIT):
        row = pl.BlockSpec((1, c, d), lambda k, ce, cv: (k, 0, 0))
        in_specs = [row,
                    pl.BlockSpec((None, 1, d, fh), lambda k, ce, cv, half=half: (layer, ce[k], 0, half)),
                    pl.BlockSpec((None, 1, d, fh), lambda k, ce, cv, half=half: (layer, ce[k], 0, half)),
                    pl.BlockSpec((None, 1, fh, d), lambda k, ce, cv, half=half: (layer, ce[k], half, 0))]
        args = [ce, cvalid, xs, wg, wu, wd]
        aliases = {}
        if y is not None:
            in_specs.append(row)
            args.append(y)
            aliases = {len(args) - 1: 0}
        y = pl.pallas_call(
            _moe_expert_kernel,
            grid_spec=pltpu.PrefetchScalarGridSpec(
                num_scalar_prefetch=2, grid=(n_chunks,), in_specs=in_specs, out_specs=row),
            out_shape=jax.ShapeDtypeStruct((n_chunks, c, d), F32),
            input_output_aliases=aliases,
            compiler_params=_cparams("arbitrary"),
            name="moe_experts",
        )(*args)
    return y


def _moe_combine_kernel(start_ref, r0_ref, need_ref, *refs):
    win_refs = refs[:MOE_WINDOWS]
    rank_ref, comb_ref, x_ref, gate_ref, lng_ref, lnb_ref, o_ref, acc_scr = refs[MOE_WINDOWS:]
    j, e = pl.program_id(0), pl.program_id(1)
    w = j * pl.num_programs(1) + e

    @pl.when(e == 0)
    def _():
        acc_scr[...] = jnp.zeros_like(acc_scr)

    tb = acc_scr.shape[0]
    for k, ys_ref in enumerate(win_refs):
        @pl.when(need_ref[w] > k)
        def _(k=k, ys_ref=ys_ref):
            lane = lax.broadcasted_iota(jnp.int32, (tb, LANES), 1)
            mine = lane == e
            rank = jnp.sum(jnp.where(mine, rank_ref[0], 0.0), axis=-1, keepdims=True)
            comb = jnp.sum(jnp.where(mine, comb_ref[0], 0.0), axis=-1, keepdims=True)
            first = (r0_ref[w] + k * MOE_WINDOW).astype(F32)
            slot = lax.broadcasted_iota(jnp.int32, (1, MOE_WINDOW), 1).astype(F32) + first
            pick = jnp.where(rank == slot, 1.0, 0.0).astype(BF16)
            y = ys_ref[...]
            y_hi = y.astype(BF16)
            y_lo = (y - y_hi.astype(F32)).astype(BF16)
            rows = (jnp.dot(pick, y_hi, preferred_element_type=F32)
                    + jnp.dot(pick, y_lo, preferred_element_type=F32))
            acc_scr[...] += comb * rows

    @pl.when(e == pl.num_programs(1) - 1)
    def _():
        y = DN_ALPHA * x_ref[0] + (1.0 + gate_ref[0]) * acc_scr[...]
        o_ref[0] = _ln(y, lng_ref[...], lnb_ref[...])


def _moe_combine_ln(start8, r0, need, ys, rank, comb, xb, gate, ln_g, ln_b, ne, blocks_per_seq):
    nb, tb, d = xb.shape
    by_block = lambda width: pl.BlockSpec((1, tb, width), lambda j, e, *_: (j, 0, 0))
    vec = pl.BlockSpec((1, d), lambda j, e, *_: (0, 0))
    window = lambda k: pl.BlockSpec(
        (pl.Element(MOE_WINDOW), pl.Element(d)),
        lambda j, e, start8, *_: ((start8[j * ne + e] + k * (MOE_WINDOW // SUBLANES)) * SUBLANES, 0))
    ys2 = ys.reshape(-1, d)
    return pl.pallas_call(
        _moe_combine_kernel,
        grid_spec=pltpu.PrefetchScalarGridSpec(
            num_scalar_prefetch=3, grid=(nb, ne),
            in_specs=[window(k) for k in range(MOE_WINDOWS)] + [
                by_block(LANES), by_block(LANES), by_block(d),
                pl.BlockSpec((1, 1, d), lambda j, e, *_: (j // blocks_per_seq, 0, 0)), vec, vec],
            out_specs=by_block(d),
            scratch_shapes=[pltpu.VMEM((tb, d), F32)]),
        out_shape=jax.ShapeDtypeStruct((nb, tb, d), F32),
        compiler_params=_cparams("arbitrary", "arbitrary"),
        name="moe_combine_ln",
    )(start8, r0, need, *([ys2] * MOE_WINDOWS), rank, comb, xb, gate, ln_g.reshape(1, d),
      ln_b.reshape(1, d))


def _moe_sparse_ln(x, sc, sh, gate, router_w, wg, wu, wd, layer, ln_g, ln_b):
    g, r, d = x.shape
    tb, c, ne = MOE_TOKEN_BLOCK, MOE_SLOT_CHUNK, wg.shape[1]
    assert sc.shape[1] == 1 and r % tb == 0
    nb = g * r // tb
    assert MOE_WINDOWS * MOE_WINDOW >= tb + SUBLANES and (MOE_WINDOWS * MOE_WINDOW) % c == 0
    n_chunks = 2 * g * r // c + ne + MOE_WINDOWS * MOE_WINDOW // c
    n_items = n_chunks + ne * nb
    xb = x.reshape(nb, tb, d)
    hb, comb, rank, rankt, cnt = _moe_route(xb, sc, sh, router_w, r // tb)
    ce, cvalid, by_chunk, start8, r0, need = _moe_tables(cnt[:, 0, :ne].astype(jnp.int32), n_chunks, n_items)
    xs = _moe_gather(by_chunk, hb, rankt, n_chunks)
    ys = _moe_experts(ce, cvalid, xs, wg, wu, wd, layer)
    out = _moe_combine_ln(start8, r0, need, ys, rank, comb, xb, gate, ln_g, ln_b, ne, r // tb)
    return out.reshape(g, r, d)


def _seq_dims(b, l, long_seqs_per_step):
    if l >= SEQ_CHUNK:
        assert l % SEQ_CHUNK == 0 and b % long_seqs_per_step == 0
        return SEQ_CHUNK, l // SEQ_CHUNK, SEQ_CHUNK, long_seqs_per_step
    assert l <= SUBLANES and b % SHORT_SEQ_BLOCK == 0
    return SUBLANES, 1, l, SHORT_SEQ_BLOCK


def _pad_seq(t, c, n):
    pad = c * n - t.shape[1]
    return t if pad == 0 else jnp.pad(t, ((0, 0), (0, pad), (0, 0)))


def _rows_layout(t, c, n, width):
    b = t.shape[0]
    return jnp.swapaxes(t[:, :, :width].reshape(b, n, c, width), 2, 3)


def _conv_window(scr, u, taps, c):
    scr[SUBLANES:SUBLANES + c, :] = u
    out = scr[5:5 + c, :] * taps[0:1, :]
    for j in range(1, CONV_W):
        out = out + scr[5 + j:5 + j + c, :] * taps[j:j + 1, :]
    return out


def _seq_spec(bb, c, width):
    return pl.BlockSpec((bb, c, width), lambda b, i: (b, i, 0))


def _state_spec(bb, shape):
    return pl.BlockSpec((bb,) + shape, lambda b, i: (b,) + (0,) * len(shape))


def _const_spec(shape):
    return pl.BlockSpec(shape, lambda b, i: (0,) * len(shape))


def _lru_kernel(xy_ref, buf_ref, h0_ref, cw_ref, cb_ref, wg_ref, bg_ref, lam_ref,
                out_ref, hnew_ref, bufnew_ref, xs_scr, a_scr, b_scr, h_scr, *, bb, c, nv, pos0):
    i = pl.program_id(1)
    w = LRU_W

    @pl.when(i == 0)
    def _():
        xs_scr[:, 0:SUBLANES, :] = jnp.zeros((bb, SUBLANES, w), F32)
        xs_scr[:, 5:8, :] = buf_ref[...]
        h_scr[...] = h0_ref[...]

    pos = lax.broadcasted_iota(jnp.int32, (c, 1), 0) + (i * c + pos0)
    for bi in range(bb):
        xc = _conv_window(xs_scr.at[bi], xy_ref[bi, :, 0:w], cw_ref[...], c) + cb_ref[...]
        gates = _dot(xc, wg_ref[...]) + bg_ref[...]
        r = _sigmoid(gates[:, 0:w])
        ig = _sigmoid(gates[:, w:2 * w])
        log_a = -LRU_C * r * _softplus(-lam_ref[...])
        mult = jnp.sqrt(-jnp.tanh(log_a) * (jnp.exp(2.0 * log_a) + 1.0))
        mult = jnp.where(pos == 0, 1.0, mult)
        a_scr[bi] = jnp.exp(log_a)
        b_scr[bi] = xc * ig * mult

    def step(t, h):
        h = a_scr[:, pl.ds(t, 1), :] * h + b_scr[:, pl.ds(t, 1), :]
        b_scr[:, pl.ds(t, 1), :] = h
        return h

    h = lax.fori_loop(0, nv, step, h_scr[...], unroll=min(nv, SUBLANES))
    h_scr[...] = h
    out_ref[...] = b_scr[...] * _gelu_tanh(xy_ref[:, :, w:2 * w])
    tail = xs_scr[:, 5 + nv:8 + nv, :]
    xs_scr[:, 5:8, :] = tail

    @pl.when(i == pl.num_programs(1) - 1)
    def _():
        hnew_ref[...] = h
        bufnew_ref[...] = tail


def _lru_mixer(xy, buf, h0, conv_w, conv_b, wr, br, wi, bi, lam, pos0):
    b, l, _ = xy.shape
    c, n, nv, bb = _seq_dims(b, l, LRU_SEQS_PER_STEP)
    w = LRU_W
    bd = lambda m: jax.scipy.linalg.block_diag(*[m[i] for i in range(LRU_BLOCKS)])
    wgate = jnp.concatenate([bd(wr), bd(wi)], axis=1).astype(BF16)
    bgate = jnp.concatenate([br, bi]).reshape(1, 2 * w)
    out, h_new, buf_new = pl.pallas_call(
        functools.partial(_lru_kernel, bb=bb, c=c, nv=nv, pos0=pos0),
        grid=(b // bb, n),
        in_specs=[_seq_spec(bb, c, 2 * w), _state_spec(bb, (CONV_W - 1, w)), _state_spec(bb, (1, w)),
                  _const_spec((CONV_W, w)), _const_spec((1, w)), _const_spec((w, 2 * w)),
                  _const_spec((1, 2 * w)), _const_spec((1, w))],
        out_specs=[_seq_spec(bb, c, w), _state_spec(bb, (1, w)), _state_spec(bb, (CONV_W - 1, w))],
        out_shape=[jax.ShapeDtypeStruct((b, n * c, w), F32),
                   jax.ShapeDtypeStruct((b, 1, w), F32),
                   jax.ShapeDtypeStruct((b, CONV_W - 1, w), F32)],
        scratch_shapes=[pltpu.VMEM((bb, SUBLANES + c, w), F32), pltpu.VMEM((bb, c, w), F32),
                        pltpu.VMEM((bb, c, w), F32), pltpu.VMEM((bb, 1, w), F32)],
        compiler_params=_cparams("parallel", "arbitrary"),
        name="rglru",
    )(_pad_seq(xy, c, n), buf, h0.reshape(b, 1, w), conv_w, conv_b.reshape(1, w), wgate, bgate,
      lam.reshape(1, w))
    return out[:, :l], h_new.reshape(b, w), buf_new


def _gdn_kernel(qkv_ref, z_ref, gcol_ref, grow_ref, buf_ref, s0_ref, cw_ref, pcol_ref, prow_ref,
                nw_ref, out_ref, snew_ref, bufnew_ref, xs_scr, s_scr, *, bb, c, nv):
    i = pl.program_id(1)
    qk = GDN_QK

    @pl.when(i == 0)
    def _():
        xs_scr[:, 0:SUBLANES, :] = jnp.zeros((bb, SUBLANES, GDN_QKV), F32)
        xs_scr[:, 5:8, :] = buf_ref[...]
        s_scr[...] = s0_ref[...]

    valid = _valid_rows(c, nv)
    valid_r = lax.broadcasted_iota(jnp.int32, (1, c), 1) < nv
    alog_c, dtb_c = pcol_ref[0:1, :], pcol_ref[1:2, :]
    alog_r, dtb_r = prow_ref[:, 0:1], prow_ref[:, 1:2]
    incl, strict = _tri(c, True), _tri(c, False)
    upper = jnp.logical_not(strict)
    heads = []
    for bi in range(bb):
        x = _silu(_conv_window(xs_scr.at[bi], qkv_ref[bi], cw_ref[...], c))
        gcol = gcol_ref[bi]
        grow = grow_ref[bi, 0]
        g_col = jnp.where(valid, -jnp.exp(alog_c) * _softplus(gcol[:, 0:GDN_H] + dtb_c), 0.0)
        beta = jnp.where(valid, _sigmoid(gcol[:, GDN_H:2 * GDN_H]), 0.0)
        g_row = jnp.where(valid_r, -jnp.exp(alog_r) * _softplus(grow[0:GDN_H, :] + dtb_r), 0.0)
        gi_cols = _dot_sel_lhs(incl, g_col)
        gi_rows = _dot_sel_rhs(g_row, upper)
        for h in range(GDN_H):
            q = _l2norm(x[:, h * GDN_DK:(h + 1) * GDN_DK]) * (GDN_DK ** -0.5)
            k = _l2norm(x[:, qk + h * GDN_DK:qk + (h + 1) * GDN_DK])
            gh = g_col[:, h:h + 1]
            gi = gi_cols[:, h:h + 1]
            gx = gi - gh
            gi_row = gi_rows[h:h + 1, :]
            g_end = gi[c - 1:c, :]
            kb = k * beta[:, h:h + 1]
            b = -jnp.exp(gh) * kb
            e_end = jnp.exp(g_end - gi)
            heads.append(dict(
                m_lhs=jnp.concatenate([k, q], axis=0), m_rhs=jnp.concatenate([b, kb], axis=0),
                pair_x=jnp.where(strict, jnp.exp(jnp.where(strict, gx - gi_row, 0.0)), 0.0),
                pair_i=jnp.where(incl, jnp.exp(jnp.where(incl, gi - gi_row, 0.0)), 0.0),
                x_lhs=jnp.concatenate([k * jnp.exp(gx), q * jnp.exp(gi)], axis=0),
                v=x[:, 2 * qk + h * GDN_DV:2 * qk + (h + 1) * GDN_DV],
                bk=jnp.concatenate([b * e_end, kb * e_end], axis=0), s_decay=jnp.exp(g_end)))
    pairs = [(bi, h) for bi in range(bb) for h in range(GDN_H)]
    outs, new_states = _dplr_heads(heads, [s_scr[bi, h] for bi, h in pairs], c, nv, state_is_vk=False)
    for (bi, h), o, s_new in zip(pairs, outs, new_states):
        s_scr[bi, h] = s_new
        zh = z_ref[bi, :, h * GDN_DV:(h + 1) * GDN_DV]
        o = o * lax.rsqrt(jnp.mean(o * o, axis=-1, keepdims=True) + 1e-6) * nw_ref[...]
        out_ref[bi, :, h * GDN_DV:(h + 1) * GDN_DV] = o * _silu(zh)
    tail = xs_scr[:, 5 + nv:8 + nv, :]
    xs_scr[:, 5:8, :] = tail

    @pl.when(i == pl.num_programs(1) - 1)
    def _():
        snew_ref[...] = s_scr[...]
        bufnew_ref[...] = tail


def _gdn_mixer(qkv, z, gates, buf, s0, conv_w, a_log, dt_bias, norm_w):
    b, l, _ = qkv.shape
    c, n, nv, bb = _seq_dims(b, l, GDN_SEQS_PER_STEP)
    gates = _pad_seq(gates, c, n)
    pcol = jnp.stack([a_log, dt_bias])
    out, s_new, buf_new = pl.pallas_call(
        functools.partial(_gdn_kernel, bb=bb, c=c, nv=nv),
        grid=(b // bb, n),
        in_specs=[_seq_spec(bb, c, GDN_QKV), _seq_spec(bb, c, GDN_W), _seq_spec(bb, c, LANES),
                  pl.BlockSpec((bb, 1, SUBLANES, c), lambda b, i: (b, i, 0, 0)),
                  _state_spec(bb, (CONV_W - 1, GDN_QKV)), _state_spec(bb, (GDN_H, GDN_DK, GDN_DV)),
                  _const_spec((CONV_W, GDN_QKV)), _const_spec((2, GDN_H)), _const_spec((GDN_H, 2)),
                  _const_spec((1, GDN_DV))],
        out_specs=[_seq_spec(bb, c, GDN_W), _state_spec(bb, (GDN_H, GDN_DK, GDN_DV)),
                   _state_spec(bb, (CONV_W - 1, GDN_QKV))],
        out_shape=[jax.ShapeDtypeStruct((b, n * c, GDN_W), F32),
                   jax.ShapeDtypeStruct((b, GDN_H, GDN_DK, GDN_DV), F32),
                   jax.ShapeDtypeStruct((b, CONV_W - 1, GDN_QKV), F32)],
        scratch_shapes=[pltpu.VMEM((bb, SUBLANES + c, GDN_QKV), F32),
                        pltpu.VMEM((bb, GDN_H, GDN_DK, GDN_DV), F32)],
        compiler_params=_cparams("parallel", "arbitrary"),
        name="gdn",
    )(_pad_seq(qkv, c, n), _pad_seq(z, c, n), gates, _rows_layout(gates, c, n, SUBLANES), buf, s0,
      conv_w, pcol, pcol.T, norm_w.reshape(1, GDN_DV))
    return out[:, :l], s_new, buf_new


def _rwkv_kernel(rw_ref, prev_ref, s0_ref, mix_ref, w0_ref, w2_ref, a0_ref, a2_ref, g2_ref,
                 kk_ref, ka_ref, rk_ref, lnw_ref, lnb_ref, out_ref, snew_ref, xs_scr, s_scr,
                 *, bb, c, nv):
    i = pl.program_id(1)
    hd = RWKV_HD
    w = RWKV_W

    @pl.when(i == 0)
    def _():
        xs_scr[:, 0:SUBLANES, :] = jnp.zeros((bb, SUBLANES, RWKV_PROJ_W), F32)
        xs_scr[:, 7:8, :] = prev_ref[...]
        s_scr[...] = s0_ref[...]

    valid = _valid_rows(c, nv)
    incl = _tri(c, True)
    heads, post = [], []
    for bi in range(bb):
        rw = rw_ref[bi]
        xs_scr[bi, SUBLANES:SUBLANES + c, :] = rw
        prev = xs_scr[bi, 7:7 + c, :]
        xs_scr[bi, 7:8, :] = xs_scr[bi, 7 + nv:8 + nv, :]
        xs = rw + (prev - rw) * mix_ref[...]
        r_all, k_all, v_all = xs[:, 0:w], xs[:, w:2 * w], xs[:, 2 * w:3 * w]
        o1 = 3 * w
        wl = xs[:, o1:o1 + RWKV_RW]
        al = xs[:, o1 + RWKV_RW:o1 + RWKV_RW + RWKV_RA]
        gl = xs[:, o1 + RWKV_RW + RWKV_RA:]
        wdec = -_softplus(-(w0_ref[...] + _dot(jnp.tanh(wl), w2_ref[...]))) - 0.5
        a_all = _sigmoid(a0_ref[...] + _dot(al, a2_ref[...]))
        gate = _dot(_sigmoid(gl), g2_ref[...])
        lw_all = jnp.where(valid, -jnp.exp(wdec), 0.0)
        kmod = jnp.where(valid, k_all * (1.0 + (a_all - 1.0) * ka_ref[...]), 0.0)
        kk_all = k_all * kk_ref[...]
        gi = _dot_sel_lhs(incl, lw_all)
        gm = gi[c // 2:c // 2 + 1, :]
        g_end = gi[c - 1:c, :]
        e_nlw = jnp.exp(-lw_all)
        e_r = jnp.exp(gi - gm)
        e_a = e_r * e_nlw
        e_m = jnp.exp(gm - gi)
        e_gi = jnp.exp(gi)
        e_gx = e_gi * e_nlw
        e_end = jnp.exp(g_end - gm) * e_m
        s_dec = jnp.exp(g_end)
        for h in range(RWKV_H):
            sl = slice(h * hd, (h + 1) * hd)
            kk = _l2norm(kk_all[:, sl])
            r, k = r_all[:, sl], kmod[:, sl]
            a = -kk
            b = jnp.where(valid, kk * a_all[:, sl], 0.0)
            heads.append(dict(
                m_lhs=jnp.concatenate([a * e_a[:, sl], r * e_r[:, sl]], axis=0),
                m_rhs=jnp.concatenate([b * e_m[:, sl], k * e_m[:, sl]], axis=0),
                x_lhs=jnp.concatenate([a * e_gx[:, sl], r * e_gi[:, sl]], axis=0),
                v=v_all[:, sl],
                bk=jnp.concatenate([b * e_end[:, sl], k * e_end[:, sl]], axis=0), s_decay=s_dec[:, sl]))
            post.append((bi, h, r, k, v_all[:, sl], gate[:, sl]))
    outs, new_states = _dplr_heads(heads, [s_scr[bi, h] for bi, h, *_ in post], c, nv, state_is_vk=True)
    for (bi, h, r, k, v, gate_h), o, s_new in zip(post, outs, new_states):
        sl = slice(h * hd, (h + 1) * hd)
        s_scr[bi, h] = s_new
        y = _ln(o, lnw_ref[:, sl], lnb_ref[:, sl], RWKV_GN_EPS)
        y = y + jnp.sum(r * k * rk_ref[:, sl], axis=-1, keepdims=True) * v
        out_ref[bi, :, sl] = y * gate_h

    @pl.when(i == pl.num_programs(1) - 1)
    def _():
        snew_ref[...] = s_scr[...]


def _rwkv_mixer(rw, shift0, s0, mix, w0, w2, a0, a2, g2, k_k, k_a, r_k, ln_w, ln_b):
    b, l, _ = rw.shape
    c, n, nv, bb = _seq_dims(b, l, RWKV_SEQS_PER_STEP)
    w = RWKV_W
    row = lambda t: t.reshape(1, -1)
    state = (RWKV_H, RWKV_HD, RWKV_HD)
    out, s_new = pl.pallas_call(
        functools.partial(_rwkv_kernel, bb=bb, c=c, nv=nv),
        grid=(b // bb, n),
        in_specs=[_seq_spec(bb, c, RWKV_PROJ_W), _state_spec(bb, (1, RWKV_PROJ_W)), _state_spec(bb, state),
                  _const_spec((1, RWKV_PROJ_W)), _const_spec((1, w)), _const_spec((RWKV_RW, w)),
                  _const_spec((1, w)), _const_spec((RWKV_RA, w)), _const_spec((RWKV_RG, w)),
                  _const_spec((1, w)), _const_spec((1, w)), _const_spec((1, w)), _const_spec((1, w)),
                  _const_spec((1, w))],
        out_specs=[_seq_spec(bb, c, w), _state_spec(bb, state)],
        out_shape=[jax.ShapeDtypeStruct((b, n * c, w), F32),
                   jax.ShapeDtypeStruct((b,) + state, F32)],
        scratch_shapes=[pltpu.VMEM((bb, SUBLANES + c, RWKV_PROJ_W), F32), pltpu.VMEM((bb,) + state, F32)],
        compiler_params=_cparams("parallel", "arbitrary"),
        name="rwkv7",
    )(_pad_seq(rw, c, n), shift0.reshape(b, 1, RWKV_PROJ_W), s0, row(mix), row(w0),
      w2.astype(BF16), row(a0), a2.astype(BF16), g2.astype(BF16), row(k_k), row(k_a), row(r_k),
      row(ln_w), row(ln_b))
    return out[:, :l], s_new


def _mlstm_kernel(p_ref, gcol_ref, grow_ref, c0_ref, n0_ref, m0_ref, bcol_ref, brow_ref, nw_ref,
                  out_ref, cnew_ref, nnew_ref, mnew_ref, c_scr, n_scr, m_scr, *, bb, c, nv):
    i = pl.program_id(1)
    nh, dk, dv = MLSTM_H, MLSTM_DK, MLSTM_DV

    @pl.when(i == 0)
    def _():
        c_scr[...] = c0_ref[...]
        n_scr[...] = n0_ref[...]
        m_scr[...] = m0_ref[...]

    valid = _valid_rows(c, nv)
    valid_r = lax.broadcasted_iota(jnp.int32, (1, c), 1) < nv
    incl = _tri(c, True)
    upper = jnp.logical_not(_tri(c, False))
    ps = [(bi, h) for bi in range(bb) for h in range(nh)]
    gate_cols = {}
    for bi in range(bb):
        gcol = gcol_ref[bi]
        grow = grow_ref[bi, 0]
        li_col = jnp.where(valid, gcol[:, 0:nh] + bcol_ref[0:1, :], NEG_BIG)
        lf_col = jnp.where(valid, -_softplus(-(gcol[:, nh:2 * nh] + bcol_ref[1:2, :])), 0.0)
        li_row = jnp.where(valid_r, grow[0:nh, :] + brow_ref[:, 0:1], NEG_BIG)
        lf_row = jnp.where(valid_r, -_softplus(-(grow[nh:2 * nh, :] + brow_ref[:, 1:2])), 0.0)
        b_cols = _dot_sel_lhs(incl, lf_col)
        b_rows = _dot_sel_rhs(lf_row, upper)
        gate_cols[bi] = (li_col, li_row, b_cols, b_rows)
    qs = [p_ref[bi, :, h * dk:(h + 1) * dk] for bi, h in ps]
    ks = [p_ref[bi, :, nh * dk + h * dk:nh * dk + (h + 1) * dk] * (dk ** -0.5) for bi, h in ps]
    vs = [p_ref[bi, :, 2 * nh * dk + h * dv:2 * nh * dk + (h + 1) * dv] for bi, h in ps]
    cms = [c_scr[bi, h] for bi, h in ps]
    nvecs = [n_scr[bi, h] for bi, h in ps]
    m_prevs = [m_scr[bi, h] for bi, h in ps]
    qks = [_dot_nt(q, k) for q, k in zip(qs, ks)]
    qcs = [_dot_nt(q, cm) for q, cm in zip(qs, cms)]
    bcs = [gate_cols[bi][2][:, h:h + 1] for bi, h in ps]
    dms = [jnp.where(incl, bc - gate_cols[bi][3][h:h + 1, :] + gate_cols[bi][1][h:h + 1, :], -jnp.inf)
           for (bi, h), bc in zip(ps, bcs)]
    m_inters = [bc + m_prev for bc, m_prev in zip(bcs, m_prevs)]
    m_ts = [jnp.maximum(mi, jnp.max(dm, axis=-1, keepdims=True)) for mi, dm in zip(m_inters, dms)]
    w_inters = [jnp.exp(mi - mt) for mi, mt in zip(m_inters, m_ts)]
    scs = [qk * jnp.exp(dm - mt) for qk, dm, mt in zip(qks, dms, m_ts)]
    scvs = [_dot(sc, v) for sc, v in zip(scs, vs)]
    m_news = [mt[c - 1:c, :] for mt in m_ts]
    b_lasts = [bc[c - 1:c, :] for bc in bcs]
    wss = [jnp.exp(b_last - bc + gate_cols[bi][0][:, h:h + 1] - m_new)
           for (bi, h), b_last, bc, m_new in zip(ps, b_lasts, bcs, m_news)]
    upds = [_dot_tn(v * ws, k) for v, ws, k in zip(vs, wss, ks)]
    for j, (bi, h) in enumerate(ps):
        num = w_inters[j] * qcs[j] + scvs[j]
        den = (w_inters[j] * jnp.sum(qs[j] * nvecs[j], axis=-1, keepdims=True)
               + jnp.sum(scs[j], axis=-1, keepdims=True))
        hh = num / jnp.maximum(jnp.abs(den), jnp.exp(-m_ts[j]))
        dec = jnp.exp(b_lasts[j] + m_prevs[j] - m_news[j])
        c_scr[bi, h] = dec * cms[j] + upds[j]
        n_scr[bi, h] = dec * nvecs[j] + jnp.sum(ks[j] * wss[j], axis=0, keepdims=True)
        m_scr[bi, h] = m_news[j]
        og = p_ref[bi, :, 2 * nh * dk + nh * dv + h * dv:2 * nh * dk + nh * dv + (h + 1) * dv]
        out_ref[bi, :, h * dv:(h + 1) * dv] = _ln(hh, nw_ref[...]) * _sigmoid(og)

    @pl.when(i == pl.num_programs(1) - 1)
    def _():
        cnew_ref[...] = c_scr[...]
        nnew_ref[...] = n_scr[...]
        mnew_ref[...] = m_scr[...]


def _mlstm_mixer(p, gates, c0, n0, m0, i_b, f_b, norm_w):
    b, l, _ = p.shape
    c, n, nv, bb = _seq_dims(b, l, MLSTM_SEQS_PER_STEP)
    nh, dk, dv = MLSTM_H, MLSTM_DK, MLSTM_DV
    gates = _pad_seq(gates, c, n)
    bcol = jnp.stack([i_b, f_b])
    out, c_new, n_new, m_new = pl.pallas_call(
        functools.partial(_mlstm_kernel, bb=bb, c=c, nv=nv),
        grid=(b // bb, n),
        in_specs=[_seq_spec(bb, c, p.shape[2]), _seq_spec(bb, c, LANES),
                  pl.BlockSpec((bb, 1, SUBLANES, c), lambda b, i: (b, i, 0, 0)),
                  _state_spec(bb, (nh, dv, dk)), _state_spec(bb, (nh, 1, dk)), _state_spec(bb, (nh, 1, 1)),
                  _const_spec((2, nh)), _const_spec((nh, 2)), _const_spec((1, dv))],
        out_specs=[_seq_spec(bb, c, nh * dv), _state_spec(bb, (nh, dv, dk)),
                   _state_spec(bb, (nh, 1, dk)), _state_spec(bb, (nh, 1, 1))],
        out_shape=[jax.ShapeDtypeStruct((b, n * c, nh * dv), F32),
                   jax.ShapeDtypeStruct((b, nh, dv, dk), F32),
                   jax.ShapeDtypeStruct((b, nh, 1, dk), F32),
                   jax.ShapeDtypeStruct((b, nh, 1, 1), F32)],
        scratch_shapes=[pltpu.VMEM((bb, nh, dv, dk), F32), pltpu.VMEM((bb, nh, 1, dk), F32),
                        pltpu.VMEM((bb, nh, 1, 1), F32)],
        compiler_params=_cparams("parallel", "arbitrary"),
        name="mlstm",
    )(_pad_seq(p, c, n), gates, _rows_layout(gates, c, n, SUBLANES), c0,
      n0.reshape(b, nh, 1, dk), m0.reshape(b, nh, 1, 1), bcol, bcol.T, norm_w.reshape(1, dv))
    return out[:, :l], c_new, n_new.reshape(b, nh, dk), m_new.reshape(b, nh)


def _pad_cols(w, n):
    return jnp.pad(w, ((0, 0), (0, n - w.shape[1])))


def _ab_in_weight(w):
    o = 2 * LRU_W
    xy, qkv = w[:, :o], w[:, o:o + GDN_QKV]
    o += GDN_QKV
    ab, z = w[:, o:o + 2 * GDN_H], w[:, o + 2 * GDN_H:]
    return jnp.concatenate([xy, qkv, z, _pad_cols(ab, LANES)], axis=1).astype(BF16)


AB_SPLITS = ((0, 2 * LRU_W), (2 * LRU_W, GDN_QKV), (2 * LRU_W + GDN_QKV, GDN_W),
             (2 * LRU_W + GDN_QKV + GDN_W, LANES))


def _cd_in_weight(w):
    o = RWKV_PROJ_W
    rw, qkv = w[:, :o], w[:, o:o + 2 * MLSTM_QK + MLSTM_W]
    o += 2 * MLSTM_QK + MLSTM_W
    gates, og = w[:, o:o + 2 * MLSTM_H], w[:, o + 2 * MLSTM_H:]
    return jnp.concatenate([rw, qkv, og, _pad_cols(gates, LANES)], axis=1).astype(BF16)


CD_SPLITS = ((0, RWKV_PROJ_W), (RWKV_PROJ_W, 2 * MLSTM_QK + 2 * MLSTM_W),
             (RWKV_PROJ_W + 2 * MLSTM_QK + 2 * MLSTM_W, LANES))


def kernel(x_prompt, x_sample, c_prompt, c_sample,
           state_lru_h, state_lru_conv, state_gdn_S, state_gdn_conv,
           state_rwkv_S, state_rwkv_shift, state_mlstm_C, state_mlstm_n, state_mlstm_m,
           mod_w, mod_b, ln1_g, ln1_b, ln2_g, ln2_b,
           ab_w_in, ab_w_out, lru_conv_w, lru_conv_b, lru_wr, lru_br, lru_wi, lru_bi, lru_lambda,
           gdn_conv_w, gdn_a_log, gdn_dt_bias, gdn_norm_w,
           cd_w_in, cd_w_out, rwkv_mix, rwkv_w0, rwkv_w2, rwkv_a0, rwkv_a2, rwkv_g2,
           rwkv_k_k, rwkv_k_a, rwkv_r_k, rwkv_ln_w, rwkv_ln_b,
           mlstm_i_b, mlstm_f_b, mlstm_norm_w,
           ffn_w_gate, ffn_w_up, ffn_w_down,
           router_w, moe_w_gate, moe_w_up, moe_w_down):
    d = D_MODEL
    bp, lp, _ = x_prompt.shape
    bs, ls, _ = x_sample.shape
    mod = _modulation(jnp.concatenate([c_prompt, c_sample], axis=0), mod_w, mod_b)

    ab_in = [_ab_in_weight(ab_w_in[j]) for j in range(ab_w_in.shape[0])]
    cd_in = [_cd_in_weight(cd_w_in[j]) for j in range(cd_w_in.shape[0])]
    ab_out, cd_out = ab_w_out.astype(BF16), cd_w_out.astype(BF16)
    ffn_g, ffn_u, ffn_d = (t.astype(BF16) for t in (ffn_w_gate, ffn_w_up, ffn_w_down))
    moe_g, moe_u, moe_d = (t.astype(BF16) for t in (moe_w_gate, moe_w_up, moe_w_down))

    def trunk(x, mods, batch, length, states, pos0):
        lru_h, lru_conv, gdn_s, gdn_conv, rwkv_s, rwkv_shift, m_c, m_n, m_m = states
        new = [[] for _ in range(9)]
        seq = lambda t: t.reshape(batch, length, t.shape[-1])
        tok = lambda t: t.reshape(x.shape[0], x.shape[1], t.shape[-1])
        for l in range(DEPTH):
            j = l // 2
            sh1, sc1, g1, sh2, sc2, g2 = mods[l]
            if l % 2 == 0:
                xy, qkv, z, gates = _inproj(x, sc1, sh1, ab_in[j], AB_SPLITS)
                out_a, s0, s1 = _lru_mixer(seq(xy), lru_conv[j], lru_h[j], lru_conv_w[j], lru_conv_b[j],
                                           lru_wr[j], lru_br[j], lru_wi[j], lru_bi[j], lru_lambda[j], pos0)
                out_b, s2, s3 = _gdn_mixer(seq(qkv), seq(z), seq(gates), gdn_conv[j], gdn_s[j],
                                           gdn_conv_w[j], gdn_a_log[j], gdn_dt_bias[j], gdn_norm_w[j])
                for slot, s in zip((0, 1, 2, 3), (s0, s1, s2, s3)):
                    new[slot].append(s)
                x = _outproj_ln(x, tok(out_a), tok(out_b), g1, ab_out, j, ln1_g[l], ln1_b[l])
                x = _ffn_ln(x, sc2, sh2, g2, ffn_g, ffn_u, ffn_d, j, ln2_g[l], ln2_b[l])
            else:
                rw, mp, gates = _inproj(x, sc1, sh1, cd_in[j], CD_SPLITS)
                out_c, s0 = _rwkv_mixer(seq(rw), rwkv_shift[j], rwkv_s[j], rwkv_mix[j], rwkv_w0[j],
                                        rwkv_w2[j], rwkv_a0[j], rwkv_a2[j], rwkv_g2[j], rwkv_k_k[j],
                                        rwkv_k_a[j], rwkv_r_k[j], rwkv_ln_w[j], rwkv_ln_b[j])
                out_d, s2, s3, s4 = _mlstm_mixer(seq(mp), seq(gates), m_c[j], m_n[j], m_m[j],
                                                 mlstm_i_b[j], mlstm_f_b[j], mlstm_norm_w[j])
                for slot, s in zip((4, 5, 6, 7, 8), (s0, seq(rw)[:, -1], s2, s3, s4)):
                    new[slot].append(s)
                x = _outproj_ln(x, tok(out_c), tok(out_d), g1, cd_out, j, ln1_g[l], ln1_b[l])
                moe = _moe_sparse_ln if x.shape[1] % MOE_TOKEN_BLOCK == 0 else _moe_ln
                x = moe(x, sc2, sh2, g2, router_w[j], moe_g, moe_u, moe_d, j, ln2_g[l], ln2_b[l])
        return x, tuple(jnp.stack(s) for s in new)

    def zeros(ref):
        return jnp.zeros((ref.shape[0], bp) + ref.shape[2:], F32)

    mods_p = [[mod[l, :bp, k * d:(k + 1) * d].reshape(bp, 1, d) for k in range(6)] for l in range(DEPTH)]
    mods_s = [[mod[l, bp:, k * d:(k + 1) * d].reshape(1, bs * ls, d) for k in range(6)] for l in range(DEPTH)]
    states_s = (state_lru_h, state_lru_conv, state_gdn_S, state_gdn_conv, state_rwkv_S,
                state_rwkv_shift, state_mlstm_C, state_mlstm_n, state_mlstm_m)
    y_p, new_p = trunk(x_prompt, mods_p, bp, lp, tuple(zeros(s) for s in states_s), 0)
    y_s, new_s = trunk(x_sample.reshape(1, bs * ls, d), mods_s, bs, ls, states_s, PAST_LEN)
    out = [y_p, y_s.reshape(bs, ls, d)]
    for p_leaf, s_leaf in zip(new_p, new_s):
        out += [p_leaf, s_leaf]
    return tuple(out)
```

```python
import functools
import math

import jax
import jax.numpy as jnp
from jax import lax
from jax.experimental import pallas as pl
from jax.experimental.pallas import tpu as pltpu

F32 = jnp.float32
BF16 = jnp.bfloat16

D_MODEL = 1024
DEPTH = 4
PAST_LEN = 16384
CONV_W = 4
LRU_W = D_MODEL // 2
LRU_BLOCKS = 8
LRU_BW = LRU_W // LRU_BLOCKS
LRU_C = 8.0
GDN_H = D_MODEL // 256
GDN_DK = 128
GDN_DV = 128
GDN_QK = GDN_H * GDN_DK
GDN_W = GDN_H * GDN_DV
GDN_QKV = 2 * GDN_QK + GDN_W
RWKV_HD = 64
RWKV_H = D_MODEL // 2 // RWKV_HD
RWKV_W = RWKV_H * RWKV_HD
RWKV_RW = 64
RWKV_RA = 64
RWKV_RG = 128
RWKV_PROJ_W = 3 * RWKV_W + RWKV_RW + RWKV_RA + RWKV_RG
RWKV_GN_EPS = 64e-5
MLSTM_H = D_MODEL // 256
MLSTM_DK = 128
MLSTM_DV = 128
MLSTM_QK = MLSTM_H * MLSTM_DK
MLSTM_W = MLSTM_H * MLSTM_DV
D_FF = 7 * D_MODEL // 2
N_EXPERTS = 8
LN_EPS = 1e-5
NEG_BIG = -1e30
DN_ALPHA = (2.0 * DEPTH) ** 0.25

LANES = 128
SUBLANES = 8
SEQ_CHUNK = 64
SHORT_SEQ_BLOCK = 8
LRU_SEQS_PER_STEP = 8
GDN_SEQS_PER_STEP = 4
RWKV_SEQS_PER_STEP = 2
MLSTM_SEQS_PER_STEP = 1
ROW_TILE = 512
FFN_ROW_TILE = 1024
FF_TILE = 512
MOE_TOKEN_BLOCK = 512
MOE_SLOT_CHUNK = 256
MOE_FF_SPLIT = 4
MOE_WINDOW = 256
MOE_WINDOWS = 3
VMEM_LIMIT = 48 * 1024 * 1024


def _cparams(*sem):
    return pltpu.CompilerParams(dimension_semantics=sem, vmem_limit_bytes=VMEM_LIMIT)


def _dot(a, b):
    return jnp.dot(a.astype(BF16), b.astype(BF16), preferred_element_type=F32)


def _dot_nt(a, b):
    return lax.dot_general(a.astype(BF16), b.astype(BF16), (((1,), (1,)), ((), ())),
                           preferred_element_type=F32)


def _dot_tn(a, b):
    return lax.dot_general(a.astype(BF16), b.astype(BF16), (((0,), (0,)), ((), ())),
                           preferred_element_type=F32)


def _split3(x):
    hi = x.astype(BF16)
    r1 = x - hi.astype(F32)
    mid = r1.astype(BF16)
    lo = (r1 - mid.astype(F32)).astype(BF16)
    return hi, mid, lo


def _dot_sel_lhs(t, x):
    tb = jnp.where(t, 1.0, 0.0).astype(BF16)
    hi, mid, lo = _split3(x)
    d = lambda p: jnp.dot(tb, p, preferred_element_type=F32)
    return d(hi) + d(mid) + d(lo)


def _dot_sel_rhs(x, t):
    tb = jnp.where(t, 1.0, 0.0).astype(BF16)
    hi, mid, lo = _split3(x)
    d = lambda p: jnp.dot(p, tb, preferred_element_type=F32)
    return d(hi) + d(mid) + d(lo)


def _dot2(p, x):
    pb = p.astype(BF16)
    xh = x.astype(BF16)
    xl = (x - xh.astype(F32)).astype(BF16)
    return (jnp.dot(pb, xh, preferred_element_type=F32)
            + jnp.dot(pb, xl, preferred_element_type=F32))


def _sigmoid(x):
    return 1.0 / (1.0 + jnp.exp(-x))


def _silu(x):
    return x * _sigmoid(x)


def _softplus(x):
    return jnp.maximum(x, 0.0) + jnp.log1p(jnp.exp(-jnp.abs(x)))


def _gelu_tanh(x):
    return 0.5 * x * (1.0 + jnp.tanh(math.sqrt(2.0 / math.pi) * (x + 0.044715 * (x * x * x))))


def _ln(y, g=None, b=None, eps=LN_EPS):
    mu = jnp.mean(y, axis=-1, keepdims=True)
    d = y - mu
    var = jnp.mean(d * d, axis=-1, keepdims=True)
    out = d * lax.rsqrt(var + eps)
    if g is not None:
        out = out * g
    if b is not None:
        out = out + b
    return out


def _l2norm(x, eps=1e-6):
    return x * lax.rsqrt(jnp.sum(x * x, axis=-1, keepdims=True) + eps)


def _tri(c, inclusive):
    t = lax.broadcasted_iota(jnp.int32, (c, c), 0)
    s = lax.broadcasted_iota(jnp.int32, (c, c), 1)
    return (s <= t) if inclusive else (s < t)


def _unit_lower_solve(ns, xs, c):
    steps = max(1, int(math.ceil(math.log2(c))))
    for i in range(steps):
        xs = [x + _dot2(p, x) for p, x in zip(ns, xs)]
        if i + 1 < steps:
            ns = [_dot(p, p) for p in ns]
    return xs


def _dplr_heads(heads, states, c, nv, state_is_vk):
    incl = _tri(c, True)
    strict = _tri(c, False)
    ms = [_dot_nt(h["m_lhs"], h["m_rhs"]) for h in heads]
    if state_is_vk:
        xhs = [_dot_nt(h["x_lhs"], s) for h, s in zip(heads, states)]
    else:
        xhs = [_dot(h["x_lhs"], s) for h, s in zip(heads, states)]
    a_abs, a_aks, r_bks = [], [], []
    for h, m in zip(heads, ms):
        if "pair_x" in h:
            a_abs.append(m[:c, :c] * h["pair_x"])
            a_aks.append(m[:c, c:] * h["pair_x"])
            r_bks.append(jnp.concatenate([m[c:, :c] * h["pair_i"], m[c:, c:] * h["pair_i"]], axis=1))
        else:
            a_abs.append(jnp.where(strict, m[:c, :c], 0.0))
            a_aks.append(jnp.where(strict, m[:c, c:], 0.0))
            r_bks.append(jnp.concatenate([jnp.where(incl, m[c:, :c], 0.0),
                                          jnp.where(incl, m[c:, c:], 0.0)], axis=1))
    if nv == 1:
        us = [xh[:c] for xh in xhs]
    else:
        rhs = [xh[:c] + _dot(a_ak, h["v"]) for xh, a_ak, h in zip(xhs, a_aks, heads)]
        us = _unit_lower_solve(a_abs, rhs, c)
    uvs = [jnp.concatenate([u, h["v"]], axis=0) for u, h in zip(us, heads)]
    outs = [xh[c:] + _dot(r_bk, uv) for xh, r_bk, uv in zip(xhs, r_bks, uvs)]
    if state_is_vk:
        new = [s * h["s_decay"] + _dot_tn(uv, h["bk"]) for s, h, uv in zip(states, heads, uvs)]
    else:
        new = [s * h["s_decay"] + _dot_tn(h["bk"], uv) for s, h, uv in zip(states, heads, uvs)]
    return outs, new


def _valid_rows(c, n_valid):
    return lax.broadcasted_iota(jnp.int32, (c, 1), 0) < n_valid


def _mod_kernel(c_ref, w_ref, b_ref, o_ref):
    o_ref[0] = _dot(_silu(c_ref[...]), w_ref[0]) + b_ref[0]


def _modulation(c_all, mod_w, mod_b):
    n = c_all.shape[0]
    d = D_MODEL
    return pl.pallas_call(
        _mod_kernel,
        grid=(DEPTH, 6),
        in_specs=[pl.BlockSpec((n, d), lambda l, j: (0, 0)),
                  pl.BlockSpec((1, d, d), lambda l, j: (l, 0, j)),
                  pl.BlockSpec((1, 1, d), lambda l, j: (l, 0, j))],
        out_specs=pl.BlockSpec((1, n, d), lambda l, j: (l, 0, j)),
        out_shape=jax.ShapeDtypeStruct((DEPTH, n, 6 * d), F32),
        compiler_params=_cparams("parallel", "parallel"),
        name="modulation",
    )(c_all, mod_w, mod_b.reshape(DEPTH, 1, 6 * d))


def _mod_spec(mod, tm):
    if mod.shape[1] == 1:
        return pl.BlockSpec((1, 1, mod.shape[2]), lambda g, i, *_: (g, 0, 0))
    return pl.BlockSpec((1, tm, mod.shape[2]), lambda g, i, *_: (g, i, 0))


def _inproj_kernel(x_ref, sc_ref, sh_ref, w_ref, *o_refs, splits):
    h = (x_ref[0] * (1.0 + sc_ref[0]) + sh_ref[0]).astype(BF16)
    for o_ref, (s, n) in zip(o_refs, splits):
        o_ref[0] = jnp.dot(h, w_ref[:, s:s + n], preferred_element_type=F32)


def _inproj(x, sc, sh, w, splits):
    g, r, d = x.shape
    tm = min(r, ROW_TILE)
    n_all = w.shape[1]
    return pl.pallas_call(
        functools.partial(_inproj_kernel, splits=splits),
        grid=(g, r // tm),
        in_specs=[pl.BlockSpec((1, tm, d), lambda g, i: (g, i, 0)),
                  _mod_spec(sc, tm), _mod_spec(sh, tm),
                  pl.BlockSpec((d, n_all), lambda g, i: (0, 0))],
        out_specs=[pl.BlockSpec((1, tm, n), lambda g, i: (g, i, 0)) for _, n in splits],
        out_shape=[jax.ShapeDtypeStruct((g, r, n), F32) for _, n in splits],
        compiler_params=_cparams("parallel", "parallel"),
        name="inproj",
    )(x, sc, sh, w)


def _outproj_ln_kernel(x_ref, ma_ref, mb_ref, gate_ref, w_ref, lng_ref, lnb_ref, o_ref):
    half = ma_ref.shape[2]
    f = _dot(ma_ref[0], w_ref[0:half, :]) + _dot(mb_ref[0], w_ref[half:, :])
    y = DN_ALPHA * x_ref[0] + (1.0 + gate_ref[0]) * f
    o_ref[0] = _ln(y, lng_ref[...], lnb_ref[...])


def _outproj_ln(x, mix_a, mix_b, gate, w, layer, ln_g, ln_b):
    g, r, d = x.shape
    tm = min(r, ROW_TILE)
    half = mix_a.shape[2]
    row = pl.BlockSpec((1, tm, d), lambda g, i: (g, i, 0))
    mrow = pl.BlockSpec((1, tm, half), lambda g, i: (g, i, 0))
    vec = pl.BlockSpec((1, d), lambda g, i: (0, 0))
    return pl.pallas_call(
        _outproj_ln_kernel,
        grid=(g, r // tm),
        in_specs=[row, mrow, mrow, _mod_spec(gate, tm),
                  pl.BlockSpec((None, 2 * half, d), lambda g, i: (layer, 0, 0)), vec, vec],
        out_specs=row,
        out_shape=jax.ShapeDtypeStruct((g, r, d), F32),
        compiler_params=_cparams("parallel", "parallel"),
        name="outproj_ln",
    )(x, mix_a, mix_b, gate, w, ln_g.reshape(1, d), ln_b.reshape(1, d))


def _ffn_kernel(x_ref, sc_ref, sh_ref, gate_ref, wg_ref, wu_ref, wd_ref, lng_ref, lnb_ref,
                o_ref, h_scr, acc_scr):
    j = pl.program_id(2)

    @pl.when(j == 0)
    def _():
        h_scr[...] = (x_ref[0] * (1.0 + sc_ref[0]) + sh_ref[0]).astype(BF16)
        acc_scr[...] = jnp.zeros_like(acc_scr)

    h = h_scr[...]
    a = jnp.dot(h, wg_ref[...], preferred_element_type=F32)
    u = jnp.dot(h, wu_ref[...], preferred_element_type=F32)
    acc_scr[...] += _dot(_silu(a) * u, wd_ref[...])

    @pl.when(j == pl.num_programs(2) - 1)
    def _():
        y = DN_ALPHA * x_ref[0] + (1.0 + gate_ref[0]) * acc_scr[...]
        o_ref[0] = _ln(y, lng_ref[...], lnb_ref[...])


def _ffn_ln(x, sc, sh, gate, wg, wu, wd, layer, ln_g, ln_b):
    g, r, d = x.shape
    tm = min(r, FFN_ROW_TILE)
    f = wg.shape[2]
    row = pl.BlockSpec((1, tm, d), lambda g, i, j: (g, i, 0))
    vec = pl.BlockSpec((1, d), lambda g, i, j: (0, 0))
    return pl.pallas_call(
        _ffn_kernel,
        grid=(g, r // tm, f // FF_TILE),
        in_specs=[row, _mod_spec(sc, tm), _mod_spec(sh, tm), _mod_spec(gate, tm),
                  pl.BlockSpec((None, d, FF_TILE), lambda g, i, j: (layer, 0, j)),
                  pl.BlockSpec((None, d, FF_TILE), lambda g, i, j: (layer, 0, j)),
                  pl.BlockSpec((None, FF_TILE, d), lambda g, i, j: (layer, j, 0)), vec, vec],
        out_specs=row,
        out_shape=jax.ShapeDtypeStruct((g, r, d), F32),
        scratch_shapes=[pltpu.VMEM((tm, d), BF16), pltpu.VMEM((tm, d), F32)],
        compiler_params=_cparams("parallel", "parallel", "arbitrary"),
        name="ffn_ln",
    )(x, sc, sh, gate, wg, wu, wd, ln_g.reshape(1, d), ln_b.reshape(1, d))


def _top2_route(h, rw, lane):
    logits = jnp.dot(h, rw, preferred_element_type=F32, precision=lax.Precision.HIGHEST)
    logits = jnp.where(lane < N_EXPERTS, logits, -jnp.inf)
    m1 = jnp.max(logits, axis=-1, keepdims=True)
    i1 = jnp.min(jnp.where(logits == m1, lane, LANES), axis=-1, keepdims=True)
    rest = jnp.where(lane == i1, -jnp.inf, logits)
    m2 = jnp.max(rest, axis=-1, keepdims=True)
    i2 = jnp.min(jnp.where(rest == m2, lane, LANES), axis=-1, keepdims=True)
    e2 = jnp.exp(m2 - m1)
    g1 = 1.0 / (1.0 + e2)
    g2 = e2 / (1.0 + e2)
    sel = (lane == i1) | (lane == i2)
    return sel, jnp.where(lane == i1, g1, 0.0) + jnp.where(lane == i2, g2, 0.0)


def _moe_kernel(x_ref, sc_ref, sh_ref, gate_ref, rw_ref, wg_ref, wu_ref, wd_ref, lng_ref, lnb_ref,
                o_ref, h_scr, comb_scr, acc_scr):
    e = pl.program_id(2)
    j = pl.program_id(3)
    lane = lax.broadcasted_iota(jnp.int32, comb_scr.shape, 1)

    @pl.when((e == 0) & (j == 0))
    def _():
        h = x_ref[0] * (1.0 + sc_ref[0]) + sh_ref[0]
        h_scr[...] = h.astype(BF16)
        acc_scr[...] = jnp.zeros_like(acc_scr)
        _, comb_scr[...] = _top2_route(h, rw_ref[...], lane)

    h = h_scr[...]
    a = _dot(h, wg_ref[0])
    u = _dot(h, wu_ref[0])
    comb_e = jnp.sum(jnp.where(lane == e, comb_scr[...], 0.0), axis=-1, keepdims=True)
    acc_scr[...] += comb_e * _dot(_silu(a) * u, wd_ref[0])

    @pl.when((e == pl.num_programs(2) - 1) & (j == pl.num_programs(3) - 1))
    def _():
        y = DN_ALPHA * x_ref[0] + (1.0 + gate_ref[0]) * acc_scr[...]
        o_ref[0] = _ln(y, lng_ref[...], lnb_ref[...])


def _moe_ln(x, sc, sh, gate, router_w, wg, wu, wd, layer, ln_g, ln_b):
    g, r, d = x.shape
    tm = min(r, ROW_TILE)
    _, ne, _, f = wg.shape
    row = pl.BlockSpec((1, tm, d), lambda g, i, e, j: (g, i, 0))
    vec = pl.BlockSpec((1, d), lambda g, i, e, j: (0, 0))
    rw = jnp.pad(router_w, ((0, 0), (0, LANES - ne)))
    return pl.pallas_call(
        _moe_kernel,
        grid=(g, r // tm, ne, f // FF_TILE),
        in_specs=[row, _mod_spec(sc, tm), _mod_spec(sh, tm), _mod_spec(gate, tm),
                  pl.BlockSpec((d, LANES), lambda g, i, e, j: (0, 0)),
                  pl.BlockSpec((None, 1, d, FF_TILE), lambda g, i, e, j: (layer, e, 0, j)),
                  pl.BlockSpec((None, 1, d, FF_TILE), lambda g, i, e, j: (layer, e, 0, j)),
                  pl.BlockSpec((None, 1, FF_TILE, d), lambda g, i, e, j: (layer, e, j, 0)), vec, vec],
        out_specs=row,
        out_shape=jax.ShapeDtypeStruct((g, r, d), F32),
        scratch_shapes=[pltpu.VMEM((tm, d), BF16), pltpu.VMEM((tm, LANES), F32),
                        pltpu.VMEM((tm, d), F32)],
        compiler_params=_cparams("parallel", "parallel", "arbitrary", "arbitrary"),
        name="moe_ln",
    )(x, sc, sh, gate, rw, wg, wu, wd, ln_g.reshape(1, d), ln_b.reshape(1, d))


def _moe_route_kernel(x_ref, sc_ref, sh_ref, rw_ref, hb_ref, comb_ref, rank_ref, rankt_ref, cnt_ref,
                      run_scr):
    @pl.when(pl.program_id(0) == 0)
    def _():
        run_scr[...] = jnp.zeros_like(run_scr)

    h = x_ref[0] * (1.0 + sc_ref[0]) + sh_ref[0]
    hb_ref[0] = h.astype(BF16)
    tb = h.shape[0]
    lane = lax.broadcasted_iota(jnp.int32, (tb, LANES), 1)
    sel, comb = _top2_route(h, rw_ref[...], lane)
    comb_ref[0] = comb
    ones = jnp.where(sel, 1.0, 0.0)
    before = jnp.dot(jnp.where(_tri(tb, False), 1.0, 0.0).astype(BF16), ones.astype(BF16),
                     preferred_element_type=F32)
    rank = jnp.where(sel, before + run_scr[...], -1.0)
    rank_ref[0] = rank
    rankt_ref[0] = rank.T[0:SUBLANES, :]
    cnt = jnp.sum(ones, axis=0, keepdims=True)
    cnt_ref[0] = cnt
    run_scr[...] += cnt


def _moe_route(xb, sc, sh, router_w, blocks_per_seq):
    nb, tb, d = xb.shape
    seq = lambda j: (j // blocks_per_seq, 0, 0)
    blk = lambda width: pl.BlockSpec((1, tb, width), lambda j: (j, 0, 0))
    return pl.pallas_call(
        _moe_route_kernel,
        grid=(nb,),
        in_specs=[blk(d), pl.BlockSpec((1, 1, d), seq), pl.BlockSpec((1, 1, d), seq),
                  pl.BlockSpec((d, LANES), lambda j: (0, 0))],
        out_specs=[blk(d), blk(LANES), blk(LANES),
                   pl.BlockSpec((1, SUBLANES, tb), lambda j: (j, 0, 0)),
                   pl.BlockSpec((1, 1, LANES), lambda j: (j, 0, 0))],
        out_shape=[jax.ShapeDtypeStruct((nb, tb, d), BF16),
                   jax.ShapeDtypeStruct((nb, tb, LANES), F32),
                   jax.ShapeDtypeStruct((nb, tb, LANES), F32),
                   jax.ShapeDtypeStruct((nb, SUBLANES, tb), F32),
                   jax.ShapeDtypeStruct((nb, 1, LANES), F32)],
        scratch_shapes=[pltpu.VMEM((1, LANES), F32)],
        compiler_params=_cparams("arbitrary"),
        name="moe_route",
    )(xb, sc, sh, jnp.pad(router_w, ((0, 0), (0, LANES - router_w.shape[1]))))


def _moe_tables(cnt, n_chunks, n_items):
    nb, ne = cnt.shape
    c = MOE_SLOT_CHUNK
    off = jnp.cumsum(cnt, axis=0) - cnt
    total = jnp.sum(cnt, axis=0)
    nch = (total + c - 1) // c
    ends = jnp.cumsum(nch)
    k = jnp.arange(n_chunks, dtype=jnp.int32)
    ce = jnp.minimum(jnp.searchsorted(ends, k, side="right"), ne - 1).astype(jnp.int32)
    cvalid = k < ends[-1]
    r0 = (k - (ends - nch)[ce]) * c
    lo = jnp.maximum(r0[:, None], off.T[ce])
    hi = jnp.minimum(r0[:, None] + c, (off + cnt).T[ce])
    overlap = cvalid[:, None] & (lo < hi)
    n_pairs = jnp.sum(overlap)
    pos = jnp.arange(n_items, dtype=jnp.int32)

    flat = jnp.nonzero(overlap.reshape(-1), size=n_items, fill_value=0)[0].astype(jnp.int32)
    flat = jnp.where(pos < n_pairs, flat, flat[jnp.maximum(n_pairs - 1, 0)])
    chunk, block = flat // nb, flat % nb
    valid = pos < n_pairs
    first = valid & ((pos == 0) | (chunk != jnp.roll(chunk, 1)))
    last = valid & ((pos == n_pairs - 1) | (chunk != jnp.roll(chunk, -1)))
    spare_chunk = ends[-1] + (pos - n_pairs)
    fill = (~valid) & (spare_chunk < n_chunks)
    chunk = jnp.where(valid, chunk, jnp.minimum(spare_chunk, n_chunks - 1)).astype(jnp.int32)
    i32 = lambda t: t.astype(jnp.int32)
    by_chunk = dict(chunk=chunk, block=block, expert=ce[chunk], r0=r0[chunk], first=i32(first | fill),
                    last=i32(last | fill), valid=i32(valid))
    region = (ends - nch) * c
    start8 = (region[None, :] + off) // SUBLANES
    r0_window = start8 * SUBLANES - region[None, :]
    rows_used = jnp.where(cnt > 0, off + cnt - r0_window, 0)
    need = (rows_used + MOE_WINDOW - 1) // MOE_WINDOW
    return (ce, i32(cvalid), by_chunk, i32(start8).reshape(-1), i32(r0_window).reshape(-1),
            i32(need).reshape(-1))


def _moe_gather_kernel(chunk_ref, block_ref, expert_ref, r0_ref, first_ref, last_ref, valid_ref,
                       hb_ref, rankt_ref, xs_ref, acc_scr):
    w = pl.program_id(0)

    @pl.when(first_ref[w] == 1)
    def _():
        acc_scr[...] = jnp.zeros_like(acc_scr)

    @pl.when(valid_ref[w] == 1)
    def _():
        c = acc_scr.shape[0]
        rank = rankt_ref[0, pl.ds(expert_ref[w], 1), :]
        slot = lax.broadcasted_iota(jnp.int32, (c, 1), 0).astype(F32) + r0_ref[w].astype(F32)
        pick = jnp.where(rank == slot, 1.0, 0.0).astype(BF16)
        acc_scr[...] += jnp.dot(pick, hb_ref[0], preferred_element_type=F32)

    @pl.when(last_ref[w] == 1)
    def _():
        xs_ref[0] = acc_scr[...].astype(BF16)


def _moe_gather(items, hb, rankt, n_chunks):
    nb, tb, d = hb.shape
    c = MOE_SLOT_CHUNK
    names = ("chunk", "block", "expert", "r0", "first", "last", "valid")
    by_block = lambda shape: pl.BlockSpec(shape, lambda w, ch, bl, *_: (bl[w], 0, 0))
    return pl.pallas_call(
        _moe_gather_kernel,
        grid_spec=pltpu.PrefetchScalarGridSpec(
            num_scalar_prefetch=len(names), grid=(items["chunk"].shape[0],),
            in_specs=[by_block((1, tb, d)), by_block((1, SUBLANES, tb))],
            out_specs=pl.BlockSpec((1, c, d), lambda w, ch, *_: (ch[w], 0, 0)),
            scratch_shapes=[pltpu.VMEM((c, d), F32)]),
        out_shape=jax.ShapeDtypeStruct((n_chunks, c, d), BF16),
        compiler_params=_cparams("arbitrary"),
        name="moe_gather",
    )(*[items[n] for n in names], hb, rankt)


def _moe_expert_kernel(ce_ref, cvalid_ref, xs_ref, wg_ref, wu_ref, wd_ref, *rest):
    *prev, o_ref, wg_scr, wu_scr, wd_scr = rest
    k = pl.program_id(0)

    @pl.when((k == 0) | (ce_ref[k] != ce_ref[jnp.maximum(k - 1, 0)]))
    def _():
        wg_scr[...] = wg_ref[0].astype(BF16)
        wu_scr[...] = wu_ref[0].astype(BF16)
        wd_scr[...] = wd_ref[0].astype(BF16)

    @pl.when(cvalid_ref[k] == 1)
    def _():
        x = xs_ref[0]
        a = jnp.dot(x, wg_scr[...], preferred_element_type=F32)
        u = jnp.dot(x, wu_scr[...], preferred_element_type=F32)
        y = _dot(_silu(a) * u, wd_scr[...])
        o_ref[0] = y + prev[0][0] if prev else y

    @pl.when(cvalid_ref[k] == 0)
    def _():
        o_ref[0] = jnp.zeros(o_ref.shape[1:], F32)


def _moe_experts(ce, cvalid, xs, wg, wu, wd, layer):
    n_chunks, c, d = xs.shape
    fh = wg.shape[3] // MOE_FF_SPLIT
    y = None
    for half in range(MOE_FF_SPLIT):
        row = pl.BlockSpec((1, c, d), lambda k, ce, cv: (k, 0, 0))
        in_specs = [row,
                    pl.BlockSpec((None, 1, d, fh), lambda k, ce, cv, half=half: (layer, ce[k], 0, half)),
                    pl.BlockSpec((None, 1, d, fh), lambda k, ce, cv, half=half: (layer, ce[k], 0, half)),
                    pl.BlockSpec((None, 1, fh, d), lambda k, ce, cv, half=half: (layer, ce[k], half, 0))]
        args = [ce, cvalid, xs, wg, wu, wd]
        aliases = {}
        if y is not None:
            in_specs.append(row)
            args.append(y)
            aliases = {len(args) - 1: 0}
        y = pl.pallas_call(
            _moe_expert_kernel,
            grid_spec=pltpu.PrefetchScalarGridSpec(
                num_scalar_prefetch=2, grid=(n_chunks,), in_specs=in_specs, out_specs=row,
                scratch_shapes=[pltpu.VMEM((d, fh), BF16), pltpu.VMEM((d, fh), BF16),
                                pltpu.VMEM((fh, d), BF16)]),
            out_shape=jax.ShapeDtypeStruct((n_chunks, c, d), F32),
            input_output_aliases=aliases,
            compiler_params=_cparams("arbitrary"),
            name="moe_experts",
        )(*args)
    return y


def _moe_combine_kernel(start_ref, r0_ref, need_ref, *refs):
    win_refs = refs[:MOE_WINDOWS]
    rank_ref, comb_ref, x_ref, gate_ref, lng_ref, lnb_ref, o_ref, acc_scr = refs[MOE_WINDOWS:]
    j, e = pl.program_id(0), pl.program_id(1)
    w = j * pl.num_programs(1) + e

    @pl.when(e == 0)
    def _():
        acc_scr[...] = jnp.zeros_like(acc_scr)

    tb = acc_scr.shape[0]
    for k, ys_ref in enumerate(win_refs):
        @pl.when(need_ref[w] > k)
        def _(k=k, ys_ref=ys_ref):
            lane = lax.broadcasted_iota(jnp.int32, (tb, LANES), 1)
            mine = lane == e
            rank = jnp.sum(jnp.where(mine, rank_ref[0], 0.0), axis=-1, keepdims=True)
            comb = jnp.sum(jnp.where(mine, comb_ref[0], 0.0), axis=-1, keepdims=True)
            first = (r0_ref[w] + k * MOE_WINDOW).astype(F32)
            slot = lax.broadcasted_iota(jnp.int32, (1, MOE_WINDOW), 1).astype(F32) + first
            pick = jnp.where(rank == slot, 1.0, 0.0).astype(BF16)
            y = ys_ref[...]
            y_hi = y.astype(BF16)
            y_lo = (y - y_hi.astype(F32)).astype(BF16)
            rows = (jnp.dot(pick, y_hi, preferred_element_type=F32)
                    + jnp.dot(pick, y_lo, preferred_element_type=F32))
            acc_scr[...] += comb * rows

    @pl.when(e == pl.num_programs(1) - 1)
    def _():
        y = DN_ALPHA * x_ref[0] + (1.0 + gate_ref[0]) * acc_scr[...]
        o_ref[0] = _ln(y, lng_ref[...], lnb_ref[...])


def _moe_combine_ln(start8, r0, need, ys, rank, comb, xb, gate, ln_g, ln_b, ne, blocks_per_seq):
    nb, tb, d = xb.shape
    by_block = lambda width: pl.BlockSpec((1, tb, width), lambda j, e, *_: (j, 0, 0))
    vec = pl.BlockSpec((1, d), lambda j, e, *_: (0, 0))
    window = lambda k: pl.BlockSpec(
        (pl.Element(MOE_WINDOW), pl.Element(d)),
        lambda j, e, start8, *_: ((start8[j * ne + e] + k * (MOE_WINDOW // SUBLANES)) * SUBLANES, 0))
    ys2 = ys.reshape(-1, d)
    return pl.pallas_call(
        _moe_combine_kernel,
        grid_spec=pltpu.PrefetchScalarGridSpec(
            num_scalar_prefetch=3, grid=(nb, ne),
            in_specs=[window(k) for k in range(MOE_WINDOWS)] + [
                by_block(LANES), by_block(LANES), by_block(d),
                pl.BlockSpec((1, 1, d), lambda j, e, *_: (j // blocks_per_seq, 0, 0)), vec, vec],
            out_specs=by_block(d),
            scratch_shapes=[pltpu.VMEM((tb, d), F32)]),
        out_shape=jax.ShapeDtypeStruct((nb, tb, d), F32),
        compiler_params=_cparams("arbitrary", "arbitrary"),
        name="moe_combine_ln",
    )(start8, r0, need, *([ys2] * MOE_WINDOWS), rank, comb, xb, gate, ln_g.reshape(1, d),
      ln_b.reshape(1, d))


def _moe_sparse_ln(x, sc, sh, gate, router_w, wg, wu, wd, layer, ln_g, ln_b):
    g, r, d = x.shape
    tb, c, ne = MOE_TOKEN_BLOCK, MOE_SLOT_CHUNK, wg.shape[1]
    assert sc.shape[1] == 1 and r % tb == 0
    nb = g * r // tb
    assert MOE_WINDOWS * MOE_WINDOW >= tb + SUBLANES and (MOE_WINDOWS * MOE_WINDOW) % c == 0
    n_chunks = 2 * g * r // c + ne + MOE_WINDOWS * MOE_WINDOW // c
    n_items = n_chunks + ne * nb
    xb = x.reshape(nb, tb, d)
    hb, comb, rank, rankt, cnt = _moe_route(xb, sc, sh, router_w, r // tb)
    ce, cvalid, by_chunk, start8, r0, need = _moe_tables(cnt[:, 0, :ne].astype(jnp.int32), n_chunks, n_items)
    xs = _moe_gather(by_chunk, hb, rankt, n_chunks)
    ys = _moe_experts(ce, cvalid, xs, wg, wu, wd, layer)
    out = _moe_combine_ln(start8, r0, need, ys, rank, comb, xb, gate, ln_g, ln_b, ne, r // tb)
    return out.reshape(g, r, d)


def _seq_dims(b, l, long_seqs_per_step):
    if l >= SEQ_CHUNK:
        assert l % SEQ_CHUNK == 0 and b % long_seqs_per_step == 0
        return SEQ_CHUNK, l // SEQ_CHUNK, SEQ_CHUNK, long_seqs_per_step
    assert l <= SUBLANES and b % SHORT_SEQ_BLOCK == 0
    return SUBLANES, 1, l, SHORT_SEQ_BLOCK


def _pad_seq(t, c, n):
    pad = c * n - t.shape[1]
    return t if pad == 0 else jnp.pad(t, ((0, 0), (0, pad), (0, 0)))


def _rows_layout(t, c, n, width):
    b = t.shape[0]
    return jnp.swapaxes(t[:, :, :width].reshape(b, n, c, width), 2, 3)


def _conv_window(scr, u, taps, c):
    scr[SUBLANES:SUBLANES + c, :] = u
    out = scr[5:5 + c, :] * taps[0:1, :]
    for j in range(1, CONV_W):
        out = out + scr[5 + j:5 + j + c, :] * taps[j:j + 1, :]
    return out


def _seq_spec(bb, c, width):
    return pl.BlockSpec((bb, c, width), lambda b, i: (b, i, 0))


def _state_spec(bb, shape):
    return pl.BlockSpec((bb,) + shape, lambda b, i: (b,) + (0,) * len(shape))


def _const_spec(shape):
    return pl.BlockSpec(shape, lambda b, i: (0,) * len(shape))


def _lru_kernel(xy_ref, buf_ref, h0_ref, cw_ref, cb_ref, wg_ref, bg_ref, lam_ref,
                out_ref, hnew_ref, bufnew_ref, xs_scr, a_scr, b_scr, h_scr, *, bb, c, nv, pos0):
    i = pl.program_id(1)
    w = LRU_W

    @pl.when(i == 0)
    def _():
        xs_scr[:, 0:SUBLANES, :] = jnp.zeros((bb, SUBLANES, w), F32)
        xs_scr[:, 5:8, :] = buf_ref[...]
        h_scr[...] = h0_ref[...]

    pos = lax.broadcasted_iota(jnp.int32, (c, 1), 0) + (i * c + pos0)
    for bi in range(bb):
        xc = _conv_window(xs_scr.at[bi], xy_ref[bi, :, 0:w], cw_ref[...], c) + cb_ref[...]
        gates = _dot(xc, wg_ref[...]) + bg_ref[...]
        r = _sigmoid(gates[:, 0:w])
        ig = _sigmoid(gates[:, w:2 * w])
        log_a = -LRU_C * r * _softplus(-lam_ref[...])
        mult = jnp.sqrt(-jnp.tanh(log_a) * (jnp.exp(2.0 * log_a) + 1.0))
        mult = jnp.where(pos == 0, 1.0, mult)
        a_scr[bi] = jnp.exp(log_a)
        b_scr[bi] = xc * ig * mult

    def step(t, h):
        h = a_scr[:, pl.ds(t, 1), :] * h + b_scr[:, pl.ds(t, 1), :]
        b_scr[:, pl.ds(t, 1), :] = h
        return h

    h = lax.fori_loop(0, nv, step, h_scr[...], unroll=min(nv, SUBLANES))
    h_scr[...] = h
    out_ref[...] = b_scr[...] * _gelu_tanh(xy_ref[:, :, w:2 * w])
    tail = xs_scr[:, 5 + nv:8 + nv, :]
    xs_scr[:, 5:8, :] = tail

    @pl.when(i == pl.num_programs(1) - 1)
    def _():
        hnew_ref[...] = h
        bufnew_ref[...] = tail


def _lru_mixer(xy, buf, h0, conv_w, conv_b, wr, br, wi, bi, lam, pos0):
    b, l, _ = xy.shape
    c, n, nv, bb = _seq_dims(b, l, LRU_SEQS_PER_STEP)
    w = LRU_W
    bd = lambda m: jax.scipy.linalg.block_diag(*[m[i] for i in range(LRU_BLOCKS)])
    wgate = jnp.concatenate([bd(wr), bd(wi)], axis=1).astype(BF16)
    bgate = jnp.concatenate([br, bi]).reshape(1, 2 * w)
    out, h_new, buf_new = pl.pallas_call(
        functools.partial(_lru_kernel, bb=bb, c=c, nv=nv, pos0=pos0),
        grid=(b // bb, n),
        in_specs=[_seq_spec(bb, c, 2 * w), _state_spec(bb, (CONV_W - 1, w)), _state_spec(bb, (1, w)),
                  _const_spec((CONV_W, w)), _const_spec((1, w)), _const_spec((w, 2 * w)),
                  _const_spec((1, 2 * w)), _const_spec((1, w))],
        out_specs=[_seq_spec(bb, c, w), _state_spec(bb, (1, w)), _state_spec(bb, (CONV_W - 1, w))],
        out_shape=[jax.ShapeDtypeStruct((b, n * c, w), F32),
                   jax.ShapeDtypeStruct((b, 1, w), F32),
                   jax.ShapeDtypeStruct((b, CONV_W - 1, w), F32)],
        scratch_shapes=[pltpu.VMEM((bb, SUBLANES + c, w), F32), pltpu.VMEM((bb, c, w), F32),
                        pltpu.VMEM((bb, c, w), F32), pltpu.VMEM((bb, 1, w), F32)],
        compiler_params=_cparams("parallel", "arbitrary"),
        name="rglru",
    )(_pad_seq(xy, c, n), buf, h0.reshape(b, 1, w), conv_w, conv_b.reshape(1, w), wgate, bgate,
      lam.reshape(1, w))
    return out[:, :l], h_new.reshape(b, w), buf_new


def _gdn_kernel(qkv_ref, z_ref, gcol_ref, grow_ref, buf_ref, s0_ref, cw_ref, pcol_ref, prow_ref,
                nw_ref, out_ref, snew_ref, bufnew_ref, xs_scr, s_scr, *, bb, c, nv):
    i = pl.program_id(1)
    qk = GDN_QK

    @pl.when(i == 0)
    def _():
        xs_scr[:, 0:SUBLANES, :] = jnp.zeros((bb, SUBLANES, GDN_QKV), F32)
        xs_scr[:, 5:8, :] = buf_ref[...]
        s_scr[...] = s0_ref[...]

    valid = _valid_rows(c, nv)
    valid_r = lax.broadcasted_iota(jnp.int32, (1, c), 1) < nv
    alog_c, dtb_c = pcol_ref[0:1, :], pcol_ref[1:2, :]
    alog_r, dtb_r = prow_ref[:, 0:1], prow_ref[:, 1:2]
    incl, strict = _tri(c, True), _tri(c, False)
    upper = jnp.logical_not(strict)
    heads = []
    for bi in range(bb):
        x = _silu(_conv_window(xs_scr.at[bi], qkv_ref[bi], cw_ref[...], c))
        gcol = gcol_ref[bi]
        grow = grow_ref[bi, 0]
        g_col = jnp.where(valid, -jnp.exp(alog_c) * _softplus(gcol[:, 0:GDN_H] + dtb_c), 0.0)
        beta = jnp.where(valid, _sigmoid(gcol[:, GDN_H:2 * GDN_H]), 0.0)
        g_row = jnp.where(valid_r, -jnp.exp(alog_r) * _softplus(grow[0:GDN_H, :] + dtb_r), 0.0)
        gi_cols = _dot_sel_lhs(incl, g_col)
        gi_rows = _dot_sel_rhs(g_row, upper)
        for h in range(GDN_H):
            q = _l2norm(x[:, h * GDN_DK:(h + 1) * GDN_DK]) * (GDN_DK ** -0.5)
            k = _l2norm(x[:, qk + h * GDN_DK:qk + (h + 1) * GDN_DK])
            gh = g_col[:, h:h + 1]
            gi = gi_cols[:, h:h + 1]
            gx = gi - gh
            gi_row = gi_rows[h:h + 1, :]
            g_end = gi[c - 1:c, :]
            kb = k * beta[:, h:h + 1]
            b = -jnp.exp(gh) * kb
            e_end = jnp.exp(g_end - gi)
            heads.append(dict(
                m_lhs=jnp.concatenate([k, q], axis=0), m_rhs=jnp.concatenate([b, kb], axis=0),
                pair_x=jnp.where(strict, jnp.exp(jnp.where(strict, gx - gi_row, 0.0)), 0.0),
                pair_i=jnp.where(incl, jnp.exp(jnp.where(incl, gi - gi_row, 0.0)), 0.0),
                x_lhs=jnp.concatenate([k * jnp.exp(gx), q * jnp.exp(gi)], axis=0),
                v=x[:, 2 * qk + h * GDN_DV:2 * qk + (h + 1) * GDN_DV],
                bk=jnp.concatenate([b * e_end, kb * e_end], axis=0), s_decay=jnp.exp(g_end)))
    pairs = [(bi, h) for bi in range(bb) for h in range(GDN_H)]
    outs, new_states = _dplr_heads(heads, [s_scr[bi, h] for bi, h in pairs], c, nv, state_is_vk=False)
    for (bi, h), o, s_new in zip(pairs, outs, new_states):
        s_scr[bi, h] = s_new
        zh = z_ref[bi, :, h * GDN_DV:(h + 1) * GDN_DV]
        o = o * lax.rsqrt(jnp.mean(o * o, axis=-1, keepdims=True) + 1e-6) * nw_ref[...]
        out_ref[bi, :, h * GDN_DV:(h + 1) * GDN_DV] = o * _silu(zh)
    tail = xs_scr[:, 5 + nv:8 + nv, :]
    xs_scr[:, 5:8, :] = tail

    @pl.when(i == pl.num_programs(1) - 1)
    def _():
        snew_ref[...] = s_scr[...]
        bufnew_ref[...] = tail


def _gdn_mixer(qkv, z, gates, buf, s0, conv_w, a_log, dt_bias, norm_w):
    b, l, _ = qkv.shape
    c, n, nv, bb = _seq_dims(b, l, GDN_SEQS_PER_STEP)
    gates = _pad_seq(gates, c, n)
    pcol = jnp.stack([a_log, dt_bias])
    out, s_new, buf_new = pl.pallas_call(
        functools.partial(_gdn_kernel, bb=bb, c=c, nv=nv),
        grid=(b // bb, n),
        in_specs=[_seq_spec(bb, c, GDN_QKV), _seq_spec(bb, c, GDN_W), _seq_spec(bb, c, LANES),
                  pl.BlockSpec((bb, 1, SUBLANES, c), lambda b, i: (b, i, 0, 0)),
                  _state_spec(bb, (CONV_W - 1, GDN_QKV)), _state_spec(bb, (GDN_H, GDN_DK, GDN_DV)),
                  _const_spec((CONV_W, GDN_QKV)), _const_spec((2, GDN_H)), _const_spec((GDN_H, 2)),
                  _const_spec((1, GDN_DV))],
        out_specs=[_seq_spec(bb, c, GDN_W), _state_spec(bb, (GDN_H, GDN_DK, GDN_DV)),
                   _state_spec(bb, (CONV_W - 1, GDN_QKV))],
        out_shape=[jax.ShapeDtypeStruct((b, n * c, GDN_W), F32),
                   jax.ShapeDtypeStruct((b, GDN_H, GDN_DK, GDN_DV), F32),
                   jax.ShapeDtypeStruct((b, CONV_W - 1, GDN_QKV), F32)],
        scratch_shapes=[pltpu.VMEM((bb, SUBLANES + c, GDN_QKV), F32),
                        pltpu.VMEM((bb, GDN_H, GDN_DK, GDN_DV), F32)],
        compiler_params=_cparams("parallel", "arbitrary"),
        name="gdn",
    )(_pad_seq(qkv, c, n), _pad_seq(z, c, n), gates, _rows_layout(gates, c, n, SUBLANES), buf, s0,
      conv_w, pcol, pcol.T, norm_w.reshape(1, GDN_DV))
    return out[:, :l], s_new, buf_new


def _rwkv_kernel(rw_ref, prev_ref, s0_ref, mix_ref, w0_ref, w2_ref, a0_ref, a2_ref, g2_ref,
                 kk_ref, ka_ref, rk_ref, lnw_ref, lnb_ref, out_ref, snew_ref, xs_scr, s_scr,
                 *, bb, c, nv):
    i = pl.program_id(1)
    hd = RWKV_HD
    w = RWKV_W

    @pl.when(i == 0)
    def _():
        xs_scr[:, 0:SUBLANES, :] = jnp.zeros((bb, SUBLANES, RWKV_PROJ_W), F32)
        xs_scr[:, 7:8, :] = prev_ref[...]
        s_scr[...] = s0_ref[...]

    valid = _valid_rows(c, nv)
    incl = _tri(c, True)
    heads, post = [], []
    for bi in range(bb):
        rw = rw_ref[bi]
        xs_scr[bi, SUBLANES:SUBLANES + c, :] = rw
        prev = xs_scr[bi, 7:7 + c, :]
        xs_scr[bi, 7:8, :] = xs_scr[bi, 7 + nv:8 + nv, :]
        xs = rw + (prev - rw) * mix_ref[...]
        r_all, k_all, v_all = xs[:, 0:w], xs[:, w:2 * w], xs[:, 2 * w:3 * w]
        o1 = 3 * w
        wl = xs[:, o1:o1 + RWKV_RW]
        al = xs[:, o1 + RWKV_RW:o1 + RWKV_RW + RWKV_RA]
        gl = xs[:, o1 + RWKV_RW + RWKV_RA:]
        wdec = -_softplus(-(w0_ref[...] + _dot(jnp.tanh(wl), w2_ref[...]))) - 0.5
        a_all = _sigmoid(a0_ref[...] + _dot(al, a2_ref[...]))
        gate = _dot(_sigmoid(gl), g2_ref[...])
        lw_all = jnp.where(valid, -jnp.exp(wdec), 0.0)
        kmod = jnp.where(valid, k_all * (1.0 + (a_all - 1.0) * ka_ref[...]), 0.0)
        kk_all = k_all * kk_ref[...]
        gi = _dot_sel_lhs(incl, lw_all)
        gm = gi[c // 2:c // 2 + 1, :]
        g_end = gi[c - 1:c, :]
        e_nlw = jnp.exp(-lw_all)
        e_r = jnp.exp(gi - gm)
        e_a = e_r * e_nlw
        e_m = jnp.exp(gm - gi)
        e_gi = jnp.exp(gi)
        e_gx = e_gi * e_nlw
        e_end = jnp.exp(g_end - gm) * e_m
        s_dec = jnp.exp(g_end)
        for h in range(RWKV_H):
            sl = slice(h * hd, (h + 1) * hd)
            kk = _l2norm(kk_all[:, sl])
            r, k = r_all[:, sl], kmod[:, sl]
            a = -kk
            b = jnp.where(valid, kk * a_all[:, sl], 0.0)
            heads.append(dict(
                m_lhs=jnp.concatenate([a * e_a[:, sl], r * e_r[:, sl]], axis=0),
                m_rhs=jnp.concatenate([b * e_m[:, sl], k * e_m[:, sl]], axis=0),
                x_lhs=jnp.concatenate([a * e_gx[:, sl], r * e_gi[:, sl]], axis=0),
                v=v_all[:, sl],
                bk=jnp.concatenate([b * e_end[:, sl], k * e_end[:, sl]], axis=0), s_decay=s_dec[:, sl]))
            post.append((bi, h, r, k, v_all[:, sl], gate[:, sl]))
    outs, new_states = _dplr_heads(heads, [s_scr[bi, h] for bi, h, *_ in post], c, nv, state_is_vk=True)
    for (bi, h, r, k, v, gate_h), o, s_new in zip(post, outs, new_states):
        sl = slice(h * hd, (h + 1) * hd)
        s_scr[bi, h] = s_new
        y = _ln(o, lnw_ref[:, sl], lnb_ref[:, sl], RWKV_GN_EPS)
        y = y + jnp.sum(r * k * rk_ref[:, sl], axis=-1, keepdims=True) * v
        out_ref[bi, :, sl] = y * gate_h

    @pl.when(i == pl.num_programs(1) - 1)
    def _():
        snew_ref[...] = s_scr[...]


def _rwkv_mixer(rw, shift0, s0, mix, w0, w2, a0, a2, g2, k_k, k_a, r_k, ln_w, ln_b):
    b, l, _ = rw.shape
    c, n, nv, bb = _seq_dims(b, l, RWKV_SEQS_PER_STEP)
    w = RWKV_W
    row = lambda t: t.reshape(1, -1)
    state = (RWKV_H, RWKV_HD, RWKV_HD)
    out, s_new = pl.pallas_call(
        functools.partial(_rwkv_kernel, bb=bb, c=c, nv=nv),
        grid=(b // bb, n),
        in_specs=[_seq_spec(bb, c, RWKV_PROJ_W), _state_spec(bb, (1, RWKV_PROJ_W)), _state_spec(bb, state),
                  _const_spec((1, RWKV_PROJ_W)), _const_spec((1, w)), _const_spec((RWKV_RW, w)),
                  _const_spec((1, w)), _const_spec((RWKV_RA, w)), _const_spec((RWKV_RG, w)),
                  _const_spec((1, w)), _const_spec((1, w)), _const_spec((1, w)), _const_spec((1, w)),
                  _const_spec((1, w))],
        out_specs=[_seq_spec(bb, c, w), _state_spec(bb, state)],
        out_shape=[jax.ShapeDtypeStruct((b, n * c, w), F32),
                   jax.ShapeDtypeStruct((b,) + state, F32)],
        scratch_shapes=[pltpu.VMEM((bb, SUBLANES + c, RWKV_PROJ_W), F32), pltpu.VMEM((bb,) + state, F32)],
        compiler_params=_cparams("parallel", "arbitrary"),
        name="rwkv7",
    )(_pad_seq(rw, c, n), shift0.reshape(b, 1, RWKV_PROJ_W), s0, row(mix), row(w0),
      w2.astype(BF16), row(a0), a2.astype(BF16), g2.astype(BF16), row(k_k), row(k_a), row(r_k),
      row(ln_w), row(ln_b))
    return out[:, :l], s_new


def _mlstm_kernel(p_ref, gcol_ref, grow_ref, c0_ref, n0_ref, m0_ref, bcol_ref, brow_ref, nw_ref,
                  out_ref, cnew_ref, nnew_ref, mnew_ref, c_scr, n_scr, m_scr, *, bb, c, nv):
    i = pl.program_id(1)
    nh, dk, dv = MLSTM_H, MLSTM_DK, MLSTM_DV

    @pl.when(i == 0)
    def _():
        c_scr[...] = c0_ref[...]
        n_scr[...] = n0_ref[...]
        m_scr[...] = m0_ref[...]

    valid = _valid_rows(c, nv)
    valid_r = lax.broadcasted_iota(jnp.int32, (1, c), 1) < nv
    incl = _tri(c, True)
    upper = jnp.logical_not(_tri(c, False))
    ps = [(bi, h) for bi in range(bb) for h in range(nh)]
    gate_cols = {}
    for bi in range(bb):
        gcol = gcol_ref[bi]
        grow = grow_ref[bi, 0]
        li_col = jnp.where(valid, gcol[:, 0:nh] + bcol_ref[0:1, :], NEG_BIG)
        lf_col = jnp.where(valid, -_softplus(-(gcol[:, nh:2 * nh] + bcol_ref[1:2, :])), 0.0)
        li_row = jnp.where(valid_r, grow[0:nh, :] + brow_ref[:, 0:1], NEG_BIG)
        lf_row = jnp.where(valid_r, -_softplus(-(grow[nh:2 * nh, :] + brow_ref[:, 1:2])), 0.0)
        b_cols = _dot_sel_lhs(incl, lf_col)
        b_rows = _dot_sel_rhs(lf_row, upper)
        gate_cols[bi] = (li_col, li_row, b_cols, b_rows)
    qs = [p_ref[bi, :, h * dk:(h + 1) * dk] for bi, h in ps]
    ks = [p_ref[bi, :, nh * dk + h * dk:nh * dk + (h + 1) * dk] * (dk ** -0.5) for bi, h in ps]
    vs = [p_ref[bi, :, 2 * nh * dk + h * dv:2 * nh * dk + (h + 1) * dv] for bi, h in ps]
    cms = [c_scr[bi, h] for bi, h in ps]
    nvecs = [n_scr[bi, h] for bi, h in ps]
    m_prevs = [m_scr[bi, h] for bi, h in ps]
    qks = [_dot_nt(q, k) for q, k in zip(qs, ks)]
    qcs = [_dot_nt(q, cm) for q, cm in zip(qs, cms)]
    bcs = [gate_cols[bi][2][:, h:h + 1] for bi, h in ps]
    dms = [jnp.where(incl, bc - gate_cols[bi][3][h:h + 1, :] + gate_cols[bi][1][h:h + 1, :], -jnp.inf)
           for (bi, h), bc in zip(ps, bcs)]
    m_inters = [bc + m_prev for bc, m_prev in zip(bcs, m_prevs)]
    m_ts = [jnp.maximum(mi, jnp.max(dm, axis=-1, keepdims=True)) for mi, dm in zip(m_inters, dms)]
    w_inters = [jnp.exp(mi - mt) for mi, mt in zip(m_inters, m_ts)]
    scs = [qk * jnp.exp(dm - mt) for qk, dm, mt in zip(qks, dms, m_ts)]
    scvs = [_dot(sc, v) for sc, v in zip(scs, vs)]
    m_news = [mt[c - 1:c, :] for mt in m_ts]
    b_lasts = [bc[c - 1:c, :] for bc in bcs]
    wss = [jnp.exp(b_last - bc + gate_cols[bi][0][:, h:h + 1] - m_new)
           for (bi, h), b_last, bc, m_new in zip(ps, b_lasts, bcs, m_news)]
    upds = [_dot_tn(v * ws, k) for v, ws, k in zip(vs, wss, ks)]
    for j, (bi, h) in enumerate(ps):
        num = w_inters[j] * qcs[j] + scvs[j]
        den = (w_inters[j] * jnp.sum(qs[j] * nvecs[j], axis=-1, keepdims=True)
               + jnp.sum(scs[j], axis=-1, keepdims=True))
        hh = num / jnp.maximum(jnp.abs(den), jnp.exp(-m_ts[j]))
        dec = jnp.exp(b_lasts[j] + m_prevs[j] - m_news[j])
        c_scr[bi, h] = dec * cms[j] + upds[j]
        n_scr[bi, h] = dec * nvecs[j] + jnp.sum(ks[j] * wss[j], axis=0, keepdims=True)
        m_scr[bi, h] = m_news[j]
        og = p_ref[bi, :, 2 * nh * dk + nh * dv + h * dv:2 * nh * dk + nh * dv + (h + 1) * dv]
        out_ref[bi, :, h * dv:(h + 1) * dv] = _ln(hh, nw_ref[...]) * _sigmoid(og)

    @pl.when(i == pl.num_programs(1) - 1)
    def _():
        cnew_ref[...] = c_scr[...]
        nnew_ref[...] = n_scr[...]
        mnew_ref[...] = m_scr[...]


def _mlstm_mixer(p, gates, c0, n0, m0, i_b, f_b, norm_w):
    b, l, _ = p.shape
    c, n, nv, bb = _seq_dims(b, l, MLSTM_SEQS_PER_STEP)
    nh, dk, dv = MLSTM_H, MLSTM_DK, MLSTM_DV
    gates = _pad_seq(gates, c, n)
    bcol = jnp.stack([i_b, f_b])
    out, c_new, n_new, m_new = pl.pallas_call(
        functools.partial(_mlstm_kernel, bb=bb, c=c, nv=nv),
        grid=(b // bb, n),
        in_specs=[_seq_spec(bb, c, p.shape[2]), _seq_spec(bb, c, LANES),
                  pl.BlockSpec((bb, 1, SUBLANES, c), lambda b, i: (b, i, 0, 0)),
                  _state_spec(bb, (nh, dv, dk)), _state_spec(bb, (nh, 1, dk)), _state_spec(bb, (nh, 1, 1)),
                  _const_spec((2, nh)), _const_spec((nh, 2)), _const_spec((1, dv))],
        out_specs=[_seq_spec(bb, c, nh * dv), _state_spec(bb, (nh, dv, dk)),
                   _state_spec(bb, (nh, 1, dk)), _state_spec(bb, (nh, 1, 1))],
        out_shape=[jax.ShapeDtypeStruct((b, n * c, nh * dv), F32),
                   jax.ShapeDtypeStruct((b, nh, dv, dk), F32),
                   jax.ShapeDtypeStruct((b, nh, 1, dk), F32),
                   jax.ShapeDtypeStruct((b, nh, 1, 1), F32)],
        scratch_shapes=[pltpu.VMEM((bb, nh, dv, dk), F32), pltpu.VMEM((bb, nh, 1, dk), F32),
                        pltpu.VMEM((bb, nh, 1, 1), F32)],
        compiler_params=_cparams("parallel", "arbitrary"),
        name="mlstm",
    )(_pad_seq(p, c, n), gates, _rows_layout(gates, c, n, SUBLANES), c0,
      n0.reshape(b, nh, 1, dk), m0.reshape(b, nh, 1, 1), bcol, bcol.T, norm_w.reshape(1, dv))
    return out[:, :l], c_new, n_new.reshape(b, nh, dk), m_new.reshape(b, nh)


def _pad_cols(w, n):
    return jnp.pad(w, ((0, 0), (0, n - w.shape[1])))


def _ab_in_weight(w):
    o = 2 * LRU_W
    xy, qkv = w[:, :o], w[:, o:o + GDN_QKV]
    o += GDN_QKV
    ab, z = w[:, o:o + 2 * GDN_H], w[:, o + 2 * GDN_H:]
    return jnp.concatenate([xy, qkv, z, _pad_cols(ab, LANES)], axis=1).astype(BF16)


AB_SPLITS = ((0, 2 * LRU_W), (2 * LRU_W, GDN_QKV), (2 * LRU_W + GDN_QKV, GDN_W),
             (2 * LRU_W + GDN_QKV + GDN_W, LANES))


def _cd_in_weight(w):
    o = RWKV_PROJ_W
    rw, qkv = w[:, :o], w[:, o:o + 2 * MLSTM_QK + MLSTM_W]
    o += 2 * MLSTM_QK + MLSTM_W
    gates, og = w[:, o:o + 2 * MLSTM_H], w[:, o + 2 * MLSTM_H:]
    return jnp.concatenate([rw, qkv, og, _pad_cols(gates, LANES)], axis=1).astype(BF16)


CD_SPLITS = ((0, RWKV_PROJ_W), (RWKV_PROJ_W, 2 * MLSTM_QK + 2 * MLSTM_W),
             (RWKV_PROJ_W + 2 * MLSTM_QK + 2 * MLSTM_W, LANES))


def kernel(x_prompt, x_sample, c_prompt, c_sample,
           state_lru_h, state_lru_conv, state_gdn_S, state_gdn_conv,
           state_rwkv_S, state_rwkv_shift, state_mlstm_C, state_mlstm_n, state_mlstm_m,
           mod_w, mod_b, ln1_g, ln1_b, ln2_g, ln2_b,
           ab_w_in, ab_w_out, lru_conv_w, lru_conv_b, lru_wr, lru_br, lru_wi, lru_bi, lru_lambda,
           gdn_conv_w, gdn_a_log, gdn_dt_bias, gdn_norm_w,
           cd_w_in, cd_w_out, rwkv_mix, rwkv_w0, rwkv_w2, rwkv_a0, rwkv_a2, rwkv_g2,
           rwkv_k_k, rwkv_k_a, rwkv_r_k, rwkv_ln_w, rwkv_ln_b,
           mlstm_i_b, mlstm_f_b, mlstm_norm_w,
           ffn_w_gate, ffn_w_up, ffn_w_down,
           router_w, moe_w_gate, moe_w_up, moe_w_down):
    d = D_MODEL
    bp, lp, _ = x_prompt.shape
    bs, ls, _ = x_sample.shape
    mod = _modulation(jnp.concatenate([c_prompt, c_sample], axis=0), mod_w, mod_b)

    ab_in = [_ab_in_weight(ab_w_in[j]) for j in range(ab_w_in.shape[0])]
    cd_in = [_cd_in_weight(cd_w_in[j]) for j in range(cd_w_in.shape[0])]
    ab_out, cd_out = ab_w_out.astype(BF16), cd_w_out.astype(BF16)
    ffn_g, ffn_u, ffn_d = (t.astype(BF16) for t in (ffn_w_gate, ffn_w_up, ffn_w_down))

    def trunk(x, mods, batch, length, states, pos0):
        lru_h, lru_conv, gdn_s, gdn_conv, rwkv_s, rwkv_shift, m_c, m_n, m_m = states
        new = [[] for _ in range(9)]
        seq = lambda t: t.reshape(batch, length, t.shape[-1])
        tok = lambda t: t.reshape(x.shape[0], x.shape[1], t.shape[-1])
        for l in range(DEPTH):
            j = l // 2
            sh1, sc1, g1, sh2, sc2, g2 = mods[l]
            if l % 2 == 0:
                xy, qkv, z, gates = _inproj(x, sc1, sh1, ab_in[j], AB_SPLITS)
                out_a, s0, s1 = _lru_mixer(seq(xy), lru_conv[j], lru_h[j], lru_conv_w[j], lru_conv_b[j],
                                           lru_wr[j], lru_br[j], lru_wi[j], lru_bi[j], lru_lambda[j], pos0)
                out_b, s2, s3 = _gdn_mixer(seq(qkv), seq(z), seq(gates), gdn_conv[j], gdn_s[j],
                                           gdn_conv_w[j], gdn_a_log[j], gdn_dt_bias[j], gdn_norm_w[j])
                for slot, s in zip((0, 1, 2, 3), (s0, s1, s2, s3)):
                    new[slot].append(s)
                x = _outproj_ln(x, tok(out_a), tok(out_b), g1, ab_out, j, ln1_g[l], ln1_b[l])
                x = _ffn_ln(x, sc2, sh2, g2, ffn_g, ffn_u, ffn_d, j, ln2_g[l], ln2_b[l])
            else:
                rw, mp, gates = _inproj(x, sc1, sh1, cd_in[j], CD_SPLITS)
                out_c, s0 = _rwkv_mixer(seq(rw), rwkv_shift[j], rwkv_s[j], rwkv_mix[j], rwkv_w0[j],
                                        rwkv_w2[j], rwkv_a0[j], rwkv_a2[j], rwkv_g2[j], rwkv_k_k[j],
                                        rwkv_k_a[j], rwkv_r_k[j], rwkv_ln_w[j], rwkv_ln_b[j])
                out_d, s2, s3, s4 = _mlstm_mixer(seq(mp), seq(gates), m_c[j], m_n[j], m_m[j],
                                                 mlstm_i_b[j], mlstm_f_b[j], mlstm_norm_w[j])
                for slot, s in zip((4, 5, 6, 7, 8), (s0, seq(rw)[:, -1], s2, s3, s4)):
                    new[slot].append(s)
                x = _outproj_ln(x, tok(out_c), tok(out_d), g1, cd_out, j, ln1_g[l], ln1_b[l])
                moe = _moe_sparse_ln if x.shape[1] % MOE_TOKEN_BLOCK == 0 else _moe_ln
                x = moe(x, sc2, sh2, g2, router_w[j], moe_w_gate, moe_w_up, moe_w_down, j,
                        ln2_g[l], ln2_b[l])
        return x, tuple(jnp.stack(s) for s in new)

    def zeros(ref):
        return jnp.zeros((ref.shape[0], bp) + ref.shape[2:], F32)

    mods_p = [[mod[l, :bp, k * d:(k + 1) * d].reshape(bp, 1, d) for k in range(6)] for l in range(DEPTH)]
    mods_s = [[mod[l, bp:, k * d:(k + 1) * d].reshape(1, bs * ls, d) for k in range(6)] for l in range(DEPTH)]
    states_s = (state_lru_h, state_lru_conv, state_gdn_S, state_gdn_conv, state_rwkv_S,
                state_rwkv_shift, state_mlstm_C, state_mlstm_n, state_mlstm_m)
    y_p, new_p = trunk(x_prompt, mods_p, bp, lp, tuple(zeros(s) for s in states_s), 0)
    y_s, new_s = trunk(x_sample.reshape(1, bs * ls, d), mods_s, bs, ls, states_s, PAST_LEN)
    out = [y_p, y_s.reshape(bs, ls, d)]
    for p_leaf, s_leaf in zip(new_p, new_s):
        out += [p_leaf, s_leaf]
    return tuple(out)
```

```python
import functools
import math

import jax
import jax.numpy as jnp
from jax import lax
from jax.experimental import pallas as pl
from jax.experimental.pallas import tpu as pltpu

F32 = jnp.float32
BF16 = jnp.bfloat16

D_MODEL = 1024
DEPTH = 4
PAST_LEN = 16384
CONV_W = 4
LRU_W = D_MODEL // 2
LRU_BLOCKS = 8
LRU_BW = LRU_W // LRU_BLOCKS
LRU_C = 8.0
GDN_H = D_MODEL // 256
GDN_DK = 128
GDN_DV = 128
GDN_QK = GDN_H * GDN_DK
GDN_W = GDN_H * GDN_DV
GDN_QKV = 2 * GDN_QK + GDN_W
RWKV_HD = 64
RWKV_H = D_MODEL // 2 // RWKV_HD
RWKV_W = RWKV_H * RWKV_HD
RWKV_RW = 64
RWKV_RA = 64
RWKV_RG = 128
RWKV_PROJ_W = 3 * RWKV_W + RWKV_RW + RWKV_RA + RWKV_RG
RWKV_GN_EPS = 64e-5
MLSTM_H = D_MODEL // 256
MLSTM_DK = 128
MLSTM_DV = 128
MLSTM_QK = MLSTM_H * MLSTM_DK
MLSTM_W = MLSTM_H * MLSTM_DV
D_FF = 7 * D_MODEL // 2
N_EXPERTS = 8
LN_EPS = 1e-5
NEG_BIG = -1e30
DN_ALPHA = (2.0 * DEPTH) ** 0.25

LANES = 128
SUBLANES = 8
SEQ_CHUNK = 64
SHORT_SEQ_BLOCK = 8
LRU_SEQS_PER_STEP = 8
GDN_SEQS_PER_STEP = 4
RWKV_SEQS_PER_STEP = 2
MLSTM_SEQS_PER_STEP = 1
ROW_TILE = 512
FFN_ROW_TILE = 1024
FF_TILE = 512
MOE_TOKEN_BLOCK = 512
MOE_SLOT_CHUNK = 512
MOE_FF_SPLIT = 4
MOE_WINDOW = 256
MOE_WINDOWS = 3
VMEM_LIMIT = 48 * 1024 * 1024


def _cparams(*sem):
    return pltpu.CompilerParams(dimension_semantics=sem, vmem_limit_bytes=VMEM_LIMIT)


def _dot(a, b):
    return jnp.dot(a.astype(BF16), b.astype(BF16), preferred_element_type=F32)


def _dot_nt(a, b):
    return lax.dot_general(a.astype(BF16), b.astype(BF16), (((1,), (1,)), ((), ())),
                           preferred_element_type=F32)


def _dot_tn(a, b):
    return lax.dot_general(a.astype(BF16), b.astype(BF16), (((0,), (0,)), ((), ())),
                           preferred_element_type=F32)


def _split3(x):
    hi = x.astype(BF16)
    r1 = x - hi.astype(F32)
    mid = r1.astype(BF16)
    lo = (r1 - mid.astype(F32)).astype(BF16)
    return hi, mid, lo


def _dot_sel_lhs(t, x):
    tb = jnp.where(t, 1.0, 0.0).astype(BF16)
    hi, mid, lo = _split3(x)
    d = lambda p: jnp.dot(tb, p, preferred_element_type=F32)
    return d(hi) + d(mid) + d(lo)


def _dot_sel_rhs(x, t):
    tb = jnp.where(t, 1.0, 0.0).astype(BF16)
    hi, mid, lo = _split3(x)
    d = lambda p: jnp.dot(p, tb, preferred_element_type=F32)
    return d(hi) + d(mid) + d(lo)


def _dot2(p, x):
    pb = p.astype(BF16)
    xh = x.astype(BF16)
    xl = (x - xh.astype(F32)).astype(BF16)
    return (jnp.dot(pb, xh, preferred_element_type=F32)
            + jnp.dot(pb, xl, preferred_element_type=F32))


def _sigmoid(x):
    return 1.0 / (1.0 + jnp.exp(-x))


def _silu(x):
    return x * _sigmoid(x)


def _softplus(x):
    return jnp.maximum(x, 0.0) + jnp.log1p(jnp.exp(-jnp.abs(x)))


def _gelu_tanh(x):
    return 0.5 * x * (1.0 + jnp.tanh(math.sqrt(2.0 / math.pi) * (x + 0.044715 * (x * x * x))))


def _ln(y, g=None, b=None, eps=LN_EPS):
    mu = jnp.mean(y, axis=-1, keepdims=True)
    d = y - mu
    var = jnp.mean(d * d, axis=-1, keepdims=True)
    out = d * lax.rsqrt(var + eps)
    if g is not None:
        out = out * g
    if b is not None:
        out = out + b
    return out


def _l2norm(x, eps=1e-6):
    return x * lax.rsqrt(jnp.sum(x * x, axis=-1, keepdims=True) + eps)


def _tri(c, inclusive):
    t = lax.broadcasted_iota(jnp.int32, (c, c), 0)
    s = lax.broadcasted_iota(jnp.int32, (c, c), 1)
    return (s <= t) if inclusive else (s < t)


def _unit_lower_solve(ns, xs, c):
    steps = max(1, int(math.ceil(math.log2(c))))
    for i in range(steps):
        xs = [x + _dot2(p, x) for p, x in zip(ns, xs)]
        if i + 1 < steps:
            ns = [_dot(p, p) for p in ns]
    return xs


def _dplr_heads(heads, states, c, nv, state_is_vk):
    incl = _tri(c, True)
    strict = _tri(c, False)
    ms = [_dot_nt(h["m_lhs"], h["m_rhs"]) for h in heads]
    if state_is_vk:
        xhs = [_dot_nt(h["x_lhs"], s) for h, s in zip(heads, states)]
    else:
        xhs = [_dot(h["x_lhs"], s) for h, s in zip(heads, states)]
    a_abs, a_aks, r_bks = [], [], []
    for h, m in zip(heads, ms):
        if "pair_x" in h:
            a_abs.append(m[:c, :c] * h["pair_x"])
            a_aks.append(m[:c, c:] * h["pair_x"])
            r_bks.append(jnp.concatenate([m[c:, :c] * h["pair_i"], m[c:, c:] * h["pair_i"]], axis=1))
        else:
            a_abs.append(jnp.where(strict, m[:c, :c], 0.0))
            a_aks.append(jnp.where(strict, m[:c, c:], 0.0))
            r_bks.append(jnp.concatenate([jnp.where(incl, m[c:, :c], 0.0),
                                          jnp.where(incl, m[c:, c:], 0.0)], axis=1))
    if nv == 1:
        us = [xh[:c] for xh in xhs]
    else:
        rhs = [xh[:c] + _dot(a_ak, h["v"]) for xh, a_ak, h in zip(xhs, a_aks, heads)]
        us = _unit_lower_solve(a_abs, rhs, c)
    uvs = [jnp.concatenate([u, h["v"]], axis=0) for u, h in zip(us, heads)]
    outs = [xh[c:] + _dot(r_bk, uv) for xh, r_bk, uv in zip(xhs, r_bks, uvs)]
    if state_is_vk:
        new = [s * h["s_decay"] + _dot_tn(uv, h["bk"]) for s, h, uv in zip(states, heads, uvs)]
    else:
        new = [s * h["s_decay"] + _dot_tn(h["bk"], uv) for s, h, uv in zip(states, heads, uvs)]
    return outs, new


def _valid_rows(c, n_valid):
    return lax.broadcasted_iota(jnp.int32, (c, 1), 0) < n_valid


def _mod_kernel(c_ref, w_ref, b_ref, o_ref):
    o_ref[0] = _dot(_silu(c_ref[...]), w_ref[0]) + b_ref[0]


def _modulation(c_all, mod_w, mod_b):
    n = c_all.shape[0]
    d = D_MODEL
    return pl.pallas_call(
        _mod_kernel,
        grid=(DEPTH, 6),
        in_specs=[pl.BlockSpec((n, d), lambda l, j: (0, 0)),
                  pl.BlockSpec((1, d, d), lambda l, j: (l, 0, j)),
                  pl.BlockSpec((1, 1, d), lambda l, j: (l, 0, j))],
        out_specs=pl.BlockSpec((1, n, d), lambda l, j: (l, 0, j)),
        out_shape=jax.ShapeDtypeStruct((DEPTH, n, 6 * d), F32),
        compiler_params=_cparams("parallel", "parallel"),
        name="modulation",
    )(c_all, mod_w, mod_b.reshape(DEPTH, 1, 6 * d))


def _mod_spec(mod, tm):
    if mod.shape[1] == 1:
        return pl.BlockSpec((1, 1, mod.shape[2]), lambda g, i, *_: (g, 0, 0))
    return pl.BlockSpec((1, tm, mod.shape[2]), lambda g, i, *_: (g, i, 0))


def _inproj_kernel(x_ref, sc_ref, sh_ref, w_ref, *o_refs, splits):
    h = (x_ref[0] * (1.0 + sc_ref[0]) + sh_ref[0]).astype(BF16)
    for o_ref, (s, n) in zip(o_refs, splits):
        o_ref[0] = jnp.dot(h, w_ref[:, s:s + n], preferred_element_type=F32)


def _inproj(x, sc, sh, w, splits):
    g, r, d = x.shape
    tm = min(r, ROW_TILE)
    n_all = w.shape[1]
    return pl.pallas_call(
        functools.partial(_inproj_kernel, splits=splits),
        grid=(g, r // tm),
        in_specs=[pl.BlockSpec((1, tm, d), lambda g, i: (g, i, 0)),
                  _mod_spec(sc, tm), _mod_spec(sh, tm),
                  pl.BlockSpec((d, n_all), lambda g, i: (0, 0))],
        out_specs=[pl.BlockSpec((1, tm, n), lambda g, i: (g, i, 0)) for _, n in splits],
        out_shape=[jax.ShapeDtypeStruct((g, r, n), F32) for _, n in splits],
        compiler_params=_cparams("parallel", "parallel"),
        name="inproj",
    )(x, sc, sh, w)


def _outproj_ln_kernel(x_ref, ma_ref, mb_ref, gate_ref, w_ref, lng_ref, lnb_ref, o_ref):
    half = ma_ref.shape[2]
    f = _dot(ma_ref[0], w_ref[0:half, :]) + _dot(mb_ref[0], w_ref[half:, :])
    y = DN_ALPHA * x_ref[0] + (1.0 + gate_ref[0]) * f
    o_ref[0] = _ln(y, lng_ref[...], lnb_ref[...])


def _outproj_ln(x, mix_a, mix_b, gate, w, layer, ln_g, ln_b):
    g, r, d = x.shape
    tm = min(r, ROW_TILE)
    half = mix_a.shape[2]
    row = pl.BlockSpec((1, tm, d), lambda g, i: (g, i, 0))
    mrow = pl.BlockSpec((1, tm, half), lambda g, i: (g, i, 0))
    vec = pl.BlockSpec((1, d), lambda g, i: (0, 0))
    return pl.pallas_call(
        _outproj_ln_kernel,
        grid=(g, r // tm),
        in_specs=[row, mrow, mrow, _mod_spec(gate, tm),
                  pl.BlockSpec((None, 2 * half, d), lambda g, i: (layer, 0, 0)), vec, vec],
        out_specs=row,
        out_shape=jax.ShapeDtypeStruct((g, r, d), F32),
        compiler_params=_cparams("parallel", "parallel"),
        name="outproj_ln",
    )(x, mix_a, mix_b, gate, w, ln_g.reshape(1, d), ln_b.reshape(1, d))


def _ffn_kernel(x_ref, sc_ref, sh_ref, gate_ref, wg_ref, wu_ref, wd_ref, lng_ref, lnb_ref,
                o_ref, h_scr, acc_scr):
    j = pl.program_id(2)

    @pl.when(j == 0)
    def _():
        h_scr[...] = (x_ref[0] * (1.0 + sc_ref[0]) + sh_ref[0]).astype(BF16)
        acc_scr[...] = jnp.zeros_like(acc_scr)

    h = h_scr[...]
    a = jnp.dot(h, wg_ref[...], preferred_element_type=F32)
    u = jnp.dot(h, wu_ref[...], preferred_element_type=F32)
    acc_scr[...] += _dot(_silu(a) * u, wd_ref[...])

    @pl.when(j == pl.num_programs(2) - 1)
    def _():
        y = DN_ALPHA * x_ref[0] + (1.0 + gate_ref[0]) * acc_scr[...]
        o_ref[0] = _ln(y, lng_ref[...], lnb_ref[...])


def _ffn_ln(x, sc, sh, gate, wg, wu, wd, layer, ln_g, ln_b):
    g, r, d = x.shape
    tm = min(r, FFN_ROW_TILE)
    f = wg.shape[2]
    row = pl.BlockSpec((1, tm, d), lambda g, i, j: (g, i, 0))
    vec = pl.BlockSpec((1, d), lambda g, i, j: (0, 0))
    return pl.pallas_call(
        _ffn_kernel,
        grid=(g, r // tm, f // FF_TILE),
        in_specs=[row, _mod_spec(sc, tm), _mod_spec(sh, tm), _mod_spec(gate, tm),
                  pl.BlockSpec((None, d, FF_TILE), lambda g, i, j: (layer, 0, j)),
                  pl.BlockSpec((None, d, FF_TILE), lambda g, i, j: (layer, 0, j)),
                  pl.BlockSpec((None, FF_TILE, d), lambda g, i, j: (layer, j, 0)), vec, vec],
        out_specs=row,
        out_shape=jax.ShapeDtypeStruct((g, r, d), F32),
        scratch_shapes=[pltpu.VMEM((tm, d), BF16), pltpu.VMEM((tm, d), F32)],
        compiler_params=_cparams("parallel", "parallel", "arbitrary"),
        name="ffn_ln",
    )(x, sc, sh, gate, wg, wu, wd, ln_g.reshape(1, d), ln_b.reshape(1, d))


def _top2_route(h, rw, lane):
    logits = jnp.dot(h, rw, preferred_element_type=F32, precision=lax.Precision.HIGHEST)
    logits = jnp.where(lane < N_EXPERTS, logits, -jnp.inf)
    m1 = jnp.max(logits, axis=-1, keepdims=True)
    i1 = jnp.min(jnp.where(logits == m1, lane, LANES), axis=-1, keepdims=True)
    rest = jnp.where(lane == i1, -jnp.inf, logits)
    m2 = jnp.max(rest, axis=-1, keepdims=True)
    i2 = jnp.min(jnp.where(rest == m2, lane, LANES), axis=-1, keepdims=True)
    e2 = jnp.exp(m2 - m1)
    g1 = 1.0 / (1.0 + e2)
    g2 = e2 / (1.0 + e2)
    sel = (lane == i1) | (lane == i2)
    return sel, jnp.where(lane == i1, g1, 0.0) + jnp.where(lane == i2, g2, 0.0)


def _moe_kernel(x_ref, sc_ref, sh_ref, gate_ref, rw_ref, wg_ref, wu_ref, wd_ref, lng_ref, lnb_ref,
                o_ref, h_scr, comb_scr, acc_scr):
    e = pl.program_id(2)
    j = pl.program_id(3)
    lane = lax.broadcasted_iota(jnp.int32, comb_scr.shape, 1)

    @pl.when((e == 0) & (j == 0))
    def _():
        h = x_ref[0] * (1.0 + sc_ref[0]) + sh_ref[0]
        h_scr[...] = h.astype(BF16)
        acc_scr[...] = jnp.zeros_like(acc_scr)
        _, comb_scr[...] = _top2_route(h, rw_ref[...], lane)

    h = h_scr[...]
    a = _dot(h, wg_ref[0])
    u = _dot(h, wu_ref[0])
    comb_e = jnp.sum(jnp.where(lane == e, comb_scr[...], 0.0), axis=-1, keepdims=True)
    acc_scr[...] += comb_e * _dot(_silu(a) * u, wd_ref[0])

    @pl.when((e == pl.num_programs(2) - 1) & (j == pl.num_programs(3) - 1))
    def _():
        y = DN_ALPHA * x_ref[0] + (1.0 + gate_ref[0]) * acc_scr[...]
        o_ref[0] = _ln(y, lng_ref[...], lnb_ref[...])


def _moe_ln(x, sc, sh, gate, router_w, wg, wu, wd, layer, ln_g, ln_b):
    g, r, d = x.shape
    tm = min(r, ROW_TILE)
    _, ne, _, f = wg.shape
    row = pl.BlockSpec((1, tm, d), lambda g, i, e, j: (g, i, 0))
    vec = pl.BlockSpec((1, d), lambda g, i, e, j: (0, 0))
    rw = jnp.pad(router_w, ((0, 0), (0, LANES - ne)))
    return pl.pallas_call(
        _moe_kernel,
        grid=(g, r // tm, ne, f // FF_TILE),
        in_specs=[row, _mod_spec(sc, tm), _mod_spec(sh, tm), _mod_spec(gate, tm),
                  pl.BlockSpec((d, LANES), lambda g, i, e, j: (0, 0)),
                  pl.BlockSpec((None, 1, d, FF_TILE), lambda g, i, e, j: (layer, e, 0, j)),
                  pl.BlockSpec((None, 1, d, FF_TILE), lambda g, i, e, j: (layer, e, 0, j)),
                  pl.BlockSpec((None, 1, FF_TILE, d), lambda g, i, e, j: (layer, e, j, 0)), vec, vec],
        out_specs=row,
        out_shape=jax.ShapeDtypeStruct((g, r, d), F32),
        scratch_shapes=[pltpu.VMEM((tm, d), BF16), pltpu.VMEM((tm, LANES), F32),
                        pltpu.VMEM((tm, d), F32)],
        compiler_params=_cparams("parallel", "parallel", "arbitrary", "arbitrary"),
        name="moe_ln",
    )(x, sc, sh, gate, rw, wg, wu, wd, ln_g.reshape(1, d), ln_b.reshape(1, d))


def _moe_route_kernel(x_ref, sc_ref, sh_ref, rw_ref, hb_ref, comb_ref, rank_ref, rankt_ref, cnt_ref,
                      run_scr):
    @pl.when(pl.program_id(0) == 0)
    def _():
        run_scr[...] = jnp.zeros_like(run_scr)

    h = x_ref[0] * (1.0 + sc_ref[0]) + sh_ref[0]
    hb_ref[0] = h.astype(BF16)
    tb = h.shape[0]
    lane = lax.broadcasted_iota(jnp.int32, (tb, LANES), 1)
    sel, comb = _top2_route(h, rw_ref[...], lane)
    comb_ref[0] = comb
    ones = jnp.where(sel, 1.0, 0.0)
    before = jnp.dot(jnp.where(_tri(tb, False), 1.0, 0.0).astype(BF16), ones.astype(BF16),
                     preferred_element_type=F32)
    rank = jnp.where(sel, before + run_scr[...], -1.0)
    rank_ref[0] = rank
    rankt_ref[0] = rank.T[0:SUBLANES, :]
    cnt = jnp.sum(ones, axis=0, keepdims=True)
    cnt_ref[0] = cnt
    run_scr[...] += cnt


def _moe_route(xb, sc, sh, router_w, blocks_per_seq):
    nb, tb, d = xb.shape
    seq = lambda j: (j // blocks_per_seq, 0, 0)
    blk = lambda width: pl.BlockSpec((1, tb, width), lambda j: (j, 0, 0))
    return pl.pallas_call(
        _moe_route_kernel,
        grid=(nb,),
        in_specs=[blk(d), pl.BlockSpec((1, 1, d), seq), pl.BlockSpec((1, 1, d), seq),
                  pl.BlockSpec((d, LANES), lambda j: (0, 0))],
        out_specs=[blk(d), blk(LANES), blk(LANES),
                   pl.BlockSpec((1, SUBLANES, tb), lambda j: (j, 0, 0)),
                   pl.BlockSpec((1, 1, LANES), lambda j: (j, 0, 0))],
        out_shape=[jax.ShapeDtypeStruct((nb, tb, d), BF16),
                   jax.ShapeDtypeStruct((nb, tb, LANES), F32),
                   jax.ShapeDtypeStruct((nb, tb, LANES), F32),
                   jax.ShapeDtypeStruct((nb, SUBLANES, tb), F32),
                   jax.ShapeDtypeStruct((nb, 1, LANES), F32)],
        scratch_shapes=[pltpu.VMEM((1, LANES), F32)],
        compiler_params=_cparams("arbitrary"),
        name="moe_route",
    )(xb, sc, sh, jnp.pad(router_w, ((0, 0), (0, LANES - router_w.shape[1]))))


def _moe_tables(cnt, n_chunks, n_items):
    nb, ne = cnt.shape
    c = MOE_SLOT_CHUNK
    off = jnp.cumsum(cnt, axis=0) - cnt
    total = jnp.sum(cnt, axis=0)
    nch = (total + c - 1) // c
    ends = jnp.cumsum(nch)
    k = jnp.arange(n_chunks, dtype=jnp.int32)
    ce = jnp.minimum(jnp.searchsorted(ends, k, side="right"), ne - 1).astype(jnp.int32)
    cvalid = k < ends[-1]
    r0 = (k - (ends - nch)[ce]) * c
    lo = jnp.maximum(r0[:, None], off.T[ce])
    hi = jnp.minimum(r0[:, None] + c, (off + cnt).T[ce])
    overlap = cvalid[:, None] & (lo < hi)
    n_pairs = jnp.sum(overlap)
    pos = jnp.arange(n_items, dtype=jnp.int32)

    flat = jnp.nonzero(overlap.reshape(-1), size=n_items, fill_value=0)[0].astype(jnp.int32)
    flat = jnp.where(pos < n_pairs, flat, flat[jnp.maximum(n_pairs - 1, 0)])
    chunk, block = flat // nb, flat % nb
    valid = pos < n_pairs
    first = valid & ((pos == 0) | (chunk != jnp.roll(chunk, 1)))
    last = valid & ((pos == n_pairs - 1) | (chunk != jnp.roll(chunk, -1)))
    spare_chunk = ends[-1] + (pos - n_pairs)
    fill = (~valid) & (spare_chunk < n_chunks)
    chunk = jnp.where(valid, chunk, jnp.minimum(spare_chunk, n_chunks - 1)).astype(jnp.int32)
    i32 = lambda t: t.astype(jnp.int32)
    by_chunk = dict(chunk=chunk, block=block, expert=ce[chunk], r0=r0[chunk], first=i32(first | fill),
                    last=i32(last | fill), valid=i32(valid))
    region = (ends - nch) * c
    start8 = (region[None, :] + off) // SUBLANES
    r0_window = start8 * SUBLANES - region[None, :]
    rows_used = jnp.where(cnt > 0, off + cnt - r0_window, 0)
    need = (rows_used + MOE_WINDOW - 1) // MOE_WINDOW
    return (ce, i32(cvalid), by_chunk, i32(start8).reshape(-1), i32(r0_window).reshape(-1),
            i32(need).reshape(-1))


def _moe_gather_kernel(chunk_ref, block_ref, expert_ref, r0_ref, first_ref, last_ref, valid_ref,
                       hb_ref, rankt_ref, xs_ref, acc_scr):
    w = pl.program_id(0)

    @pl.when(first_ref[w] == 1)
    def _():
        acc_scr[...] = jnp.zeros_like(acc_scr)

    @pl.when(valid_ref[w] == 1)
    def _():
        c = acc_scr.shape[0]
        rank = rankt_ref[0, pl.ds(expert_ref[w], 1), :]
        slot = lax.broadcasted_iota(jnp.int32, (c, 1), 0).astype(F32) + r0_ref[w].astype(F32)
        pick = jnp.where(rank == slot, 1.0, 0.0).astype(BF16)
        acc_scr[...] += jnp.dot(pick, hb_ref[0], preferred_element_type=F32)

    @pl.when(last_ref[w] == 1)
    def _():
        xs_ref[0] = acc_scr[...].astype(BF16)


def _moe_gather(items, hb, rankt, n_chunks):
    nb, tb, d = hb.shape
    c = MOE_SLOT_CHUNK
    names = ("chunk", "block", "expert", "r0", "first", "last", "valid")
    by_block = lambda shape: pl.BlockSpec(shape, lambda w, ch, bl, *_: (bl[w], 0, 0))
    return pl.pallas_call(
        _moe_gather_kernel,
        grid_spec=pltpu.PrefetchScalarGridSpec(
            num_scalar_prefetch=len(names), grid=(items["chunk"].shape[0],),
            in_specs=[by_block((1, tb, d)), by_block((1, SUBLANES, tb))],
            out_specs=pl.BlockSpec((1, c, d), lambda w, ch, *_: (ch[w], 0, 0)),
            scratch_shapes=[pltpu.VMEM((c, d), F32)]),
        out_shape=jax.ShapeDtypeStruct((n_chunks, c, d), BF16),
        compiler_params=_cparams("arbitrary"),
        name="moe_gather",
    )(*[items[n] for n in names], hb, rankt)


def _moe_expert_kernel(ce_ref, cvalid_ref, xs_ref, wg_ref, wu_ref, wd_ref, *rest):
    *prev, o_ref, wg_scr, wu_scr, wd_scr = rest
    k = pl.program_id(0)

    @pl.when((k == 0) | (ce_ref[k] != ce_ref[jnp.maximum(k - 1, 0)]))
    def _():
        wg_scr[...] = wg_ref[0].astype(BF16)
        wu_scr[...] = wu_ref[0].astype(BF16)
        wd_scr[...] = wd_ref[0].astype(BF16)

    @pl.when(cvalid_ref[k] == 1)
    def _():
        x = xs_ref[0]
        a = jnp.dot(x, wg_scr[...], preferred_element_type=F32)
        u = jnp.dot(x, wu_scr[...], preferred_element_type=F32)
        y = _dot(_silu(a) * u, wd_scr[...])
        o_ref[0] = y + prev[0][0] if prev else y

    @pl.when(cvalid_ref[k] == 0)
    def _():
        o_ref[0] = jnp.zeros(o_ref.shape[1:], F32)


def _moe_experts(ce, cvalid, xs, wg, wu, wd, layer):
    n_chunks, c, d = xs.shape
    fh = wg.shape[3] // MOE_FF_SPLIT
    y = None
    for half in range(MOE_FF_SPLIT):
        row = pl.BlockSpec((1, c, d), lambda k, ce, cv: (k, 0, 0))
        in_specs = [row,
                    pl.BlockSpec((None, 1, d, fh), lambda k, ce, cv, half=half: (layer, ce[k], 0, half)),
                    pl.BlockSpec((None, 1, d, fh), lambda k, ce, cv, half=half: (layer, ce[k], 0, half)),
                    pl.BlockSpec((None, 1, fh, d), lambda k, ce, cv, half=half: (layer, ce[k], half, 0))]
        args = [ce, cvalid, xs, wg, wu, wd]
        aliases = {}
        if y is not None:
            in_specs.append(row)
            args.append(y)
            aliases = {len(args) - 1: 0}
        y = pl.pallas_call(
            _moe_expert_kernel,
            grid_spec=pltpu.PrefetchScalarGridSpec(
                num_scalar_prefetch=2, grid=(n_chunks,), in_specs=in_specs, out_specs=row,
                scratch_shapes=[pltpu.VMEM((d, fh), BF16), pltpu.VMEM((d, fh), BF16),
                                pltpu.VMEM((fh, d), BF16)]),
            out_shape=jax.ShapeDtypeStruct((n_chunks, c, d), F32),
            input_output_aliases=aliases,
            compiler_params=_cparams("arbitrary"),
            name="moe_experts",
        )(*args)
    return y


def _moe_combine_kernel(start_ref, r0_ref, need_ref, *refs):
    win_refs = refs[:MOE_WINDOWS]
    rank_ref, comb_ref, x_ref, gate_ref, lng_ref, lnb_ref, o_ref, acc_scr = refs[MOE_WINDOWS:]
    j, e = pl.program_id(0), pl.program_id(1)
    w = j * pl.num_programs(1) + e

    @pl.when(e == 0)
    def _():
        acc_scr[...] = jnp.zeros_like(acc_scr)

    tb = acc_scr.shape[0]
    for k, ys_ref in enumerate(win_refs):
        @pl.when(need_ref[w] > k)
        def _(k=k, ys_ref=ys_ref):
            lane = lax.broadcasted_iota(jnp.int32, (tb, LANES), 1)
            mine = lane == e
            rank = jnp.sum(jnp.where(mine, rank_ref[0], 0.0), axis=-1, keepdims=True)
            comb = jnp.sum(jnp.where(mine, comb_ref[0], 0.0), axis=-1, keepdims=True)
            first = (r0_ref[w] + k * MOE_WINDOW).astype(F32)
            slot = lax.broadcasted_iota(jnp.int32, (1, MOE_WINDOW), 1).astype(F32) + first
            pick = jnp.where(rank == slot, 1.0, 0.0).astype(BF16)
            rows = jnp.dot(pick, ys_ref[...].astype(BF16), preferred_element_type=F32)
            acc_scr[...] += comb * rows

    @pl.when(e == pl.num_programs(1) - 1)
    def _():
        y = DN_ALPHA * x_ref[0] + (1.0 + gate_ref[0]) * acc_scr[...]
        o_ref[0] = _ln(y, lng_ref[...], lnb_ref[...])


def _moe_combine_ln(start8, r0, need, ys, rank, comb, xb, gate, ln_g, ln_b, ne, blocks_per_seq):
    nb, tb, d = xb.shape
    by_block = lambda width: pl.BlockSpec((1, tb, width), lambda j, e, *_: (j, 0, 0))
    vec = pl.BlockSpec((1, d), lambda j, e, *_: (0, 0))
    window = lambda k: pl.BlockSpec(
        (pl.Element(MOE_WINDOW), pl.Element(d)),
        lambda j, e, start8, *_: ((start8[j * ne + e] + k * (MOE_WINDOW // SUBLANES)) * SUBLANES, 0))
    ys2 = ys.reshape(-1, d)
    return pl.pallas_call(
        _moe_combine_kernel,
        grid_spec=pltpu.PrefetchScalarGridSpec(
            num_scalar_prefetch=3, grid=(nb, ne),
            in_specs=[window(k) for k in range(MOE_WINDOWS)] + [
                by_block(LANES), by_block(LANES), by_block(d),
                pl.BlockSpec((1, 1, d), lambda j, e, *_: (j // blocks_per_seq, 0, 0)), vec, vec],
            out_specs=by_block(d),
            scratch_shapes=[pltpu.VMEM((tb, d), F32)]),
        out_shape=jax.ShapeDtypeStruct((nb, tb, d), F32),
        compiler_params=_cparams("arbitrary", "arbitrary"),
        name="moe_combine_ln",
    )(start8, r0, need, *([ys2] * MOE_WINDOWS), rank, comb, xb, gate, ln_g.reshape(1, d),
      ln_b.reshape(1, d))


def _moe_sparse_ln(x, sc, sh, gate, router_w, wg, wu, wd, layer, ln_g, ln_b):
    g, r, d = x.shape
    tb, c, ne = MOE_TOKEN_BLOCK, MOE_SLOT_CHUNK, wg.shape[1]
    assert sc.shape[1] == 1 and r % tb == 0
    nb = g * r // tb
    assert MOE_WINDOWS * MOE_WINDOW >= tb + SUBLANES
    n_chunks = 2 * g * r // c + ne + -(-MOE_WINDOWS * MOE_WINDOW // c)
    n_items = n_chunks + ne * nb
    xb = x.reshape(nb, tb, d)
    hb, comb, rank, rankt, cnt = _moe_route(xb, sc, sh, router_w, r // tb)
    ce, cvalid, by_chunk, start8, r0, need = _moe_tables(cnt[:, 0, :ne].astype(jnp.int32), n_chunks, n_items)
    xs = _moe_gather(by_chunk, hb, rankt, n_chunks)
    ys = _moe_experts(ce, cvalid, xs, wg, wu, wd, layer)
    out = _moe_combine_ln(start8, r0, need, ys, rank, comb, xb, gate, ln_g, ln_b, ne, r // tb)
    return out.reshape(g, r, d)


def _seq_dims(b, l, long_seqs_per_step):
    if l >= SEQ_CHUNK:
        assert l % SEQ_CHUNK == 0 and b % long_seqs_per_step == 0
        return SEQ_CHUNK, l // SEQ_CHUNK, SEQ_CHUNK, long_seqs_per_step
    assert l <= SUBLANES and b % SHORT_SEQ_BLOCK == 0
    return SUBLANES, 1, l, SHORT_SEQ_BLOCK


def _pad_seq(t, c, n):
    pad = c * n - t.shape[1]
    return t if pad == 0 else jnp.pad(t, ((0, 0), (0, pad), (0, 0)))


def _rows_layout(t, c, n, width):
    b = t.shape[0]
    return jnp.swapaxes(t[:, :, :width].reshape(b, n, c, width), 2, 3)


def _conv_window(scr, u, taps, c):
    scr[SUBLANES:SUBLANES + c, :] = u
    out = scr[5:5 + c, :] * taps[0:1, :]
    for j in range(1, CONV_W):
        out = out + scr[5 + j:5 + j + c, :] * taps[j:j + 1, :]
    return out


def _seq_spec(bb, c, width):
    return pl.BlockSpec((bb, c, width), lambda b, i: (b, i, 0))


def _state_spec(bb, shape):
    return pl.BlockSpec((bb,) + shape, lambda b, i: (b,) + (0,) * len(shape))


def _layer_state_spec(bb, shape, layer):
    return pl.BlockSpec((None, bb) + shape, lambda b, i: (layer, b) + (0,) * len(shape))


def _const_spec(shape):
    return pl.BlockSpec(shape, lambda b, i: (0,) * len(shape))


def _lru_kernel(xy_ref, buf_ref, h0_ref, cw_ref, cb_ref, wg_ref, bg_ref, lam_ref,
                out_ref, hnew_ref, bufnew_ref, xs_scr, a_scr, b_scr, h_scr, *, bb, c, nv, pos0):
    i = pl.program_id(1)
    w = LRU_W

    @pl.when(i == 0)
    def _():
        xs_scr[:, 0:SUBLANES, :] = jnp.zeros((bb, SUBLANES, w), F32)
        xs_scr[:, 5:8, :] = buf_ref[...]
        h_scr[...] = h0_ref[...]

    pos = lax.broadcasted_iota(jnp.int32, (c, 1), 0) + (i * c + pos0)
    for bi in range(bb):
        xc = _conv_window(xs_scr.at[bi], xy_ref[bi, :, 0:w], cw_ref[...], c) + cb_ref[...]
        gates = _dot(xc, wg_ref[...]) + bg_ref[...]
        r = _sigmoid(gates[:, 0:w])
        ig = _sigmoid(gates[:, w:2 * w])
        log_a = -LRU_C * r * _softplus(-lam_ref[...])
        mult = jnp.sqrt(-jnp.tanh(log_a) * (jnp.exp(2.0 * log_a) + 1.0))
        mult = jnp.where(pos == 0, 1.0, mult)
        a_scr[bi] = jnp.exp(log_a)
        b_scr[bi] = xc * ig * mult

    def step(t, h):
        h = a_scr[:, pl.ds(t, 1), :] * h + b_scr[:, pl.ds(t, 1), :]
        b_scr[:, pl.ds(t, 1), :] = h
        return h

    h = lax.fori_loop(0, nv, step, h_scr[...], unroll=min(nv, SUBLANES))
    h_scr[...] = h
    out_ref[...] = b_scr[...] * _gelu_tanh(xy_ref[:, :, w:2 * w])
    tail = xs_scr[:, 5 + nv:8 + nv, :]
    xs_scr[:, 5:8, :] = tail

    @pl.when(i == pl.num_programs(1) - 1)
    def _():
        hnew_ref[...] = h
        bufnew_ref[...] = tail


def _lru_mixer(xy, buf, h0, conv_w, conv_b, wr, br, wi, bi, lam, pos0):
    b, l, _ = xy.shape
    c, n, nv, bb = _seq_dims(b, l, LRU_SEQS_PER_STEP)
    w = LRU_W
    bd = lambda m: jax.scipy.linalg.block_diag(*[m[i] for i in range(LRU_BLOCKS)])
    wgate = jnp.concatenate([bd(wr), bd(wi)], axis=1).astype(BF16)
    bgate = jnp.concatenate([br, bi]).reshape(1, 2 * w)
    out, h_new, buf_new = pl.pallas_call(
        functools.partial(_lru_kernel, bb=bb, c=c, nv=nv, pos0=pos0),
        grid=(b // bb, n),
        in_specs=[_seq_spec(bb, c, 2 * w), _state_spec(bb, (CONV_W - 1, w)), _state_spec(bb, (1, w)),
                  _const_spec((CONV_W, w)), _const_spec((1, w)), _const_spec((w, 2 * w)),
                  _const_spec((1, 2 * w)), _const_spec((1, w))],
        out_specs=[_seq_spec(bb, c, w), _state_spec(bb, (1, w)), _state_spec(bb, (CONV_W - 1, w))],
        out_shape=[jax.ShapeDtypeStruct((b, n * c, w), F32),
                   jax.ShapeDtypeStruct((b, 1, w), F32),
                   jax.ShapeDtypeStruct((b, CONV_W - 1, w), F32)],
        scratch_shapes=[pltpu.VMEM((bb, SUBLANES + c, w), F32), pltpu.VMEM((bb, c, w), F32),
                        pltpu.VMEM((bb, c, w), F32), pltpu.VMEM((bb, 1, w), F32)],
        compiler_params=_cparams("parallel", "arbitrary"),
        name="rglru",
    )(_pad_seq(xy, c, n), buf, h0.reshape(b, 1, w), conv_w, conv_b.reshape(1, w), wgate, bgate,
      lam.reshape(1, w))
    return out[:, :l], h_new.reshape(b, w), buf_new


def _gdn_kernel(qkv_ref, z_ref, gcol_ref, grow_ref, buf_ref, s0_ref, cw_ref, pcol_ref, prow_ref,
                nw_ref, out_ref, snew_ref, bufnew_ref, xs_scr, s_scr, *, bb, c, nv):
    i = pl.program_id(1)
    qk = GDN_QK

    @pl.when(i == 0)
    def _():
        xs_scr[:, 0:SUBLANES, :] = jnp.zeros((bb, SUBLANES, GDN_QKV), F32)
        xs_scr[:, 5:8, :] = buf_ref[...]
        s_scr[...] = s0_ref[...]

    valid = _valid_rows(c, nv)
    valid_r = lax.broadcasted_iota(jnp.int32, (1, c), 1) < nv
    alog_c, dtb_c = pcol_ref[0:1, :], pcol_ref[1:2, :]
    alog_r, dtb_r = prow_ref[:, 0:1], prow_ref[:, 1:2]
    incl, strict = _tri(c, True), _tri(c, False)
    upper = jnp.logical_not(strict)
    heads = []
    for bi in range(bb):
        x = _silu(_conv_window(xs_scr.at[bi], qkv_ref[bi], cw_ref[...], c))
        gcol = gcol_ref[bi]
        grow = grow_ref[bi, 0]
        g_col = jnp.where(valid, -jnp.exp(alog_c) * _softplus(gcol[:, 0:GDN_H] + dtb_c), 0.0)
        beta = jnp.where(valid, _sigmoid(gcol[:, GDN_H:2 * GDN_H]), 0.0)
        g_row = jnp.where(valid_r, -jnp.exp(alog_r) * _softplus(grow[0:GDN_H, :] + dtb_r), 0.0)
        gi_cols = _dot_sel_lhs(incl, g_col)
        gi_rows = _dot_sel_rhs(g_row, upper)
        for h in range(GDN_H):
            q = _l2norm(x[:, h * GDN_DK:(h + 1) * GDN_DK]) * (GDN_DK ** -0.5)
            k = _l2norm(x[:, qk + h * GDN_DK:qk + (h + 1) * GDN_DK])
            gh = g_col[:, h:h + 1]
            gi = gi_cols[:, h:h + 1]
            gx = gi - gh
            gi_row = gi_rows[h:h + 1, :]
            g_end = gi[c - 1:c, :]
            kb = k * beta[:, h:h + 1]
            b = -jnp.exp(gh) * kb
            e_end = jnp.exp(g_end - gi)
            heads.append(dict(
                m_lhs=jnp.concatenate([k, q], axis=0), m_rhs=jnp.concatenate([b, kb], axis=0),
                pair_x=jnp.where(strict, jnp.exp(jnp.where(strict, gx - gi_row, 0.0)), 0.0),
                pair_i=jnp.where(incl, jnp.exp(jnp.where(incl, gi - gi_row, 0.0)), 0.0),
                x_lhs=jnp.concatenate([k * jnp.exp(gx), q * jnp.exp(gi)], axis=0),
                v=x[:, 2 * qk + h * GDN_DV:2 * qk + (h + 1) * GDN_DV],
                bk=jnp.concatenate([b * e_end, kb * e_end], axis=0), s_decay=jnp.exp(g_end)))
    pairs = [(bi, h) for bi in range(bb) for h in range(GDN_H)]
    outs, new_states = _dplr_heads(heads, [s_scr[bi, h] for bi, h in pairs], c, nv, state_is_vk=False)
    for (bi, h), o, s_new in zip(pairs, outs, new_states):
        s_scr[bi, h] = s_new
        zh = z_ref[bi, :, h * GDN_DV:(h + 1) * GDN_DV]
        o = o * lax.rsqrt(jnp.mean(o * o, axis=-1, keepdims=True) + 1e-6) * nw_ref[...]
        out_ref[bi, :, h * GDN_DV:(h + 1) * GDN_DV] = o * _silu(zh)
    tail = xs_scr[:, 5 + nv:8 + nv, :]
    xs_scr[:, 5:8, :] = tail

    @pl.when(i == pl.num_programs(1) - 1)
    def _():
        snew_ref[...] = s_scr[...]
        bufnew_ref[...] = tail


def _gdn_mixer(qkv, z, gates, buf, s0, layer, conv_w, a_log, dt_bias, norm_w):
    b, l, _ = qkv.shape
    c, n, nv, bb = _seq_dims(b, l, GDN_SEQS_PER_STEP)
    gates = _pad_seq(gates, c, n)
    pcol = jnp.stack([a_log, dt_bias])
    out, s_new, buf_new = pl.pallas_call(
        functools.partial(_gdn_kernel, bb=bb, c=c, nv=nv),
        grid=(b // bb, n),
        in_specs=[_seq_spec(bb, c, GDN_QKV), _seq_spec(bb, c, GDN_W), _seq_spec(bb, c, LANES),
                  pl.BlockSpec((bb, 1, SUBLANES, c), lambda b, i: (b, i, 0, 0)),
                  _state_spec(bb, (CONV_W - 1, GDN_QKV)),
                  _layer_state_spec(bb, (GDN_H, GDN_DK, GDN_DV), layer),
                  _const_spec((CONV_W, GDN_QKV)), _const_spec((2, GDN_H)), _const_spec((GDN_H, 2)),
                  _const_spec((1, GDN_DV))],
        out_specs=[_seq_spec(bb, c, GDN_W), _state_spec(bb, (GDN_H, GDN_DK, GDN_DV)),
                   _state_spec(bb, (CONV_W - 1, GDN_QKV))],
        out_shape=[jax.ShapeDtypeStruct((b, n * c, GDN_W), F32),
                   jax.ShapeDtypeStruct((b, GDN_H, GDN_DK, GDN_DV), F32),
                   jax.ShapeDtypeStruct((b, CONV_W - 1, GDN_QKV), F32)],
        scratch_shapes=[pltpu.VMEM((bb, SUBLANES + c, GDN_QKV), F32),
                        pltpu.VMEM((bb, GDN_H, GDN_DK, GDN_DV), F32)],
        compiler_params=_cparams("parallel", "arbitrary"),
        name="gdn",
    )(_pad_seq(qkv, c, n), _pad_seq(z, c, n), gates, _rows_layout(gates, c, n, SUBLANES), buf, s0,
      conv_w, pcol, pcol.T, norm_w.reshape(1, GDN_DV))
    return out[:, :l], s_new, buf_new


def _rwkv_kernel(rw_ref, prev_ref, s0_ref, mix_ref, w0_ref, w2_ref, a0_ref, a2_ref, g2_ref,
                 kk_ref, ka_ref, rk_ref, lnw_ref, lnb_ref, out_ref, snew_ref, xs_scr, s_scr,
                 *, bb, c, nv):
    i = pl.program_id(1)
    hd = RWKV_HD
    w = RWKV_W

    @pl.when(i == 0)
    def _():
        xs_scr[:, 0:SUBLANES, :] = jnp.zeros((bb, SUBLANES, RWKV_PROJ_W), F32)
        xs_scr[:, 7:8, :] = prev_ref[...]
        s_scr[...] = s0_ref[...]

    valid = _valid_rows(c, nv)
    incl = _tri(c, True)
    heads, post = [], []
    for bi in range(bb):
        rw = rw_ref[bi]
        xs_scr[bi, SUBLANES:SUBLANES + c, :] = rw
        prev = xs_scr[bi, 7:7 + c, :]
        xs_scr[bi, 7:8, :] = xs_scr[bi, 7 + nv:8 + nv, :]
        xs = rw + (prev - rw) * mix_ref[...]
        r_all, k_all, v_all = xs[:, 0:w], xs[:, w:2 * w], xs[:, 2 * w:3 * w]
        o1 = 3 * w
        wl = xs[:, o1:o1 + RWKV_RW]
        al = xs[:, o1 + RWKV_RW:o1 + RWKV_RW + RWKV_RA]
        gl = xs[:, o1 + RWKV_RW + RWKV_RA:]
        wdec = -_softplus(-(w0_ref[...] + _dot(jnp.tanh(wl), w2_ref[...]))) - 0.5
        a_all = _sigmoid(a0_ref[...] + _dot(al, a2_ref[...]))
        gate = _dot(_sigmoid(gl), g2_ref[...])
        lw_all = jnp.where(valid, -jnp.exp(wdec), 0.0)
        kmod = jnp.where(valid, k_all * (1.0 + (a_all - 1.0) * ka_ref[...]), 0.0)
        kk_all = k_all * kk_ref[...]
        gi = _dot_sel_lhs(incl, lw_all)
        gm = gi[c // 2:c // 2 + 1, :]
        g_end = gi[c - 1:c, :]
        e_nlw = jnp.exp(-lw_all)
        e_r = jnp.exp(gi - gm)
        e_a = e_r * e_nlw
        e_m = jnp.exp(gm - gi)
        e_gi = jnp.exp(gi)
        e_gx = e_gi * e_nlw
        e_end = jnp.exp(g_end - gm) * e_m
        s_dec = jnp.exp(g_end)
        for h in range(RWKV_H):
            sl = slice(h * hd, (h + 1) * hd)
            kk = _l2norm(kk_all[:, sl])
            r, k = r_all[:, sl], kmod[:, sl]
            a = -kk
            b = jnp.where(valid, kk * a_all[:, sl], 0.0)
            heads.append(dict(
                m_lhs=jnp.concatenate([a * e_a[:, sl], r * e_r[:, sl]], axis=0),
                m_rhs=jnp.concatenate([b * e_m[:, sl], k * e_m[:, sl]], axis=0),
                x_lhs=jnp.concatenate([a * e_gx[:, sl], r * e_gi[:, sl]], axis=0),
                v=v_all[:, sl],
                bk=jnp.concatenate([b * e_end[:, sl], k * e_end[:, sl]], axis=0), s_decay=s_dec[:, sl]))
            post.append((bi, h, r, k, v_all[:, sl], gate[:, sl]))
    outs, new_states = _dplr_heads(heads, [s_scr[bi, h] for bi, h, *_ in post], c, nv, state_is_vk=True)
    for (bi, h, r, k, v, gate_h), o, s_new in zip(post, outs, new_states):
        sl = slice(h * hd, (h + 1) * hd)
        s_scr[bi, h] = s_new
        y = _ln(o, lnw_ref[:, sl], lnb_ref[:, sl], RWKV_GN_EPS)
        y = y + jnp.sum(r * k * rk_ref[:, sl], axis=-1, keepdims=True) * v
        out_ref[bi, :, sl] = y * gate_h

    @pl.when(i == pl.num_programs(1) - 1)
    def _():
        snew_ref[...] = s_scr[...]


def _rwkv_mixer(rw, shift0, s0, layer, mix, w0, w2, a0, a2, g2, k_k, k_a, r_k, ln_w, ln_b):
    b, l, _ = rw.shape
    c, n, nv, bb = _seq_dims(b, l, RWKV_SEQS_PER_STEP)
    w = RWKV_W
    row = lambda t: t.reshape(1, -1)
    state = (RWKV_H, RWKV_HD, RWKV_HD)
    out, s_new = pl.pallas_call(
        functools.partial(_rwkv_kernel, bb=bb, c=c, nv=nv),
        grid=(b // bb, n),
        in_specs=[_seq_spec(bb, c, RWKV_PROJ_W), _state_spec(bb, (1, RWKV_PROJ_W)),
                  _layer_state_spec(bb, state, layer),
                  _const_spec((1, RWKV_PROJ_W)), _const_spec((1, w)), _const_spec((RWKV_RW, w)),
                  _const_spec((1, w)), _const_spec((RWKV_RA, w)), _const_spec((RWKV_RG, w)),
                  _const_spec((1, w)), _const_spec((1, w)), _const_spec((1, w)), _const_spec((1, w)),
                  _const_spec((1, w))],
        out_specs=[_seq_spec(bb, c, w), _state_spec(bb, state)],
        out_shape=[jax.ShapeDtypeStruct((b, n * c, w), F32),
                   jax.ShapeDtypeStruct((b,) + state, F32)],
        scratch_shapes=[pltpu.VMEM((bb, SUBLANES + c, RWKV_PROJ_W), F32), pltpu.VMEM((bb,) + state, F32)],
        compiler_params=_cparams("parallel", "arbitrary"),
        name="rwkv7",
    )(_pad_seq(rw, c, n), shift0.reshape(b, 1, RWKV_PROJ_W), s0, row(mix), row(w0),
      w2.astype(BF16), row(a0), a2.astype(BF16), g2.astype(BF16), row(k_k), row(k_a), row(r_k),
      row(ln_w), row(ln_b))
    return out[:, :l], s_new


def _mlstm_kernel(p_ref, gcol_ref, grow_ref, c0_ref, n0_ref, m0_ref, bcol_ref, brow_ref, nw_ref,
                  out_ref, cnew_ref, nnew_ref, mnew_ref, c_scr, n_scr, m_scr, *, bb, c, nv):
    i = pl.program_id(1)
    nh, dk, dv = MLSTM_H, MLSTM_DK, MLSTM_DV

    @pl.when(i == 0)
    def _():
        c_scr[...] = c0_ref[...]
        n_scr[...] = n0_ref[...]
        m_scr[...] = m0_ref[...]

    valid = _valid_rows(c, nv)
    valid_r = lax.broadcasted_iota(jnp.int32, (1, c), 1) < nv
    incl = _tri(c, True)
    upper = jnp.logical_not(_tri(c, False))
    ps = [(bi, h) for bi in range(bb) for h in range(nh)]
    gate_cols = {}
    for bi in range(bb):
        gcol = gcol_ref[bi]
        grow = grow_ref[bi, 0]
        li_col = jnp.where(valid, gcol[:, 0:nh] + bcol_ref[0:1, :], NEG_BIG)
        lf_col = jnp.where(valid, -_softplus(-(gcol[:, nh:2 * nh] + bcol_ref[1:2, :])), 0.0)
        li_row = jnp.where(valid_r, grow[0:nh, :] + brow_ref[:, 0:1], NEG_BIG)
        lf_row = jnp.where(valid_r, -_softplus(-(grow[nh:2 * nh, :] + brow_ref[:, 1:2])), 0.0)
        b_cols = _dot_sel_lhs(incl, lf_col)
        b_rows = _dot_sel_rhs(lf_row, upper)
        gate_cols[bi] = (li_col, li_row, b_cols, b_rows)
    qs = [p_ref[bi, :, h * dk:(h + 1) * dk] for bi, h in ps]
    ks = [p_ref[bi, :, nh * dk + h * dk:nh * dk + (h + 1) * dk] * (dk ** -0.5) for bi, h in ps]
    vs = [p_ref[bi, :, 2 * nh * dk + h * dv:2 * nh * dk + (h + 1) * dv] for bi, h in ps]
    cms = [c_scr[bi, h] for bi, h in ps]
    nvecs = [n_scr[bi, h] for bi, h in ps]
    m_prevs = [m_scr[bi, h] for bi, h in ps]
    qks = [_dot_nt(q, k) for q, k in zip(qs, ks)]
    qcs = [_dot_nt(q, cm) for q, cm in zip(qs, cms)]
    bcs = [gate_cols[bi][2][:, h:h + 1] for bi, h in ps]
    dms = [jnp.where(incl, bc - gate_cols[bi][3][h:h + 1, :] + gate_cols[bi][1][h:h + 1, :], -jnp.inf)
           for (bi, h), bc in zip(ps, bcs)]
    m_inters = [bc + m_prev for bc, m_prev in zip(bcs, m_prevs)]
    m_ts = [jnp.maximum(mi, jnp.max(dm, axis=-1, keepdims=True)) for mi, dm in zip(m_inters, dms)]
    w_inters = [jnp.exp(mi - mt) for mi, mt in zip(m_inters, m_ts)]
    scs = [qk * jnp.exp(dm - mt) for qk, dm, mt in zip(qks, dms, m_ts)]
    scvs = [_dot(sc, v) for sc, v in zip(scs, vs)]
    m_news = [mt[c - 1:c, :] for mt in m_ts]
    b_lasts = [bc[c - 1:c, :] for bc in bcs]
    wss = [jnp.exp(b_last - bc + gate_cols[bi][0][:, h:h + 1] - m_new)
           for (bi, h), b_last, bc, m_new in zip(ps, b_lasts, bcs, m_news)]
    upds = [_dot_tn(v * ws, k) for v, ws, k in zip(vs, wss, ks)]
    for j, (bi, h) in enumerate(ps):
        num = w_inters[j] * qcs[j] + scvs[j]
        den = (w_inters[j] * jnp.sum(qs[j] * nvecs[j], axis=-1, keepdims=True)
               + jnp.sum(scs[j], axis=-1, keepdims=True))
        hh = num / jnp.maximum(jnp.abs(den), jnp.exp(-m_ts[j]))
        dec = jnp.exp(b_lasts[j] + m_prevs[j] - m_news[j])
        c_scr[bi, h] = dec * cms[j] + upds[j]
        n_scr[bi, h] = dec * nvecs[j] + jnp.sum(ks[j] * wss[j], axis=0, keepdims=True)
        m_scr[bi, h] = m_news[j]
        og = p_ref[bi, :, 2 * nh * dk + nh * dv + h * dv:2 * nh * dk + nh * dv + (h + 1) * dv]
        out_ref[bi, :, h * dv:(h + 1) * dv] = _ln(hh, nw_ref[...]) * _sigmoid(og)

    @pl.when(i == pl.num_programs(1) - 1)
    def _():
        cnew_ref[...] = c_scr[...]
        nnew_ref[...] = n_scr[...]
        mnew_ref[...] = m_scr[...]


def _mlstm_mixer(p, gates, c0, layer, n0, m0, i_b, f_b, norm_w):
    b, l, _ = p.shape
    c, n, nv, bb = _seq_dims(b, l, MLSTM_SEQS_PER_STEP)
    nh, dk, dv = MLSTM_H, MLSTM_DK, MLSTM_DV
    gates = _pad_seq(gates, c, n)
    bcol = jnp.stack([i_b, f_b])
    out, c_new, n_new, m_new = pl.pallas_call(
        functools.partial(_mlstm_kernel, bb=bb, c=c, nv=nv),
        grid=(b // bb, n),
        in_specs=[_seq_spec(bb, c, p.shape[2]), _seq_spec(bb, c, LANES),
                  pl.BlockSpec((bb, 1, SUBLANES, c), lambda b, i: (b, i, 0, 0)),
                  _layer_state_spec(bb, (nh, dv, dk), layer), _state_spec(bb, (nh, 1, dk)),
                  _state_spec(bb, (nh, 1, 1)), _const_spec((2, nh)), _const_spec((nh, 2)), _const_spec((1, dv))],
        out_specs=[_seq_spec(bb, c, nh * dv), _state_spec(bb, (nh, dv, dk)),
                   _state_spec(bb, (nh, 1, dk)), _state_spec(bb, (nh, 1, 1))],
        out_shape=[jax.ShapeDtypeStruct((b, n * c, nh * dv), F32),
                   jax.ShapeDtypeStruct((b, nh, dv, dk), F32),
                   jax.ShapeDtypeStruct((b, nh, 1, dk), F32),
                   jax.ShapeDtypeStruct((b, nh, 1, 1), F32)],
        scratch_shapes=[pltpu.VMEM((bb, nh, dv, dk), F32), pltpu.VMEM((bb, nh, 1, dk), F32),
                        pltpu.VMEM((bb, nh, 1, 1), F32)],
        compiler_params=_cparams("parallel", "arbitrary"),
        name="mlstm",
    )(_pad_seq(p, c, n), gates, _rows_layout(gates, c, n, SUBLANES), c0,
      n0.reshape(b, nh, 1, dk), m0.reshape(b, nh, 1, 1), bcol, bcol.T, norm_w.reshape(1, dv))
    return out[:, :l], c_new, n_new.reshape(b, nh, dk), m_new.reshape(b, nh)


def _pad_cols(w, n):
    return jnp.pad(w, ((0, 0), (0, n - w.shape[1])))


def _ab_in_weight(w):
    o = 2 * LRU_W
    xy, qkv = w[:, :o], w[:, o:o + GDN_QKV]
    o += GDN_QKV
    ab, z = w[:, o:o + 2 * GDN_H], w[:, o + 2 * GDN_H:]
    return jnp.concatenate([xy, qkv, z, _pad_cols(ab, LANES)], axis=1).astype(BF16)


AB_SPLITS = ((0, 2 * LRU_W), (2 * LRU_W, GDN_QKV), (2 * LRU_W + GDN_QKV, GDN_W),
             (2 * LRU_W + GDN_QKV + GDN_W, LANES))


def _cd_in_weight(w):
    o = RWKV_PROJ_W
    rw, qkv = w[:, :o], w[:, o:o + 2 * MLSTM_QK + MLSTM_W]
    o += 2 * MLSTM_QK + MLSTM_W
    gates, og = w[:, o:o + 2 * MLSTM_H], w[:, o + 2 * MLSTM_H:]
    return jnp.concatenate([rw, qkv, og, _pad_cols(gates, LANES)], axis=1).astype(BF16)


CD_SPLITS = ((0, RWKV_PROJ_W), (RWKV_PROJ_W, 2 * MLSTM_QK + 2 * MLSTM_W),
             (RWKV_PROJ_W + 2 * MLSTM_QK + 2 * MLSTM_W, LANES))


def kernel(x_prompt, x_sample, c_prompt, c_sample,
           state_lru_h, state_lru_conv, state_gdn_S, state_gdn_conv,
           state_rwkv_S, state_rwkv_shift, state_mlstm_C, state_mlstm_n, state_mlstm_m,
           mod_w, mod_b, ln1_g, ln1_b, ln2_g, ln2_b,
           ab_w_in, ab_w_out, lru_conv_w, lru_conv_b, lru_wr, lru_br, lru_wi, lru_bi, lru_lambda,
           gdn_conv_w, gdn_a_log, gdn_dt_bias, gdn_norm_w,
           cd_w_in, cd_w_out, rwkv_mix, rwkv_w0, rwkv_w2, rwkv_a0, rwkv_a2, rwkv_g2,
           rwkv_k_k, rwkv_k_a, rwkv_r_k, rwkv_ln_w, rwkv_ln_b,
           mlstm_i_b, mlstm_f_b, mlstm_norm_w,
           ffn_w_gate, ffn_w_up, ffn_w_down,
           router_w, moe_w_gate, moe_w_up, moe_w_down):
    d = D_MODEL
    bp, lp, _ = x_prompt.shape
    bs, ls, _ = x_sample.shape
    mod = _modulation(jnp.concatenate([c_prompt, c_sample], axis=0), mod_w, mod_b)

    ab_in = [_ab_in_weight(ab_w_in[j]) for j in range(ab_w_in.shape[0])]
    cd_in = [_cd_in_weight(cd_w_in[j]) for j in range(cd_w_in.shape[0])]
    ab_out, cd_out = ab_w_out.astype(BF16), cd_w_out.astype(BF16)
    ffn_g, ffn_u, ffn_d = (t.astype(BF16) for t in (ffn_w_gate, ffn_w_up, ffn_w_down))

    def trunk(x, mods, batch, length, states, pos0):
        lru_h, lru_conv, gdn_s, gdn_conv, rwkv_s, rwkv_shift, m_c, m_n, m_m = states
        new = [[] for _ in range(9)]
        seq = lambda t: t.reshape(batch, length, t.shape[-1])
        tok = lambda t: t.reshape(x.shape[0], x.shape[1], t.shape[-1])
        for l in range(DEPTH):
            j = l // 2
            sh1, sc1, g1, sh2, sc2, g2 = mods[l]
            if l % 2 == 0:
                xy, qkv, z, gates = _inproj(x, sc1, sh1, ab_in[j], AB_SPLITS)
                out_a, s0, s1 = _lru_mixer(seq(xy), lru_conv[j], lru_h[j], lru_conv_w[j], lru_conv_b[j],
                                           lru_wr[j], lru_br[j], lru_wi[j], lru_bi[j], lru_lambda[j], pos0)
                out_b, s2, s3 = _gdn_mixer(seq(qkv), seq(z), seq(gates), gdn_conv[j], gdn_s, j,
                                           gdn_conv_w[j], gdn_a_log[j], gdn_dt_bias[j], gdn_norm_w[j])
                for slot, s in zip((0, 1, 2, 3), (s0, s1, s2, s3)):
                    new[slot].append(s)
                x = _outproj_ln(x, tok(out_a), tok(out_b), g1, ab_out, j, ln1_g[l], ln1_b[l])
                x = _ffn_ln(x, sc2, sh2, g2, ffn_g, ffn_u, ffn_d, j, ln2_g[l], ln2_b[l])
            else:
                rw, mp, gates = _inproj(x, sc1, sh1, cd_in[j], CD_SPLITS)
                out_c, s0 = _rwkv_mixer(seq(rw), rwkv_shift[j], rwkv_s, j, rwkv_mix[j], rwkv_w0[j],
                                        rwkv_w2[j], rwkv_a0[j], rwkv_a2[j], rwkv_g2[j], rwkv_k_k[j],
                                        rwkv_k_a[j], rwkv_r_k[j], rwkv_ln_w[j], rwkv_ln_b[j])
                out_d, s2, s3, s4 = _mlstm_mixer(seq(mp), seq(gates), m_c, j, m_n[j], m_m[j],
                                                 mlstm_i_b[j], mlstm_f_b[j], mlstm_norm_w[j])
                for slot, s in zip((4, 5, 6, 7, 8), (s0, seq(rw)[:, -1], s2, s3, s4)):
                    new[slot].append(s)
                x = _outproj_ln(x, tok(out_c), tok(out_d), g1, cd_out, j, ln1_g[l], ln1_b[l])
                moe = _moe_sparse_ln if x.shape[1] % MOE_TOKEN_BLOCK == 0 else _moe_ln
                x = moe(x, sc2, sh2, g2, router_w[j], moe_w_gate, moe_w_up, moe_w_down, j,
                        ln2_g[l], ln2_b[l])
        return x, tuple(jnp.stack(s) for s in new)

    def zeros(ref):
        return jnp.zeros((ref.shape[0], bp) + ref.shape[2:], F32)

    mods_p = [[mod[l, :bp, k * d:(k + 1) * d].reshape(bp, 1, d) for k in range(6)] for l in range(DEPTH)]
    mods_s = [[mod[l, bp:, k * d:(k + 1) * d].reshape(1, bs * ls, d) for k in range(6)] for l in range(DEPTH)]
    states_s = (state_lru_h, state_lru_conv, state_gdn_S, state_gdn_conv, state_rwkv_S,
                state_rwkv_shift, state_mlstm_C, state_mlstm_n, state_mlstm_m)
    y_p, new_p = trunk(x_prompt, mods_p, bp, lp, tuple(zeros(s) for s in states_s), 0)
    y_s, new_s = trunk(x_sample.reshape(1, bs * ls, d), mods_s, bs, ls, states_s, PAST_LEN)
    out = [y_p, y_s.reshape(bs, ls, d)]
    for p_leaf, s_leaf in zip(new_p, new_s):
        out += [p_leaf, s_leaf]
    return tuple(out)
```

```python
import functools
import math

import jax
import jax.numpy as jnp
from jax import lax
from jax.experimental import pallas as pl
from jax.experimental.pallas import tpu as pltpu

F32 = jnp.float32
BF16 = jnp.bfloat16

D_MODEL = 1024
DEPTH = 4
PAST_LEN = 16384
CONV_W = 4
LRU_W = D_MODEL // 2
LRU_BLOCKS = 8
LRU_BW = LRU_W // LRU_BLOCKS
LRU_C = 8.0
GDN_H = D_MODEL // 256
GDN_DK = 128
GDN_DV = 128
GDN_QK = GDN_H * GDN_DK
GDN_W = GDN_H * GDN_DV
GDN_QKV = 2 * GDN_QK + GDN_W
RWKV_HD = 64
RWKV_H = D_MODEL // 2 // RWKV_HD
RWKV_W = RWKV_H * RWKV_HD
RWKV_RW = 64
RWKV_RA = 64
RWKV_RG = 128
RWKV_PROJ_W = 3 * RWKV_W + RWKV_RW + RWKV_RA + RWKV_RG
RWKV_GN_EPS = 64e-5
MLSTM_H = D_MODEL // 256
MLSTM_DK = 128
MLSTM_DV = 128
MLSTM_QK = MLSTM_H * MLSTM_DK
MLSTM_W = MLSTM_H * MLSTM_DV
D_FF = 7 * D_MODEL // 2
N_EXPERTS = 8
LN_EPS = 1e-5
NEG_BIG = -1e30
DN_ALPHA = (2.0 * DEPTH) ** 0.25

LANES = 128
SUBLANES = 8
SEQ_CHUNK = 64
SHORT_SEQ_BLOCK = 8
LRU_SEQS_PER_STEP = 8
GDN_SEQS_PER_STEP = 4
RWKV_SEQS_PER_STEP = 2
MLSTM_SEQS_PER_STEP = 1
ROW_TILE = 512
FFN_ROW_TILE = 1024
FF_TILE = 512
MOE_TOKEN_BLOCK = 512
MOE_SLOT_CHUNK = 512
MOE_FF_SPLIT = 4
MOE_WINDOW = 256
MOE_WINDOWS = 3
VMEM_LIMIT = 48 * 1024 * 1024


def _cparams(*sem):
    return pltpu.CompilerParams(dimension_semantics=sem, vmem_limit_bytes=VMEM_LIMIT)


def _dot(a, b):
    return jnp.dot(a.astype(BF16), b.astype(BF16), preferred_element_type=F32)


def _dot_nt(a, b):
    return lax.dot_general(a.astype(BF16), b.astype(BF16), (((1,), (1,)), ((), ())),
                           preferred_element_type=F32)


def _dot_tn(a, b):
    return lax.dot_general(a.astype(BF16), b.astype(BF16), (((0,), (0,)), ((), ())),
                           preferred_element_type=F32)


def _split3(x):
    hi = x.astype(BF16)
    r1 = x - hi.astype(F32)
    mid = r1.astype(BF16)
    lo = (r1 - mid.astype(F32)).astype(BF16)
    return hi, mid, lo


def _dot_sel_lhs(t, x):
    tb = jnp.where(t, 1.0, 0.0).astype(BF16)
    hi, mid, lo = _split3(x)
    d = lambda p: jnp.dot(tb, p, preferred_element_type=F32)
    return d(hi) + d(mid) + d(lo)


def _dot_sel_rhs(x, t):
    tb = jnp.where(t, 1.0, 0.0).astype(BF16)
    hi, mid, lo = _split3(x)
    d = lambda p: jnp.dot(p, tb, preferred_element_type=F32)
    return d(hi) + d(mid) + d(lo)


def _dot2(p, x):
    pb = p.astype(BF16)
    xh = x.astype(BF16)
    xl = (x - xh.astype(F32)).astype(BF16)
    return (jnp.dot(pb, xh, preferred_element_type=F32)
            + jnp.dot(pb, xl, preferred_element_type=F32))


def _sigmoid(x):
    return 1.0 / (1.0 + jnp.exp(-x))


def _silu(x):
    return x * _sigmoid(x)


def _softplus(x):
    return jnp.maximum(x, 0.0) + jnp.log1p(jnp.exp(-jnp.abs(x)))


def _gelu_tanh(x):
    return 0.5 * x * (1.0 + jnp.tanh(math.sqrt(2.0 / math.pi) * (x + 0.044715 * (x * x * x))))


def _ln(y, g=None, b=None, eps=LN_EPS):
    mu = jnp.mean(y, axis=-1, keepdims=True)
    d = y - mu
    var = jnp.mean(d * d, axis=-1, keepdims=True)
    out = d * lax.rsqrt(var + eps)
    if g is not None:
        out = out * g
    if b is not None:
        out = out + b
    return out


def _l2norm(x, eps=1e-6):
    return x * lax.rsqrt(jnp.sum(x * x, axis=-1, keepdims=True) + eps)


def _tri(c, inclusive):
    t = lax.broadcasted_iota(jnp.int32, (c, c), 0)
    s = lax.broadcasted_iota(jnp.int32, (c, c), 1)
    return (s <= t) if inclusive else (s < t)


def _unit_lower_solve(ns, xs, c):
    steps = max(1, int(math.ceil(math.log2(c))))
    for i in range(steps):
        xs = [x + _dot2(p, x) for p, x in zip(ns, xs)]
        if i + 1 < steps:
            ns = [_dot(p, p) for p in ns]
    return xs


def _dplr_heads(heads, states, c, nv, state_is_vk):
    incl = _tri(c, True)
    strict = _tri(c, False)
    ms = [_dot_nt(h["m_lhs"], h["m_rhs"]) for h in heads]
    if state_is_vk:
        xhs = [_dot_nt(h["x_lhs"], s) for h, s in zip(heads, states)]
    else:
        xhs = [_dot(h["x_lhs"], s) for h, s in zip(heads, states)]
    a_abs, a_aks, r_bks = [], [], []
    for h, m in zip(heads, ms):
        if "pair_x" in h:
            a_abs.append(m[:c, :c] * h["pair_x"])
            a_aks.append(m[:c, c:] * h["pair_x"])
            r_bks.append(jnp.concatenate([m[c:, :c] * h["pair_i"], m[c:, c:] * h["pair_i"]], axis=1))
        else:
            a_abs.append(jnp.where(strict, m[:c, :c], 0.0))
            a_aks.append(jnp.where(strict, m[:c, c:], 0.0))
            r_bks.append(jnp.concatenate([jnp.where(incl, m[c:, :c], 0.0),
                                          jnp.where(incl, m[c:, c:], 0.0)], axis=1))
    if nv == 1:
        us = [xh[:c] for xh in xhs]
    else:
        rhs = [xh[:c] + _dot(a_ak, h["v"]) for xh, a_ak, h in zip(xhs, a_aks, heads)]
        us = _unit_lower_solve(a_abs, rhs, c)
    uvs = [jnp.concatenate([u, h["v"]], axis=0) for u, h in zip(us, heads)]
    outs = [xh[c:] + _dot(r_bk, uv) for xh, r_bk, uv in zip(xhs, r_bks, uvs)]
    if state_is_vk:
        new = [s * h["s_decay"] + _dot_tn(uv, h["bk"]) for s, h, uv in zip(states, heads, uvs)]
    else:
        new = [s * h["s_decay"] + _dot_tn(h["bk"], uv) for s, h, uv in zip(states, heads, uvs)]
    return outs, new


def _valid_rows(c, n_valid):
    return lax.broadcasted_iota(jnp.int32, (c, 1), 0) < n_valid


def _mod_kernel(c_ref, w_ref, b_ref, o_ref):
    o_ref[0] = _dot(_silu(c_ref[...]), w_ref[0]) + b_ref[0]


def _modulation(c_all, mod_w, mod_b):
    n = c_all.shape[0]
    d = D_MODEL
    return pl.pallas_call(
        _mod_kernel,
        grid=(DEPTH, 6),
        in_specs=[pl.BlockSpec((n, d), lambda l, j: (0, 0)),
                  pl.BlockSpec((1, d, d), lambda l, j: (l, 0, j)),
                  pl.BlockSpec((1, 1, d), lambda l, j: (l, 0, j))],
        out_specs=pl.BlockSpec((1, n, d), lambda l, j: (l, 0, j)),
        out_shape=jax.ShapeDtypeStruct((DEPTH, n, 6 * d), F32),
        compiler_params=_cparams("parallel", "parallel"),
        name="modulation",
    )(c_all, mod_w, mod_b.reshape(DEPTH, 1, 6 * d))


def _mod_spec(mod, tm):
    if mod.shape[1] == 1:
        return pl.BlockSpec((1, 1, mod.shape[2]), lambda g, i, *_: (g, 0, 0))
    return pl.BlockSpec((1, tm, mod.shape[2]), lambda g, i, *_: (g, i, 0))


def _inproj_kernel(x_ref, sc_ref, sh_ref, w_ref, *o_refs, splits):
    h = (x_ref[0] * (1.0 + sc_ref[0]) + sh_ref[0]).astype(BF16)
    for o_ref, (s, n) in zip(o_refs, splits):
        o_ref[0] = jnp.dot(h, w_ref[:, s:s + n], preferred_element_type=F32)


def _inproj(x, sc, sh, w, splits):
    g, r, d = x.shape
    tm = min(r, ROW_TILE)
    n_all = w.shape[1]
    return pl.pallas_call(
        functools.partial(_inproj_kernel, splits=splits),
        grid=(g, r // tm),
        in_specs=[pl.BlockSpec((1, tm, d), lambda g, i: (g, i, 0)),
                  _mod_spec(sc, tm), _mod_spec(sh, tm),
                  pl.BlockSpec((d, n_all), lambda g, i: (0, 0))],
        out_specs=[pl.BlockSpec((1, tm, n), lambda g, i: (g, i, 0)) for _, n in splits],
        out_shape=[jax.ShapeDtypeStruct((g, r, n), F32) for _, n in splits],
        compiler_params=_cparams("parallel", "parallel"),
        name="inproj",
    )(x, sc, sh, w)


def _outproj_ln_kernel(x_ref, ma_ref, mb_ref, gate_ref, w_ref, lng_ref, lnb_ref, o_ref):
    half = ma_ref.shape[2]
    f = _dot(ma_ref[0], w_ref[0:half, :]) + _dot(mb_ref[0], w_ref[half:, :])
    y = DN_ALPHA * x_ref[0] + (1.0 + gate_ref[0]) * f
    o_ref[0] = _ln(y, lng_ref[...], lnb_ref[...])


def _outproj_ln(x, mix_a, mix_b, gate, w, layer, ln_g, ln_b):
    g, r, d = x.shape
    tm = min(r, ROW_TILE)
    half = mix_a.shape[2]
    row = pl.BlockSpec((1, tm, d), lambda g, i: (g, i, 0))
    mrow = pl.BlockSpec((1, tm, half), lambda g, i: (g, i, 0))
    vec = pl.BlockSpec((1, d), lambda g, i: (0, 0))
    return pl.pallas_call(
        _outproj_ln_kernel,
        grid=(g, r // tm),
        in_specs=[row, mrow, mrow, _mod_spec(gate, tm),
                  pl.BlockSpec((None, 2 * half, d), lambda g, i: (layer, 0, 0)), vec, vec],
        out_specs=row,
        out_shape=jax.ShapeDtypeStruct((g, r, d), F32),
        compiler_params=_cparams("parallel", "parallel"),
        name="outproj_ln",
    )(x, mix_a, mix_b, gate, w, ln_g.reshape(1, d), ln_b.reshape(1, d))


def _ffn_kernel(x_ref, sc_ref, sh_ref, gate_ref, wg_ref, wu_ref, wd_ref, lng_ref, lnb_ref,
                o_ref, h_scr, acc_scr):
    j = pl.program_id(2)

    @pl.when(j == 0)
    def _():
        h_scr[...] = (x_ref[0] * (1.0 + sc_ref[0]) + sh_ref[0]).astype(BF16)
        acc_scr[...] = jnp.zeros_like(acc_scr)

    h = h_scr[...]
    a = jnp.dot(h, wg_ref[...], preferred_element_type=F32)
    u = jnp.dot(h, wu_ref[...], preferred_element_type=F32)
    acc_scr[...] += _dot(_silu(a) * u, wd_ref[...])

    @pl.when(j == pl.num_programs(2) - 1)
    def _():
        y = DN_ALPHA * x_ref[0] + (1.0 + gate_ref[0]) * acc_scr[...]
        o_ref[0] = _ln(y, lng_ref[...], lnb_ref[...])


def _ffn_ln(x, sc, sh, gate, wg, wu, wd, layer, ln_g, ln_b):
    g, r, d = x.shape
    tm = min(r, FFN_ROW_TILE)
    f = wg.shape[2]
    row = pl.BlockSpec((1, tm, d), lambda g, i, j: (g, i, 0))
    vec = pl.BlockSpec((1, d), lambda g, i, j: (0, 0))
    return pl.pallas_call(
        _ffn_kernel,
        grid=(g, r // tm, f // FF_TILE),
        in_specs=[row, _mod_spec(sc, tm), _mod_spec(sh, tm), _mod_spec(gate, tm),
                  pl.BlockSpec((None, d, FF_TILE), lambda g, i, j: (layer, 0, j)),
                  pl.BlockSpec((None, d, FF_TILE), lambda g, i, j: (layer, 0, j)),
                  pl.BlockSpec((None, FF_TILE, d), lambda g, i, j: (layer, j, 0)), vec, vec],
        out_specs=row,
        out_shape=jax.ShapeDtypeStruct((g, r, d), F32),
        scratch_shapes=[pltpu.VMEM((tm, d), BF16), pltpu.VMEM((tm, d), F32)],
        compiler_params=_cparams("parallel", "parallel", "arbitrary"),
        name="ffn_ln",
    )(x, sc, sh, gate, wg, wu, wd, ln_g.reshape(1, d), ln_b.reshape(1, d))


def _top2_route(h, rw, lane):
    logits = jnp.dot(h, rw, preferred_element_type=F32, precision=lax.Precision.HIGHEST)
    logits = jnp.where(lane < N_EXPERTS, logits, -jnp.inf)
    m1 = jnp.max(logits, axis=-1, keepdims=True)
    i1 = jnp.min(jnp.where(logits == m1, lane, LANES), axis=-1, keepdims=True)
    rest = jnp.where(lane == i1, -jnp.inf, logits)
    m2 = jnp.max(rest, axis=-1, keepdims=True)
    i2 = jnp.min(jnp.where(rest == m2, lane, LANES), axis=-1, keepdims=True)
    e2 = jnp.exp(m2 - m1)
    g1 = 1.0 / (1.0 + e2)
    g2 = e2 / (1.0 + e2)
    sel = (lane == i1) | (lane == i2)
    return sel, jnp.where(lane == i1, g1, 0.0) + jnp.where(lane == i2, g2, 0.0)


def _moe_kernel(x_ref, sc_ref, sh_ref, gate_ref, rw_ref, wg_ref, wu_ref, wd_ref, lng_ref, lnb_ref,
                o_ref, h_scr, comb_scr, acc_scr):
    e = pl.program_id(2)
    j = pl.program_id(3)
    lane = lax.broadcasted_iota(jnp.int32, comb_scr.shape, 1)

    @pl.when((e == 0) & (j == 0))
    def _():
        h = x_ref[0] * (1.0 + sc_ref[0]) + sh_ref[0]
        h_scr[...] = h.astype(BF16)
        acc_scr[...] = jnp.zeros_like(acc_scr)
        _, comb_scr[...] = _top2_route(h, rw_ref[...], lane)

    h = h_scr[...]
    a = _dot(h, wg_ref[0])
    u = _dot(h, wu_ref[0])
    comb_e = jnp.sum(jnp.where(lane == e, comb_scr[...], 0.0), axis=-1, keepdims=True)
    acc_scr[...] += comb_e * _dot(_silu(a) * u, wd_ref[0])

    @pl.when((e == pl.num_programs(2) - 1) & (j == pl.num_programs(3) - 1))
    def _():
        y = DN_ALPHA * x_ref[0] + (1.0 + gate_ref[0]) * acc_scr[...]
        o_ref[0] = _ln(y, lng_ref[...], lnb_ref[...])


def _moe_ln(x, sc, sh, gate, router_w, wg, wu, wd, layer, ln_g, ln_b):
    g, r, d = x.shape
    tm = min(r, ROW_TILE)
    _, ne, _, f = wg.shape
    row = pl.BlockSpec((1, tm, d), lambda g, i, e, j: (g, i, 0))
    vec = pl.BlockSpec((1, d), lambda g, i, e, j: (0, 0))
    rw = jnp.pad(router_w, ((0, 0), (0, LANES - ne)))
    return pl.pallas_call(
        _moe_kernel,
        grid=(g, r // tm, ne, f // FF_TILE),
        in_specs=[row, _mod_spec(sc, tm), _mod_spec(sh, tm), _mod_spec(gate, tm),
                  pl.BlockSpec((d, LANES), lambda g, i, e, j: (0, 0)),
                  pl.BlockSpec((None, 1, d, FF_TILE), lambda g, i, e, j: (layer, e, 0, j)),
                  pl.BlockSpec((None, 1, d, FF_TILE), lambda g, i, e, j: (layer, e, 0, j)),
                  pl.BlockSpec((None, 1, FF_TILE, d), lambda g, i, e, j: (layer, e, j, 0)), vec, vec],
        out_specs=row,
        out_shape=jax.ShapeDtypeStruct((g, r, d), F32),
        scratch_shapes=[pltpu.VMEM((tm, d), BF16), pltpu.VMEM((tm, LANES), F32),
                        pltpu.VMEM((tm, d), F32)],
        compiler_params=_cparams("parallel", "parallel", "arbitrary", "arbitrary"),
        name="moe_ln",
    )(x, sc, sh, gate, rw, wg, wu, wd, ln_g.reshape(1, d), ln_b.reshape(1, d))


def _moe_route_kernel(x_ref, sc_ref, sh_ref, rw_ref, hb_ref, comb_ref, rank_ref, rankt_ref, cnt_ref,
                      run_scr):
    @pl.when(pl.program_id(0) == 0)
    def _():
        run_scr[...] = jnp.zeros_like(run_scr)

    h = x_ref[0] * (1.0 + sc_ref[0]) + sh_ref[0]
    hb_ref[0] = h.astype(BF16)
    tb = h.shape[0]
    lane = lax.broadcasted_iota(jnp.int32, (tb, LANES), 1)
    sel, comb = _top2_route(h, rw_ref[...], lane)
    comb_ref[0] = comb
    ones = jnp.where(sel, 1.0, 0.0)
    before = jnp.dot(jnp.where(_tri(tb, False), 1.0, 0.0).astype(BF16), ones.astype(BF16),
                     preferred_element_type=F32)
    rank = jnp.where(sel, before + run_scr[...], -1.0)
    rank_ref[0] = rank
    rankt_ref[0] = rank.T[0:SUBLANES, :]
    cnt = jnp.sum(ones, axis=0, keepdims=True)
    cnt_ref[0] = cnt
    run_scr[...] += cnt


def _moe_route(xb, sc, sh, router_w, blocks_per_seq):
    nb, tb, d = xb.shape
    seq = lambda j: (j // blocks_per_seq, 0, 0)
    blk = lambda width: pl.BlockSpec((1, tb, width), lambda j: (j, 0, 0))
    return pl.pallas_call(
        _moe_route_kernel,
        grid=(nb,),
        in_specs=[blk(d), pl.BlockSpec((1, 1, d), seq), pl.BlockSpec((1, 1, d), seq),
                  pl.BlockSpec((d, LANES), lambda j: (0, 0))],
        out_specs=[blk(d), blk(LANES), blk(LANES),
                   pl.BlockSpec((1, SUBLANES, tb), lambda j: (j, 0, 0)),
                   pl.BlockSpec((1, 1, LANES), lambda j: (j, 0, 0))],
        out_shape=[jax.ShapeDtypeStruct((nb, tb, d), BF16),
                   jax.ShapeDtypeStruct((nb, tb, LANES), F32),
                   jax.ShapeDtypeStruct((nb, tb, LANES), F32),
                   jax.ShapeDtypeStruct((nb, SUBLANES, tb), F32),
                   jax.ShapeDtypeStruct((nb, 1, LANES), F32)],
        scratch_shapes=[pltpu.VMEM((1, LANES), F32)],
        compiler_params=_cparams("arbitrary"),
        name="moe_route",
    )(xb, sc, sh, jnp.pad(router_w, ((0, 0), (0, LANES - router_w.shape[1]))))


def _moe_tables(cnt, n_chunks, n_items):
    nb, ne = cnt.shape
    c = MOE_SLOT_CHUNK
    off = jnp.cumsum(cnt, axis=0) - cnt
    total = jnp.sum(cnt, axis=0)
    nch = (total + c - 1) // c
    ends = jnp.cumsum(nch)
    k = jnp.arange(n_chunks, dtype=jnp.int32)
    ce = jnp.minimum(jnp.searchsorted(ends, k, side="right"), ne - 1).astype(jnp.int32)
    cvalid = k < ends[-1]
    r0 = (k - (ends - nch)[ce]) * c
    lo = jnp.maximum(r0[:, None], off.T[ce])
    hi = jnp.minimum(r0[:, None] + c, (off + cnt).T[ce])
    overlap = cvalid[:, None] & (lo < hi)
    n_pairs = jnp.sum(overlap)
    pos = jnp.arange(n_items, dtype=jnp.int32)

    flat = jnp.nonzero(overlap.reshape(-1), size=n_items, fill_value=0)[0].astype(jnp.int32)
    flat = jnp.where(pos < n_pairs, flat, flat[jnp.maximum(n_pairs - 1, 0)])
    chunk, block = flat // nb, flat % nb
    valid = pos < n_pairs
    first = valid & ((pos == 0) | (chunk != jnp.roll(chunk, 1)))
    last = valid & ((pos == n_pairs - 1) | (chunk != jnp.roll(chunk, -1)))
    spare_chunk = ends[-1] + (pos - n_pairs)
    fill = (~valid) & (spare_chunk < n_chunks)
    chunk = jnp.where(valid, chunk, jnp.minimum(spare_chunk, n_chunks - 1)).astype(jnp.int32)
    i32 = lambda t: t.astype(jnp.int32)
    by_chunk = dict(chunk=chunk, block=block, expert=ce[chunk], r0=r0[chunk], first=i32(first | fill),
                    last=i32(last | fill), valid=i32(valid))
    region = (ends - nch) * c
    start8 = (region[None, :] + off) // SUBLANES
    r0_window = start8 * SUBLANES - region[None, :]
    rows_used = jnp.where(cnt > 0, off + cnt - r0_window, 0)
    need = (rows_used + MOE_WINDOW - 1) // MOE_WINDOW
    return (ce, i32(cvalid), by_chunk, i32(start8).reshape(-1), i32(r0_window).reshape(-1),
            i32(need).reshape(-1))


def _moe_gather_kernel(chunk_ref, block_ref, expert_ref, r0_ref, first_ref, last_ref, valid_ref,
                       hb_ref, rankt_ref, xs_ref, acc_scr):
    w = pl.program_id(0)

    @pl.when(first_ref[w] == 1)
    def _():
        acc_scr[...] = jnp.zeros_like(acc_scr)

    @pl.when(valid_ref[w] == 1)
    def _():
        c = acc_scr.shape[0]
        rank = rankt_ref[0, pl.ds(expert_ref[w], 1), :]
        slot = lax.broadcasted_iota(jnp.int32, (c, 1), 0).astype(F32) + r0_ref[w].astype(F32)
        pick = jnp.where(rank == slot, 1.0, 0.0).astype(BF16)
        acc_scr[...] += jnp.dot(pick, hb_ref[0], preferred_element_type=F32)

    @pl.when(last_ref[w] == 1)
    def _():
        xs_ref[0] = acc_scr[...].astype(BF16)


def _moe_gather(items, hb, rankt, n_chunks):
    nb, tb, d = hb.shape
    c = MOE_SLOT_CHUNK
    names = ("chunk", "block", "expert", "r0", "first", "last", "valid")
    by_block = lambda shape: pl.BlockSpec(shape, lambda w, ch, bl, *_: (bl[w], 0, 0))
    return pl.pallas_call(
        _moe_gather_kernel,
        grid_spec=pltpu.PrefetchScalarGridSpec(
            num_scalar_prefetch=len(names), grid=(items["chunk"].shape[0],),
            in_specs=[by_block((1, tb, d)), by_block((1, SUBLANES, tb))],
            out_specs=pl.BlockSpec((1, c, d), lambda w, ch, *_: (ch[w], 0, 0)),
            scratch_shapes=[pltpu.VMEM((c, d), F32)]),
        out_shape=jax.ShapeDtypeStruct((n_chunks, c, d), BF16),
        compiler_params=_cparams("arbitrary"),
        name="moe_gather",
    )(*[items[n] for n in names], hb, rankt)


def _moe_expert_kernel(ce_ref, cvalid_ref, xs_ref, wg_ref, wu_ref, wd_ref, *rest):
    *prev, o_ref, wg_scr, wu_scr, wd_scr = rest
    k = pl.program_id(0)

    @pl.when((k == 0) | (ce_ref[k] != ce_ref[jnp.maximum(k - 1, 0)]))
    def _():
        wg_scr[...] = wg_ref[0].astype(BF16)
        wu_scr[...] = wu_ref[0].astype(BF16)
        wd_scr[...] = wd_ref[0].astype(BF16)

    @pl.when(cvalid_ref[k] == 1)
    def _():
        x = xs_ref[0]
        a = jnp.dot(x, wg_scr[...], preferred_element_type=F32)
        u = jnp.dot(x, wu_scr[...], preferred_element_type=F32)
        y = _dot(_silu(a) * u, wd_scr[...])
        o_ref[0] = y + prev[0][0] if prev else y

    @pl.when(cvalid_ref[k] == 0)
    def _():
        o_ref[0] = jnp.zeros(o_ref.shape[1:], F32)


def _moe_experts(ce, cvalid, xs, wg, wu, wd, layer):
    n_chunks, c, d = xs.shape
    fh = wg.shape[3] // MOE_FF_SPLIT
    y = None
    for half in range(MOE_FF_SPLIT):
        row = pl.BlockSpec((1, c, d), lambda k, ce, cv: (k, 0, 0))
        in_specs = [row,
                    pl.BlockSpec((None, 1, d, fh), lambda k, ce, cv, half=half: (layer, ce[k], 0, half)),
                    pl.BlockSpec((None, 1, d, fh), lambda k, ce, cv, half=half: (layer, ce[k], 0, half)),
                    pl.BlockSpec((None, 1, fh, d), lambda k, ce, cv, half=half: (layer, ce[k], half, 0))]
        args = [ce, cvalid, xs, wg, wu, wd]
        aliases = {}
        if y is not None:
            in_specs.append(row)
            args.append(y)
            aliases = {len(args) - 1: 0}
        y = pl.pallas_call(
            _moe_expert_kernel,
            grid_spec=pltpu.PrefetchScalarGridSpec(
                num_scalar_prefetch=2, grid=(n_chunks,), in_specs=in_specs, out_specs=row,
                scratch_shapes=[pltpu.VMEM((d, fh), BF16), pltpu.VMEM((d, fh), BF16),
                                pltpu.VMEM((fh, d), BF16)]),
            out_shape=jax.ShapeDtypeStruct((n_chunks, c, d), F32),
            input_output_aliases=aliases,
            compiler_params=_cparams("arbitrary"),
            name="moe_experts",
        )(*args)
    return y


def _moe_combine_kernel(start_ref, r0_ref, need_ref, *refs):
    win_refs = refs[:MOE_WINDOWS]
    rank_ref, comb_ref, x_ref, gate_ref, lng_ref, lnb_ref, o_ref, acc_scr = refs[MOE_WINDOWS:]
    j, e = pl.program_id(0), pl.program_id(1)
    w = j * pl.num_programs(1) + e

    @pl.when(e == 0)
    def _():
        acc_scr[...] = jnp.zeros_like(acc_scr)

    tb = acc_scr.shape[0]
    for k, ys_ref in enumerate(win_refs):
        @pl.when(need_ref[w] > k)
        def _(k=k, ys_ref=ys_ref):
            lane = lax.broadcasted_iota(jnp.int32, (tb, LANES), 1)
            mine = lane == e
            rank = jnp.sum(jnp.where(mine, rank_ref[0], 0.0), axis=-1, keepdims=True)
            comb = jnp.sum(jnp.where(mine, comb_ref[0], 0.0), axis=-1, keepdims=True)
            first = (r0_ref[w] + k * MOE_WINDOW).astype(F32)
            slot = lax.broadcasted_iota(jnp.int32, (1, MOE_WINDOW), 1).astype(F32) + first
            pick = jnp.where(rank == slot, 1.0, 0.0).astype(BF16)
            rows = jnp.dot(pick, ys_ref[...].astype(BF16), preferred_element_type=F32)
            acc_scr[...] += comb * rows

    @pl.when(e == pl.num_programs(1) - 1)
    def _():
        y = DN_ALPHA * x_ref[0] + (1.0 + gate_ref[0]) * acc_scr[...]
        o_ref[0] = _ln(y, lng_ref[...], lnb_ref[...])


def _moe_combine_ln(start8, r0, need, ys, rank, comb, xb, gate, ln_g, ln_b, ne, blocks_per_seq):
    nb, tb, d = xb.shape
    by_block = lambda width: pl.BlockSpec((1, tb, width), lambda j, e, *_: (j, 0, 0))
    vec = pl.BlockSpec((1, d), lambda j, e, *_: (0, 0))
    window = lambda k: pl.BlockSpec(
        (pl.Element(MOE_WINDOW), pl.Element(d)),
        lambda j, e, start8, r0, need: (
            jnp.where(need[j * ne + e] > k, start8[j * ne + e] + k * (MOE_WINDOW // SUBLANES), 0) * SUBLANES, 0))
    ys2 = ys.reshape(-1, d)
    return pl.pallas_call(
        _moe_combine_kernel,
        grid_spec=pltpu.PrefetchScalarGridSpec(
            num_scalar_prefetch=3, grid=(nb, ne),
            in_specs=[window(k) for k in range(MOE_WINDOWS)] + [
                by_block(LANES), by_block(LANES), by_block(d),
                pl.BlockSpec((1, 1, d), lambda j, e, *_: (j // blocks_per_seq, 0, 0)), vec, vec],
            out_specs=by_block(d),
            scratch_shapes=[pltpu.VMEM((tb, d), F32)]),
        out_shape=jax.ShapeDtypeStruct((nb, tb, d), F32),
        compiler_params=_cparams("arbitrary", "arbitrary"),
        name="moe_combine_ln",
    )(start8, r0, need, *([ys2] * MOE_WINDOWS), rank, comb, xb, gate, ln_g.reshape(1, d),
      ln_b.reshape(1, d))


def _moe_sparse_ln(x, sc, sh, gate, router_w, wg, wu, wd, layer, ln_g, ln_b):
    g, r, d = x.shape
    tb, c, ne = MOE_TOKEN_BLOCK, MOE_SLOT_CHUNK, wg.shape[1]
    assert sc.shape[1] == 1 and r % tb == 0
    nb = g * r // tb
    assert MOE_WINDOWS * MOE_WINDOW >= tb + SUBLANES
    n_chunks = 2 * g * r // c + ne + -(-MOE_WINDOWS * MOE_WINDOW // c)
    n_items = n_chunks + ne * nb
    xb = x.reshape(nb, tb, d)
    hb, comb, rank, rankt, cnt = _moe_route(xb, sc, sh, router_w, r // tb)
    ce, cvalid, by_chunk, start8, r0, need = _moe_tables(cnt[:, 0, :ne].astype(jnp.int32), n_chunks, n_items)
    xs = _moe_gather(by_chunk, hb, rankt, n_chunks)
    ys = _moe_experts(ce, cvalid, xs, wg, wu, wd, layer)
    out = _moe_combine_ln(start8, r0, need, ys, rank, comb, xb, gate, ln_g, ln_b, ne, r // tb)
    return out.reshape(g, r, d)


def _seq_dims(b, l, long_seqs_per_step):
    if l >= SEQ_CHUNK:
        assert l % SEQ_CHUNK == 0 and b % long_seqs_per_step == 0
        return SEQ_CHUNK, l // SEQ_CHUNK, SEQ_CHUNK, long_seqs_per_step
    assert l <= SUBLANES and b % SHORT_SEQ_BLOCK == 0
    return SUBLANES, 1, l, SHORT_SEQ_BLOCK


def _pad_seq(t, c, n):
    pad = c * n - t.shape[1]
    return t if pad == 0 else jnp.pad(t, ((0, 0), (0, pad), (0, 0)))


def _rows_layout(t, c, n, width):
    b = t.shape[0]
    return jnp.swapaxes(t[:, :, :width].reshape(b, n, c, width), 2, 3)


def _conv_window(scr, u, taps, c):
    scr[SUBLANES:SUBLANES + c, :] = u
    out = scr[5:5 + c, :] * taps[0:1, :]
    for j in range(1, CONV_W):
        out = out + scr[5 + j:5 + j + c, :] * taps[j:j + 1, :]
    return out


def _seq_spec(bb, c, width):
    return pl.BlockSpec((bb, c, width), lambda b, i: (b, i, 0))


def _state_spec(bb, shape):
    return pl.BlockSpec((bb,) + shape, lambda b, i: (b,) + (0,) * len(shape))


def _layer_state_spec(bb, shape, layer):
    return pl.BlockSpec((None, bb) + shape, lambda b, i: (layer, b) + (0,) * len(shape))


def _const_spec(shape):
    return pl.BlockSpec(shape, lambda b, i: (0,) * len(shape))


def _lru_kernel(xy_ref, buf_ref, h0_ref, cw_ref, cb_ref, wg_ref, bg_ref, lam_ref,
                out_ref, hnew_ref, bufnew_ref, xs_scr, a_scr, b_scr, h_scr, *, bb, c, nv, pos0):
    i = pl.program_id(1)
    w = LRU_W

    @pl.when(i == 0)
    def _():
        xs_scr[:, 0:SUBLANES, :] = jnp.zeros((bb, SUBLANES, w), F32)
        xs_scr[:, 5:8, :] = buf_ref[...]
        h_scr[...] = h0_ref[...]

    pos = lax.broadcasted_iota(jnp.int32, (c, 1), 0) + (i * c + pos0)
    for bi in range(bb):
        xc = _conv_window(xs_scr.at[bi], xy_ref[bi, :, 0:w], cw_ref[...], c) + cb_ref[...]
        gates = _dot(xc, wg_ref[...]) + bg_ref[...]
        r = _sigmoid(gates[:, 0:w])
        ig = _sigmoid(gates[:, w:2 * w])
        log_a = -LRU_C * r * _softplus(-lam_ref[...])
        mult = jnp.sqrt(-jnp.tanh(log_a) * (jnp.exp(2.0 * log_a) + 1.0))
        mult = jnp.where(pos == 0, 1.0, mult)
        a_scr[bi] = jnp.exp(log_a)
        b_scr[bi] = xc * ig * mult

    def step(t, h):
        h = a_scr[:, pl.ds(t, 1), :] * h + b_scr[:, pl.ds(t, 1), :]
        b_scr[:, pl.ds(t, 1), :] = h
        return h

    h = lax.fori_loop(0, nv, step, h_scr[...], unroll=min(nv, SUBLANES))
    h_scr[...] = h
    out_ref[...] = b_scr[...] * _gelu_tanh(xy_ref[:, :, w:2 * w])
    tail = xs_scr[:, 5 + nv:8 + nv, :]
    xs_scr[:, 5:8, :] = tail

    @pl.when(i == pl.num_programs(1) - 1)
    def _():
        hnew_ref[...] = h
        bufnew_ref[...] = tail


def _lru_mixer(xy, buf, h0, conv_w, conv_b, wr, br, wi, bi, lam, pos0):
    b, l, _ = xy.shape
    c, n, nv, bb = _seq_dims(b, l, LRU_SEQS_PER_STEP)
    w = LRU_W
    bd = lambda m: jax.scipy.linalg.block_diag(*[m[i] for i in range(LRU_BLOCKS)])
    wgate = jnp.concatenate([bd(wr), bd(wi)], axis=1).astype(BF16)
    bgate = jnp.concatenate([br, bi]).reshape(1, 2 * w)
    out, h_new, buf_new = pl.pallas_call(
        functools.partial(_lru_kernel, bb=bb, c=c, nv=nv, pos0=pos0),
        grid=(b // bb, n),
        in_specs=[_seq_spec(bb, c, 2 * w), _state_spec(bb, (CONV_W - 1, w)), _state_spec(bb, (1, w)),
                  _const_spec((CONV_W, w)), _const_spec((1, w)), _const_spec((w, 2 * w)),
                  _const_spec((1, 2 * w)), _const_spec((1, w))],
        out_specs=[_seq_spec(bb, c, w), _state_spec(bb, (1, w)), _state_spec(bb, (CONV_W - 1, w))],
        out_shape=[jax.ShapeDtypeStruct((b, n * c, w), F32),
                   jax.ShapeDtypeStruct((b, 1, w), F32),
                   jax.ShapeDtypeStruct((b, CONV_W - 1, w), F32)],
        scratch_shapes=[pltpu.VMEM((bb, SUBLANES + c, w), F32), pltpu.VMEM((bb, c, w), F32),
                        pltpu.VMEM((bb, c, w), F32), pltpu.VMEM((bb, 1, w), F32)],
        compiler_params=_cparams("parallel", "arbitrary"),
        name="rglru",
    )(_pad_seq(xy, c, n), buf, h0.reshape(b, 1, w), conv_w, conv_b.reshape(1, w), wgate, bgate,
      lam.reshape(1, w))
    return out[:, :l], h_new.reshape(b, w), buf_new


def _gdn_kernel(qkv_ref, z_ref, gcol_ref, grow_ref, buf_ref, s0_ref, cw_ref, pcol_ref, prow_ref,
                nw_ref, out_ref, snew_ref, bufnew_ref, xs_scr, s_scr, *, bb, c, nv):
    i = pl.program_id(1)
    qk = GDN_QK

    @pl.when(i == 0)
    def _():
        xs_scr[:, 0:SUBLANES, :] = jnp.zeros((bb, SUBLANES, GDN_QKV), F32)
        xs_scr[:, 5:8, :] = buf_ref[...]
        s_scr[...] = s0_ref[...]

    valid = _valid_rows(c, nv)
    valid_r = lax.broadcasted_iota(jnp.int32, (1, c), 1) < nv
    alog_c, dtb_c = pcol_ref[0:1, :], pcol_ref[1:2, :]
    alog_r, dtb_r = prow_ref[:, 0:1], prow_ref[:, 1:2]
    incl, strict = _tri(c, True), _tri(c, False)
    upper = jnp.logical_not(strict)
    heads = []
    for bi in range(bb):
        x = _silu(_conv_window(xs_scr.at[bi], qkv_ref[bi], cw_ref[...], c))
        gcol = gcol_ref[bi]
        grow = grow_ref[bi, 0]
        g_col = jnp.where(valid, -jnp.exp(alog_c) * _softplus(gcol[:, 0:GDN_H] + dtb_c), 0.0)
        beta = jnp.where(valid, _sigmoid(gcol[:, GDN_H:2 * GDN_H]), 0.0)
        g_row = jnp.where(valid_r, -jnp.exp(alog_r) * _softplus(grow[0:GDN_H, :] + dtb_r), 0.0)
        gi_cols = _dot_sel_lhs(incl, g_col)
        gi_rows = _dot_sel_rhs(g_row, upper)
        for h in range(GDN_H):
            q = _l2norm(x[:, h * GDN_DK:(h + 1) * GDN_DK]) * (GDN_DK ** -0.5)
            k = _l2norm(x[:, qk + h * GDN_DK:qk + (h + 1) * GDN_DK])
            gh = g_col[:, h:h + 1]
            gi = gi_cols[:, h:h + 1]
            gx = gi - gh
            gi_row = gi_rows[h:h + 1, :]
            g_end = gi[c - 1:c, :]
            kb = k * beta[:, h:h + 1]
            b = -jnp.exp(gh) * kb
            e_end = jnp.exp(g_end - gi)
            heads.append(dict(
                m_lhs=jnp.concatenate([k, q], axis=0), m_rhs=jnp.concatenate([b, kb], axis=0),
                pair_x=jnp.where(strict, jnp.exp(jnp.where(strict, gx - gi_row, 0.0)), 0.0),
                pair_i=jnp.where(incl, jnp.exp(jnp.where(incl, gi - gi_row, 0.0)), 0.0),
                x_lhs=jnp.concatenate([k * jnp.exp(gx), q * jnp.exp(gi)], axis=0),
                v=x[:, 2 * qk + h * GDN_DV:2 * qk + (h + 1) * GDN_DV],
                bk=jnp.concatenate([b * e_end, kb * e_end], axis=0), s_decay=jnp.exp(g_end)))
    pairs = [(bi, h) for bi in range(bb) for h in range(GDN_H)]
    outs, new_states = _dplr_heads(heads, [s_scr[bi, h] for bi, h in pairs], c, nv, state_is_vk=False)
    for (bi, h), o, s_new in zip(pairs, outs, new_states):
        s_scr[bi, h] = s_new
        zh = z_ref[bi, :, h * GDN_DV:(h + 1) * GDN_DV]
        o = o * lax.rsqrt(jnp.mean(o * o, axis=-1, keepdims=True) + 1e-6) * nw_ref[...]
        out_ref[bi, :, h * GDN_DV:(h + 1) * GDN_DV] = o * _silu(zh)
    tail = xs_scr[:, 5 + nv:8 + nv, :]
    xs_scr[:, 5:8, :] = tail

    @pl.when(i == pl.num_programs(1) - 1)
    def _():
        snew_ref[...] = s_scr[...]
        bufnew_ref[...] = tail


def _gdn_mixer(qkv, z, gates, buf, s0, layer, conv_w, a_log, dt_bias, norm_w):
    b, l, _ = qkv.shape
    c, n, nv, bb = _seq_dims(b, l, GDN_SEQS_PER_STEP)
    gates = _pad_seq(gates, c, n)
    pcol = jnp.stack([a_log, dt_bias])
    out, s_new, buf_new = pl.pallas_call(
        functools.partial(_gdn_kernel, bb=bb, c=c, nv=nv),
        grid=(b // bb, n),
        in_specs=[_seq_spec(bb, c, GDN_QKV), _seq_spec(bb, c, GDN_W), _seq_spec(bb, c, LANES),
                  pl.BlockSpec((bb, 1, SUBLANES, c), lambda b, i: (b, i, 0, 0)),
                  _state_spec(bb, (CONV_W - 1, GDN_QKV)),
                  _layer_state_spec(bb, (GDN_H, GDN_DK, GDN_DV), layer),
                  _const_spec((CONV_W, GDN_QKV)), _const_spec((2, GDN_H)), _const_spec((GDN_H, 2)),
                  _const_spec((1, GDN_DV))],
        out_specs=[_seq_spec(bb, c, GDN_W), _state_spec(bb, (GDN_H, GDN_DK, GDN_DV)),
                   _state_spec(bb, (CONV_W - 1, GDN_QKV))],
        out_shape=[jax.ShapeDtypeStruct((b, n * c, GDN_W), F32),
                   jax.ShapeDtypeStruct((b, GDN_H, GDN_DK, GDN_DV), F32),
                   jax.ShapeDtypeStruct((b, CONV_W - 1, GDN_QKV), F32)],
        scratch_shapes=[pltpu.VMEM((bb, SUBLANES + c, GDN_QKV), F32),
                        pltpu.VMEM((bb, GDN_H, GDN_DK, GDN_DV), F32)],
        compiler_params=_cparams("parallel", "arbitrary"),
        name="gdn",
    )(_pad_seq(qkv, c, n), _pad_seq(z, c, n), gates, _rows_layout(gates, c, n, SUBLANES), buf, s0,
      conv_w, pcol, pcol.T, norm_w.reshape(1, GDN_DV))
    return out[:, :l], s_new, buf_new


def _rwkv_kernel(rw_ref, prev_ref, s0_ref, mix_ref, w0_ref, w2_ref, a0_ref, a2_ref, g2_ref,
                 kk_ref, ka_ref, rk_ref, lnw_ref, lnb_ref, out_ref, snew_ref, xs_scr, s_scr,
                 *, bb, c, nv):
    i = pl.program_id(1)
    hd = RWKV_HD
    w = RWKV_W

    @pl.when(i == 0)
    def _():
        xs_scr[:, 0:SUBLANES, :] = jnp.zeros((bb, SUBLANES, RWKV_PROJ_W), F32)
        xs_scr[:, 7:8, :] = prev_ref[...]
        s_scr[...] = s0_ref[...]

    valid = _valid_rows(c, nv)
    incl = _tri(c, True)
    heads, post = [], []
    for bi in range(bb):
        rw = rw_ref[bi]
        xs_scr[bi, SUBLANES:SUBLANES + c, :] = rw
        prev = xs_scr[bi, 7:7 + c, :]
        xs_scr[bi, 7:8, :] = xs_scr[bi, 7 + nv:8 + nv, :]
        xs = rw + (prev - rw) * mix_ref[...]
        r_all, k_all, v_all = xs[:, 0:w], xs[:, w:2 * w], xs[:, 2 * w:3 * w]
        o1 = 3 * w
        wl = xs[:, o1:o1 + RWKV_RW]
        al = xs[:, o1 + RWKV_RW:o1 + RWKV_RW + RWKV_RA]
        gl = xs[:, o1 + RWKV_RW + RWKV_RA:]
        wdec = -_softplus(-(w0_ref[...] + _dot(jnp.tanh(wl), w2_ref[...]))) - 0.5
        a_all = _sigmoid(a0_ref[...] + _dot(al, a2_ref[...]))
        gate = _dot(_sigmoid(gl), g2_ref[...])
        lw_all = jnp.where(valid, -jnp.exp(wdec), 0.0)
        kmod = jnp.where(valid, k_all * (1.0 + (a_all - 1.0) * ka_ref[...]), 0.0)
        kk_all = k_all * kk_ref[...]
        gi = _dot_sel_lhs(incl, lw_all)
        gm = gi[c // 2:c // 2 + 1, :]
        g_end = gi[c - 1:c, :]
        e_nlw = jnp.exp(-lw_all)
        e_r = jnp.exp(gi - gm)
        e_a = e_r * e_nlw
        e_m = jnp.exp(gm - gi)
        e_gi = jnp.exp(gi)
        e_gx = e_gi * e_nlw
        e_end = jnp.exp(g_end - gm) * e_m
        s_dec = jnp.exp(g_end)
        for h in range(RWKV_H):
            sl = slice(h * hd, (h + 1) * hd)
            kk = _l2norm(kk_all[:, sl])
            r, k = r_all[:, sl], kmod[:, sl]
            a = -kk
            b = jnp.where(valid, kk * a_all[:, sl], 0.0)
            heads.append(dict(
                m_lhs=jnp.concatenate([a * e_a[:, sl], r * e_r[:, sl]], axis=0),
                m_rhs=jnp.concatenate([b * e_m[:, sl], k * e_m[:, sl]], axis=0),
                x_lhs=jnp.concatenate([a * e_gx[:, sl], r * e_gi[:, sl]], axis=0),
                v=v_all[:, sl],
                bk=jnp.concatenate([b * e_end[:, sl], k * e_end[:, sl]], axis=0), s_decay=s_dec[:, sl]))
            post.append((bi, h, r, k, v_all[:, sl], gate[:, sl]))
    outs, new_states = _dplr_heads(heads, [s_scr[bi, h] for bi, h, *_ in post], c, nv, state_is_vk=True)
    for (bi, h, r, k, v, gate_h), o, s_new in zip(post, outs, new_states):
        sl = slice(h * hd, (h + 1) * hd)
        s_scr[bi, h] = s_new
        y = _ln(o, lnw_ref[:, sl], lnb_ref[:, sl], RWKV_GN_EPS)
        y = y + jnp.sum(r * k * rk_ref[:, sl], axis=-1, keepdims=True) * v
        out_ref[bi, :, sl] = y * gate_h

    @pl.when(i == pl.num_programs(1) - 1)
    def _():
        snew_ref[...] = s_scr[...]


def _rwkv_mixer(rw, shift0, s0, layer, mix, w0, w2, a0, a2, g2, k_k, k_a, r_k, ln_w, ln_b):
    b, l, _ = rw.shape
    c, n, nv, bb = _seq_dims(b, l, RWKV_SEQS_PER_STEP)
    w = RWKV_W
    row = lambda t: t.reshape(1, -1)
    state = (RWKV_H, RWKV_HD, RWKV_HD)
    out, s_new = pl.pallas_call(
        functools.partial(_rwkv_kernel, bb=bb, c=c, nv=nv),
        grid=(b // bb, n),
        in_specs=[_seq_spec(bb, c, RWKV_PROJ_W), _state_spec(bb, (1, RWKV_PROJ_W)),
                  _layer_state_spec(bb, state, layer),
                  _const_spec((1, RWKV_PROJ_W)), _const_spec((1, w)), _const_spec((RWKV_RW, w)),
                  _const_spec((1, w)), _const_spec((RWKV_RA, w)), _const_spec((RWKV_RG, w)),
                  _const_spec((1, w)), _const_spec((1, w)), _const_spec((1, w)), _const_spec((1, w)),
                  _const_spec((1, w))],
        out_specs=[_seq_spec(bb, c, w), _state_spec(bb, state)],
        out_shape=[jax.ShapeDtypeStruct((b, n * c, w), F32),
                   jax.ShapeDtypeStruct((b,) + state, F32)],
        scratch_shapes=[pltpu.VMEM((bb, SUBLANES + c, RWKV_PROJ_W), F32), pltpu.VMEM((bb,) + state, F32)],
        compiler_params=_cparams("parallel", "arbitrary"),
        name="rwkv7",
    )(_pad_seq(rw, c, n), shift0.reshape(b, 1, RWKV_PROJ_W), s0, row(mix), row(w0),
      w2.astype(BF16), row(a0), a2.astype(BF16), g2.astype(BF16), row(k_k), row(k_a), row(r_k),
      row(ln_w), row(ln_b))
    return out[:, :l], s_new


def _mlstm_kernel(p_ref, gcol_ref, grow_ref, c0_ref, n0_ref, m0_ref, bcol_ref, brow_ref, nw_ref,
                  out_ref, cnew_ref, nnew_ref, mnew_ref, c_scr, n_scr, m_scr, *, bb, c, nv):
    i = pl.program_id(1)
    nh, dk, dv = MLSTM_H, MLSTM_DK, MLSTM_DV

    @pl.when(i == 0)
    def _():
        c_scr[...] = c0_ref[...]
        n_scr[...] = n0_ref[...]
        m_scr[...] = m0_ref[...]

    valid = _valid_rows(c, nv)
    valid_r = lax.broadcasted_iota(jnp.int32, (1, c), 1) < nv
    incl = _tri(c, True)
    upper = jnp.logical_not(_tri(c, False))
    ps = [(bi, h) for bi in range(bb) for h in range(nh)]
    gate_cols = {}
    for bi in range(bb):
        gcol = gcol_ref[bi]
        grow = grow_ref[bi, 0]
        li_col = jnp.where(valid, gcol[:, 0:nh] + bcol_ref[0:1, :], NEG_BIG)
        lf_col = jnp.where(valid, -_softplus(-(gcol[:, nh:2 * nh] + bcol_ref[1:2, :])), 0.0)
        li_row = jnp.where(valid_r, grow[0:nh, :] + brow_ref[:, 0:1], NEG_BIG)
        lf_row = jnp.where(valid_r, -_softplus(-(grow[nh:2 * nh, :] + brow_ref[:, 1:2])), 0.0)
        b_cols = _dot_sel_lhs(incl, lf_col)
        b_rows = _dot_sel_rhs(lf_row, upper)
        gate_cols[bi] = (li_col, li_row, b_cols, b_rows)
    qs = [p_ref[bi, :, h * dk:(h + 1) * dk] for bi, h in ps]
    ks = [p_ref[bi, :, nh * dk + h * dk:nh * dk + (h + 1) * dk] * (dk ** -0.5) for bi, h in ps]
    vs = [p_ref[bi, :, 2 * nh * dk + h * dv:2 * nh * dk + (h + 1) * dv] for bi, h in ps]
    cms = [c_scr[bi, h] for bi, h in ps]
    nvecs = [n_scr[bi, h] for bi, h in ps]
    m_prevs = [m_scr[bi, h] for bi, h in ps]
    qks = [_dot_nt(q, k) for q, k in zip(qs, ks)]
    qcs = [_dot_nt(q, cm) for q, cm in zip(qs, cms)]
    bcs = [gate_cols[bi][2][:, h:h + 1] for bi, h in ps]
    dms = [jnp.where(incl, bc - gate_cols[bi][3][h:h + 1, :] + gate_cols[bi][1][h:h + 1, :], -jnp.inf)
           for (bi, h), bc in zip(ps, bcs)]
    m_inters = [bc + m_prev for bc, m_prev in zip(bcs, m_prevs)]
    m_ts = [jnp.maximum(mi, jnp.max(dm, axis=-1, keepdims=True)) for mi, dm in zip(m_inters, dms)]
    w_inters = [jnp.exp(mi - mt) for mi, mt in zip(m_inters, m_ts)]
    scs = [qk * jnp.exp(dm - mt) for qk, dm, mt in zip(qks, dms, m_ts)]
    scvs = [_dot(sc, v) for sc, v in zip(scs, vs)]
    m_news = [mt[c - 1:c, :] for mt in m_ts]
    b_lasts = [bc[c - 1:c, :] for bc in bcs]
    wss = [jnp.exp(b_last - bc + gate_cols[bi][0][:, h:h + 1] - m_new)
           for (bi, h), b_last, bc, m_new in zip(ps, b_lasts, bcs, m_news)]
    upds = [_dot_tn(v * ws, k) for v, ws, k in zip(vs, wss, ks)]
    for j, (bi, h) in enumerate(ps):
        num = w_inters[j] * qcs[j] + scvs[j]
        den = (w_inters[j] * jnp.sum(qs[j] * nvecs[j], axis=-1, keepdims=True)
               + jnp.sum(scs[j], axis=-1, keepdims=True))
        hh = num / jnp.maximum(jnp.abs(den), jnp.exp(-m_ts[j]))
        dec = jnp.exp(b_lasts[j] + m_prevs[j] - m_news[j])
        c_scr[bi, h] = dec * cms[j] + upds[j]
        n_scr[bi, h] = dec * nvecs[j] + jnp.sum(ks[j] * wss[j], axis=0, keepdims=True)
        m_scr[bi, h] = m_news[j]
        og = p_ref[bi, :, 2 * nh * dk + nh * dv + h * dv:2 * nh * dk + nh * dv + (h + 1) * dv]
        out_ref[bi, :, h * dv:(h + 1) * dv] = _ln(hh, nw_ref[...]) * _sigmoid(og)

    @pl.when(i == pl.num_programs(1) - 1)
    def _():
        cnew_ref[...] = c_scr[...]
        nnew_ref[...] = n_scr[...]
        mnew_ref[...] = m_scr[...]


def _mlstm_mixer(p, gates, c0, layer, n0, m0, i_b, f_b, norm_w):
    b, l, _ = p.shape
    c, n, nv, bb = _seq_dims(b, l, MLSTM_SEQS_PER_STEP)
    nh, dk, dv = MLSTM_H, MLSTM_DK, MLSTM_DV
    gates = _pad_seq(gates, c, n)
    bcol = jnp.stack([i_b, f_b])
    out, c_new, n_new, m_new = pl.pallas_call(
        functools.partial(_mlstm_kernel, bb=bb, c=c, nv=nv),
        grid=(b // bb, n),
        in_specs=[_seq_spec(bb, c, p.shape[2]), _seq_spec(bb, c, LANES),
                  pl.BlockSpec((bb, 1, SUBLANES, c), lambda b, i: (b, i, 0, 0)),
                  _layer_state_spec(bb, (nh, dv, dk), layer), _state_spec(bb, (nh, 1, dk)),
                  _state_spec(bb, (nh, 1, 1)), _const_spec((2, nh)), _const_spec((nh, 2)), _const_spec((1, dv))],
        out_specs=[_seq_spec(bb, c, nh * dv), _state_spec(bb, (nh, dv, dk)),
                   _state_spec(bb, (nh, 1, dk)), _state_spec(bb, (nh, 1, 1))],
        out_shape=[jax.ShapeDtypeStruct((b, n * c, nh * dv), F32),
                   jax.ShapeDtypeStruct((b, nh, dv, dk), F32),
                   jax.ShapeDtypeStruct((b, nh, 1, dk), F32),
                   jax.ShapeDtypeStruct((b, nh, 1, 1), F32)],
        scratch_shapes=[pltpu.VMEM((bb, nh, dv, dk), F32), pltpu.VMEM((bb, nh, 1, dk), F32),
                        pltpu.VMEM((bb, nh, 1, 1), F32)],
        compiler_params=_cparams("parallel", "arbitrary"),
        name="mlstm",
    )(_pad_seq(p, c, n), gates, _rows_layout(gates, c, n, SUBLANES), c0,
      n0.reshape(b, nh, 1, dk), m0.reshape(b, nh, 1, 1), bcol, bcol.T, norm_w.reshape(1, dv))
    return out[:, :l], c_new, n_new.reshape(b, nh, dk), m_new.reshape(b, nh)


def _pad_cols(w, n):
    return jnp.pad(w, ((0, 0), (0, n - w.shape[1])))


def _ab_in_weight(w):
    o = 2 * LRU_W
    xy, qkv = w[:, :o], w[:, o:o + GDN_QKV]
    o += GDN_QKV
    ab, z = w[:, o:o + 2 * GDN_H], w[:, o + 2 * GDN_H:]
    return jnp.concatenate([xy, qkv, z, _pad_cols(ab, LANES)], axis=1).astype(BF16)


AB_SPLITS = ((0, 2 * LRU_W), (2 * LRU_W, GDN_QKV), (2 * LRU_W + GDN_QKV, GDN_W),
             (2 * LRU_W + GDN_QKV + GDN_W, LANES))


def _cd_in_weight(w):
    o = RWKV_PROJ_W
    rw, qkv = w[:, :o], w[:, o:o + 2 * MLSTM_QK + MLSTM_W]
    o += 2 * MLSTM_QK + MLSTM_W
    gates, og = w[:, o:o + 2 * MLSTM_H], w[:, o + 2 * MLSTM_H:]
    return jnp.concatenate([rw, qkv, og, _pad_cols(gates, LANES)], axis=1).astype(BF16)


CD_SPLITS = ((0, RWKV_PROJ_W), (RWKV_PROJ_W, 2 * MLSTM_QK + 2 * MLSTM_W),
             (RWKV_PROJ_W + 2 * MLSTM_QK + 2 * MLSTM_W, LANES))


def kernel(x_prompt, x_sample, c_prompt, c_sample,
           state_lru_h, state_lru_conv, state_gdn_S, state_gdn_conv,
           state_rwkv_S, state_rwkv_shift, state_mlstm_C, state_mlstm_n, state_mlstm_m,
           mod_w, mod_b, ln1_g, ln1_b, ln2_g, ln2_b,
           ab_w_in, ab_w_out, lru_conv_w, lru_conv_b, lru_wr, lru_br, lru_wi, lru_bi, lru_lambda,
           gdn_conv_w, gdn_a_log, gdn_dt_bias, gdn_norm_w,
           cd_w_in, cd_w_out, rwkv_mix, rwkv_w0, rwkv_w2, rwkv_a0, rwkv_a2, rwkv_g2,
           rwkv_k_k, rwkv_k_a, rwkv_r_k, rwkv_ln_w, rwkv_ln_b,
           mlstm_i_b, mlstm_f_b, mlstm_norm_w,
           ffn_w_gate, ffn_w_up, ffn_w_down,
           router_w, moe_w_gate, moe_w_up, moe_w_down):
    d = D_MODEL
    bp, lp, _ = x_prompt.shape
    bs, ls, _ = x_sample.shape
    mod = _modulation(jnp.concatenate([c_prompt, c_sample], axis=0), mod_w, mod_b)

    ab_in = [_ab_in_weight(ab_w_in[j]) for j in range(ab_w_in.shape[0])]
    cd_in = [_cd_in_weight(cd_w_in[j]) for j in range(cd_w_in.shape[0])]
    ab_out, cd_out = ab_w_out.astype(BF16), cd_w_out.astype(BF16)
    ffn_g, ffn_u, ffn_d = (t.astype(BF16) for t in (ffn_w_gate, ffn_w_up, ffn_w_down))

    def trunk(x, mods, batch, length, states, pos0):
        lru_h, lru_conv, gdn_s, gdn_conv, rwkv_s, rwkv_shift, m_c, m_n, m_m = states
        new = [[] for _ in range(9)]
        seq = lambda t: t.reshape(batch, length, t.shape[-1])
        tok = lambda t: t.reshape(x.shape[0], x.shape[1], t.shape[-1])
        for l in range(DEPTH):
            j = l // 2
            sh1, sc1, g1, sh2, sc2, g2 = mods[l]
            if l % 2 == 0:
                xy, qkv, z, gates = _inproj(x, sc1, sh1, ab_in[j], AB_SPLITS)
                out_a, s0, s1 = _lru_mixer(seq(xy), lru_conv[j], lru_h[j], lru_conv_w[j], lru_conv_b[j],
                                           lru_wr[j], lru_br[j], lru_wi[j], lru_bi[j], lru_lambda[j], pos0)
                out_b, s2, s3 = _gdn_mixer(seq(qkv), seq(z), seq(gates), gdn_conv[j], gdn_s, j,
                                           gdn_conv_w[j], gdn_a_log[j], gdn_dt_bias[j], gdn_norm_w[j])
                for slot, s in zip((0, 1, 2, 3), (s0, s1, s2, s3)):
                    new[slot].append(s)
                x = _outproj_ln(x, tok(out_a), tok(out_b), g1, ab_out, j, ln1_g[l], ln1_b[l])
                x = _ffn_ln(x, sc2, sh2, g2, ffn_g, ffn_u, ffn_d, j, ln2_g[l], ln2_b[l])
            else:
                rw, mp, gates = _inproj(x, sc1, sh1, cd_in[j], CD_SPLITS)
                out_c, s0 = _rwkv_mixer(seq(rw), rwkv_shift[j], rwkv_s, j, rwkv_mix[j], rwkv_w0[j],
                                        rwkv_w2[j], rwkv_a0[j], rwkv_a2[j], rwkv_g2[j], rwkv_k_k[j],
                                        rwkv_k_a[j], rwkv_r_k[j], rwkv_ln_w[j], rwkv_ln_b[j])
                out_d, s2, s3, s4 = _mlstm_mixer(seq(mp), seq(gates), m_c, j, m_n[j], m_m[j],
                                                 mlstm_i_b[j], mlstm_f_b[j], mlstm_norm_w[j])
                for slot, s in zip((4, 5, 6, 7, 8), (s0, seq(rw)[:, -1], s2, s3, s4)):
                    new[slot].append(s)
                x = _outproj_ln(x, tok(out_c), tok(out_d), g1, cd_out, j, ln1_g[l], ln1_b[l])
                moe = _moe_sparse_ln if x.shape[1] % MOE_TOKEN_BLOCK == 0 else _moe_ln
                x = moe(x, sc2, sh2, g2, router_w[j], moe_w_gate, moe_w_up, moe_w_down, j,
                        ln2_g[l], ln2_b[l])
        return x, tuple(jnp.stack(s) for s in new)

    def zeros(ref):
        return jnp.zeros((ref.shape[0], bp) + ref.shape[2:], F32)

    mods_p = [[mod[l, :bp, k * d:(k + 1) * d].reshape(bp, 1, d) for k in range(6)] for l in range(DEPTH)]
    mods_s = [[mod[l, bp:, k * d:(k + 1) * d].reshape(1, bs * ls, d) for k in range(6)] for l in range(DEPTH)]
    states_s = (state_lru_h, state_lru_conv, state_gdn_S, state_gdn_conv, state_rwkv_S,
                state_rwkv_shift, state_mlstm_C, state_mlstm_n, state_mlstm_m)
    y_p, new_p = trunk(x_prompt, mods_p, bp, lp, tuple(zeros(s) for s in states_s), 0)
    y_s, new_s = trunk(x_sample.reshape(1, bs * ls, d), mods_s, bs, ls, states_s, PAST_LEN)
    out = [y_p, y_s.reshape(bs, ls, d)]
    for p_leaf, s_leaf in zip(new_p, new_s):
        out += [p_leaf, s_leaf]
    return tuple(out)
```

```python
import functools
import math

import jax
import jax.numpy as jnp
from jax import lax
from jax.experimental import pallas as pl
from jax.experimental.pallas import tpu as pltpu

F32 = jnp.float32
BF16 = jnp.bfloat16

D_MODEL = 1024
DEPTH = 4
PAST_LEN = 16384
CONV_W = 4
LRU_W = D_MODEL // 2
LRU_BLOCKS = 8
LRU_C = 8.0
GDN_H = D_MODEL // 256
GDN_DK = 128
GDN_DV = 128
GDN_QK = GDN_H * GDN_DK
GDN_W = GDN_H * GDN_DV
GDN_QKV = 2 * GDN_QK + GDN_W
RWKV_HD = 64
RWKV_H = D_MODEL // 2 // RWKV_HD
RWKV_W = RWKV_H * RWKV_HD
RWKV_RW = 64
RWKV_RA = 64
RWKV_RG = 128
RWKV_PROJ_W = 3 * RWKV_W + RWKV_RW + RWKV_RA + RWKV_RG
RWKV_GN_EPS = 64e-5
MLSTM_H = D_MODEL // 256
MLSTM_DK = 128
MLSTM_DV = 128
MLSTM_QK = MLSTM_H * MLSTM_DK
MLSTM_W = MLSTM_H * MLSTM_DV
N_EXPERTS = 8
LN_EPS = 1e-5
NEG_BIG = -1e30
DN_ALPHA = (2.0 * DEPTH) ** 0.25

LANES = 128
SUBLANES = 8
SEQ_CHUNK = 64
SHORT_SEQ_BLOCK = 8
LRU_SEQS_PER_STEP = 8
GDN_SEQS_PER_STEP = 4
RWKV_SEQS_PER_STEP = 2
MLSTM_SEQS_PER_STEP = 1
ROW_TILE = 512
FFN_ROW_TILE = 1024
FF_TILE = 896
MOE_TOKEN_BLOCK = 512
MOE_SLOT_CHUNK = 512
MOE_FF_SPLIT = 4
MOE_WINDOW = 256
MOE_WINDOWS = 3
VMEM_LIMIT = 48 * 1024 * 1024


def _cparams(*sem):
    return pltpu.CompilerParams(dimension_semantics=sem, vmem_limit_bytes=VMEM_LIMIT)


def _dot(a, b):
    return jnp.dot(a.astype(BF16), b.astype(BF16), preferred_element_type=F32)


def _dot_nt(a, b):
    return lax.dot_general(a.astype(BF16), b.astype(BF16), (((1,), (1,)), ((), ())),
                           preferred_element_type=F32)


def _dot_tn(a, b):
    return lax.dot_general(a.astype(BF16), b.astype(BF16), (((0,), (0,)), ((), ())),
                           preferred_element_type=F32)


def _split3(x):
    hi = x.astype(BF16)
    r1 = x - hi.astype(F32)
    mid = r1.astype(BF16)
    lo = (r1 - mid.astype(F32)).astype(BF16)
    return hi, mid, lo


def _dot_sel_lhs(t, x):
    tb = jnp.where(t, 1.0, 0.0).astype(BF16)
    hi, mid, lo = _split3(x)
    d = lambda p: jnp.dot(tb, p, preferred_element_type=F32)
    return d(hi) + d(mid) + d(lo)


def _dot_sel_rhs(x, t):
    tb = jnp.where(t, 1.0, 0.0).astype(BF16)
    hi, mid, lo = _split3(x)
    d = lambda p: jnp.dot(p, tb, preferred_element_type=F32)
    return d(hi) + d(mid) + d(lo)


def _dot2(p, x):
    pb = p.astype(BF16)
    xh = x.astype(BF16)
    xl = (x - xh.astype(F32)).astype(BF16)
    return (jnp.dot(pb, xh, preferred_element_type=F32)
            + jnp.dot(pb, xl, preferred_element_type=F32))


def _sigmoid(x):
    return 1.0 / (1.0 + jnp.exp(-x))


def _silu(x):
    return x * _sigmoid(x)


def _softplus(x):
    return jnp.maximum(x, 0.0) + jnp.log1p(jnp.exp(-jnp.abs(x)))


def _gelu_tanh(x):
    return 0.5 * x * (1.0 + jnp.tanh(math.sqrt(2.0 / math.pi) * (x + 0.044715 * (x * x * x))))


def _ln(y, g=None, b=None, eps=LN_EPS):
    mu = jnp.mean(y, axis=-1, keepdims=True)
    d = y - mu
    var = jnp.mean(d * d, axis=-1, keepdims=True)
    out = d * lax.rsqrt(var + eps)
    if g is not None:
        out = out * g
    if b is not None:
        out = out + b
    return out


def _l2norm(x, eps=1e-6):
    return x * lax.rsqrt(jnp.sum(x * x, axis=-1, keepdims=True) + eps)


def _tri(c, inclusive):
    t = lax.broadcasted_iota(jnp.int32, (c, c), 0)
    s = lax.broadcasted_iota(jnp.int32, (c, c), 1)
    return (s <= t) if inclusive else (s < t)


def _unit_lower_solve(ns, xs, c):
    steps = max(1, int(math.ceil(math.log2(c))))
    for i in range(steps):
        xs = [x + _dot2(p, x) for p, x in zip(ns, xs)]
        if i + 1 < steps:
            ns = [_dot(p, p) for p in ns]
    return xs


def _dplr_heads(heads, states, c, nv, state_is_vk):
    incl = _tri(c, True)
    strict = _tri(c, False)
    ms = [_dot_nt(h["m_lhs"], h["m_rhs"]) for h in heads]
    if state_is_vk:
        xhs = [_dot_nt(h["x_lhs"], s) for h, s in zip(heads, states)]
    else:
        xhs = [_dot(h["x_lhs"], s) for h, s in zip(heads, states)]
    a_abs, a_aks, r_bks = [], [], []
    for h, m in zip(heads, ms):
        if "pair_x" in h:
            a_abs.append(m[:c, :c] * h["pair_x"])
            a_aks.append(m[:c, c:] * h["pair_x"])
            r_bks.append(jnp.concatenate([m[c:, :c] * h["pair_i"], m[c:, c:] * h["pair_i"]], axis=1))
        else:
            a_abs.append(jnp.where(strict, m[:c, :c], 0.0))
            a_aks.append(jnp.where(strict, m[:c, c:], 0.0))
            r_bks.append(jnp.concatenate([jnp.where(incl, m[c:, :c], 0.0),
                                          jnp.where(incl, m[c:, c:], 0.0)], axis=1))
    if nv == 1:
        us = [xh[:c] for xh in xhs]
    else:
        rhs = [xh[:c] + _dot(a_ak, h["v"]) for xh, a_ak, h in zip(xhs, a_aks, heads)]
        us = _unit_lower_solve(a_abs, rhs, c)
    uvs = [jnp.concatenate([u, h["v"]], axis=0) for u, h in zip(us, heads)]
    outs = [xh[c:] + _dot(r_bk, uv) for xh, r_bk, uv in zip(xhs, r_bks, uvs)]
    if state_is_vk:
        new = [s * h["s_decay"] + _dot_tn(uv, h["bk"]) for s, h, uv in zip(states, heads, uvs)]
    else:
        new = [s * h["s_decay"] + _dot_tn(h["bk"], uv) for s, h, uv in zip(states, heads, uvs)]
    return outs, new


def _valid_rows(c, n_valid):
    return lax.broadcasted_iota(jnp.int32, (c, 1), 0) < n_valid


def _mod_kernel(c_ref, w_ref, b_ref, o_ref):
    o_ref[0] = _dot(_silu(c_ref[...]), w_ref[0]) + b_ref[0]


def _modulation(c_all, mod_w, mod_b):
    n = c_all.shape[0]
    d = D_MODEL
    return pl.pallas_call(
        _mod_kernel,
        grid=(DEPTH, 6),
        in_specs=[pl.BlockSpec((n, d), lambda l, j: (0, 0)),
                  pl.BlockSpec((1, d, d), lambda l, j: (l, 0, j)),
                  pl.BlockSpec((1, 1, d), lambda l, j: (l, 0, j))],
        out_specs=pl.BlockSpec((1, n, d), lambda l, j: (l, 0, j)),
        out_shape=jax.ShapeDtypeStruct((DEPTH, n, 6 * d), F32),
        compiler_params=_cparams("parallel", "parallel"),
        name="modulation",
    )(c_all, mod_w, mod_b.reshape(DEPTH, 1, 6 * d))


def _mod_spec(mod, tm):
    if mod.shape[1] == 1:
        return pl.BlockSpec((1, 1, mod.shape[2]), lambda g, i, *_: (g, 0, 0))
    return pl.BlockSpec((1, tm, mod.shape[2]), lambda g, i, *_: (g, i, 0))


def _inproj_kernel(x_ref, sc_ref, sh_ref, w_ref, *o_refs, splits):
    h = (x_ref[0] * (1.0 + sc_ref[0]) + sh_ref[0]).astype(BF16)
    for o_ref, (s, n) in zip(o_refs, splits):
        o_ref[0] = jnp.dot(h, w_ref[:, s:s + n], preferred_element_type=F32)


def _inproj(x, sc, sh, w, splits):
    g, r, d = x.shape
    tm = min(r, ROW_TILE)
    n_all = w.shape[1]
    return pl.pallas_call(
        functools.partial(_inproj_kernel, splits=splits),
        grid=(g, r // tm),
        in_specs=[pl.BlockSpec((1, tm, d), lambda g, i: (g, i, 0)),
                  _mod_spec(sc, tm), _mod_spec(sh, tm),
                  pl.BlockSpec((d, n_all), lambda g, i: (0, 0))],
        out_specs=[pl.BlockSpec((1, tm, n), lambda g, i: (g, i, 0)) for _, n in splits],
        out_shape=[jax.ShapeDtypeStruct((g, r, n), F32) for _, n in splits],
        compiler_params=_cparams("parallel", "parallel"),
        name="inproj",
    )(x, sc, sh, w)


def _outproj_ln_kernel(x_ref, ma_ref, mb_ref, gate_ref, w_ref, lng_ref, lnb_ref, o_ref):
    half = ma_ref.shape[2]
    f = _dot(ma_ref[0], w_ref[0:half, :]) + _dot(mb_ref[0], w_ref[half:, :])
    y = DN_ALPHA * x_ref[0] + (1.0 + gate_ref[0]) * f
    o_ref[0] = _ln(y, lng_ref[...], lnb_ref[...])


def _outproj_ln(x, mix_a, mix_b, gate, w, layer, ln_g, ln_b):
    g, r, d = x.shape
    tm = min(r, ROW_TILE)
    half = mix_a.shape[2]
    row = pl.BlockSpec((1, tm, d), lambda g, i: (g, i, 0))
    mrow = pl.BlockSpec((1, tm, half), lambda g, i: (g, i, 0))
    vec = pl.BlockSpec((1, d), lambda g, i: (0, 0))
    return pl.pallas_call(
        _outproj_ln_kernel,
        grid=(g, r // tm),
        in_specs=[row, mrow, mrow, _mod_spec(gate, tm),
                  pl.BlockSpec((None, 2 * half, d), lambda g, i: (layer, 0, 0)), vec, vec],
        out_specs=row,
        out_shape=jax.ShapeDtypeStruct((g, r, d), F32),
        compiler_params=_cparams("parallel", "parallel"),
        name="outproj_ln",
    )(x, mix_a, mix_b, gate, w, ln_g.reshape(1, d), ln_b.reshape(1, d))


def _ffn_kernel(x_ref, sc_ref, sh_ref, gate_ref, wg_ref, wu_ref, wd_ref, lng_ref, lnb_ref,
                o_ref, h_scr, acc_scr):
    j = pl.program_id(2)

    @pl.when(j == 0)
    def _():
        h_scr[...] = (x_ref[0] * (1.0 + sc_ref[0]) + sh_ref[0]).astype(BF16)
        acc_scr[...] = jnp.zeros_like(acc_scr)

    h = h_scr[...]
    a = jnp.dot(h, wg_ref[...], preferred_element_type=F32)
    u = jnp.dot(h, wu_ref[...], preferred_element_type=F32)
    acc_scr[...] += _dot(_silu(a) * u, wd_ref[...])

    @pl.when(j == pl.num_programs(2) - 1)
    def _():
        y = DN_ALPHA * x_ref[0] + (1.0 + gate_ref[0]) * acc_scr[...]
        o_ref[0] = _ln(y, lng_ref[...], lnb_ref[...])


def _ffn_ln(x, sc, sh, gate, wg, wu, wd, layer, ln_g, ln_b):
    g, r, d = x.shape
    tm = min(r, FFN_ROW_TILE)
    f = wg.shape[2]
    row = pl.BlockSpec((1, tm, d), lambda g, i, j: (g, i, 0))
    vec = pl.BlockSpec((1, d), lambda g, i, j: (0, 0))
    return pl.pallas_call(
        _ffn_kernel,
        grid=(g, r // tm, f // FF_TILE),
        in_specs=[row, _mod_spec(sc, tm), _mod_spec(sh, tm), _mod_spec(gate, tm),
                  pl.BlockSpec((None, d, FF_TILE), lambda g, i, j: (layer, 0, j)),
                  pl.BlockSpec((None, d, FF_TILE), lambda g, i, j: (layer, 0, j)),
                  pl.BlockSpec((None, FF_TILE, d), lambda g, i, j: (layer, j, 0)), vec, vec],
        out_specs=row,
        out_shape=jax.ShapeDtypeStruct((g, r, d), F32),
        scratch_shapes=[pltpu.VMEM((tm, d), BF16), pltpu.VMEM((tm, d), F32)],
        compiler_params=_cparams("parallel", "parallel", "arbitrary"),
        name="ffn_ln",
    )(x, sc, sh, gate, wg, wu, wd, ln_g.reshape(1, d), ln_b.reshape(1, d))


def _top2_route(h, rw, lane):
    logits = jnp.dot(h, rw, preferred_element_type=F32, precision=lax.Precision.HIGHEST)
    logits = jnp.where(lane < N_EXPERTS, logits, -jnp.inf)
    m1 = jnp.max(logits, axis=-1, keepdims=True)
    i1 = jnp.min(jnp.where(logits == m1, lane, LANES), axis=-1, keepdims=True)
    rest = jnp.where(lane == i1, -jnp.inf, logits)
    m2 = jnp.max(rest, axis=-1, keepdims=True)
    i2 = jnp.min(jnp.where(rest == m2, lane, LANES), axis=-1, keepdims=True)
    e2 = jnp.exp(m2 - m1)
    g1 = 1.0 / (1.0 + e2)
    g2 = e2 / (1.0 + e2)
    sel = (lane == i1) | (lane == i2)
    return sel, jnp.where(lane == i1, g1, 0.0) + jnp.where(lane == i2, g2, 0.0)


def _moe_kernel(x_ref, sc_ref, sh_ref, gate_ref, rw_ref, wg_ref, wu_ref, wd_ref, lng_ref, lnb_ref,
                o_ref, h_scr, comb_scr, acc_scr):
    e = pl.program_id(2)
    j = pl.program_id(3)
    lane = lax.broadcasted_iota(jnp.int32, comb_scr.shape, 1)

    @pl.when((e == 0) & (j == 0))
    def _():
        h = x_ref[0] * (1.0 + sc_ref[0]) + sh_ref[0]
        h_scr[...] = h.astype(BF16)
        acc_scr[...] = jnp.zeros_like(acc_scr)
        _, comb_scr[...] = _top2_route(h, rw_ref[...], lane)

    h = h_scr[...]
    a = _dot(h, wg_ref[0])
    u = _dot(h, wu_ref[0])
    comb_e = jnp.sum(jnp.where(lane == e, comb_scr[...], 0.0), axis=-1, keepdims=True)
    acc_scr[...] += comb_e * _dot(_silu(a) * u, wd_ref[0])

    @pl.when((e == pl.num_programs(2) - 1) & (j == pl.num_programs(3) - 1))
    def _():
        y = DN_ALPHA * x_ref[0] + (1.0 + gate_ref[0]) * acc_scr[...]
        o_ref[0] = _ln(y, lng_ref[...], lnb_ref[...])


def _moe_ln(x, sc, sh, gate, router_w, wg, wu, wd, layer, ln_g, ln_b):
    g, r, d = x.shape
    tm = min(r, ROW_TILE)
    _, ne, _, f = wg.shape
    row = pl.BlockSpec((1, tm, d), lambda g, i, e, j: (g, i, 0))
    vec = pl.BlockSpec((1, d), lambda g, i, e, j: (0, 0))
    rw = jnp.pad(router_w, ((0, 0), (0, LANES - ne)))
    return pl.pallas_call(
        _moe_kernel,
        grid=(g, r // tm, ne, f // FF_TILE),
        in_specs=[row, _mod_spec(sc, tm), _mod_spec(sh, tm), _mod_spec(gate, tm),
                  pl.BlockSpec((d, LANES), lambda g, i, e, j: (0, 0)),
                  pl.BlockSpec((None, 1, d, FF_TILE), lambda g, i, e, j: (layer, e, 0, j)),
                  pl.BlockSpec((None, 1, d, FF_TILE), lambda g, i, e, j: (layer, e, 0, j)),
                  pl.BlockSpec((None, 1, FF_TILE, d), lambda g, i, e, j: (layer, e, j, 0)), vec, vec],
        out_specs=row,
        out_shape=jax.ShapeDtypeStruct((g, r, d), F32),
        scratch_shapes=[pltpu.VMEM((tm, d), BF16), pltpu.VMEM((tm, LANES), F32),
                        pltpu.VMEM((tm, d), F32)],
        compiler_params=_cparams("parallel", "parallel", "arbitrary", "arbitrary"),
        name="moe_ln",
    )(x, sc, sh, gate, rw, wg, wu, wd, ln_g.reshape(1, d), ln_b.reshape(1, d))


def _moe_route_kernel(x_ref, sc_ref, sh_ref, rw_ref, hb_ref, comb_ref, rank_ref, rankt_ref, cnt_ref,
                      run_scr):
    @pl.when(pl.program_id(0) == 0)
    def _():
        run_scr[...] = jnp.zeros_like(run_scr)

    h = x_ref[0] * (1.0 + sc_ref[0]) + sh_ref[0]
    hb_ref[0] = h.astype(BF16)
    tb = h.shape[0]
    lane = lax.broadcasted_iota(jnp.int32, (tb, LANES), 1)
    sel, comb = _top2_route(h, rw_ref[...], lane)
    comb_ref[0] = comb
    ones = jnp.where(sel, 1.0, 0.0)
    before = jnp.dot(jnp.where(_tri(tb, False), 1.0, 0.0).astype(BF16), ones.astype(BF16),
                     preferred_element_type=F32)
    rank = jnp.where(sel, before + run_scr[...], -1.0)
    rank_ref[0] = rank
    rankt_ref[0] = rank.T[0:SUBLANES, :]
    cnt = jnp.sum(ones, axis=0, keepdims=True)
    cnt_ref[0] = cnt
    run_scr[...] += cnt


def _moe_route(xb, sc, sh, router_w, blocks_per_seq):
    nb, tb, d = xb.shape
    seq = lambda j: (j // blocks_per_seq, 0, 0)
    blk = lambda width: pl.BlockSpec((1, tb, width), lambda j: (j, 0, 0))
    return pl.pallas_call(
        _moe_route_kernel,
        grid=(nb,),
        in_specs=[blk(d), pl.BlockSpec((1, 1, d), seq), pl.BlockSpec((1, 1, d), seq),
                  pl.BlockSpec((d, LANES), lambda j: (0, 0))],
        out_specs=[blk(d), blk(LANES), blk(LANES),
                   pl.BlockSpec((1, SUBLANES, tb), lambda j: (j, 0, 0)),
                   pl.BlockSpec((1, 1, LANES), lambda j: (j, 0, 0))],
        out_shape=[jax.ShapeDtypeStruct((nb, tb, d), BF16),
                   jax.ShapeDtypeStruct((nb, tb, LANES), F32),
                   jax.ShapeDtypeStruct((nb, tb, LANES), F32),
                   jax.ShapeDtypeStruct((nb, SUBLANES, tb), F32),
                   jax.ShapeDtypeStruct((nb, 1, LANES), F32)],
        scratch_shapes=[pltpu.VMEM((1, LANES), F32)],
        compiler_params=_cparams("arbitrary"),
        name="moe_route",
    )(xb, sc, sh, jnp.pad(router_w, ((0, 0), (0, LANES - router_w.shape[1]))))


def _moe_tables(cnt, n_chunks, n_items):
    nb, ne = cnt.shape
    c = MOE_SLOT_CHUNK
    off = jnp.cumsum(cnt, axis=0) - cnt
    total = jnp.sum(cnt, axis=0)
    nch = (total + c - 1) // c
    ends = jnp.cumsum(nch)
    k = jnp.arange(n_chunks, dtype=jnp.int32)
    ce = jnp.minimum(jnp.searchsorted(ends, k, side="right"), ne - 1).astype(jnp.int32)
    cvalid = k < ends[-1]
    r0 = (k - (ends - nch)[ce]) * c
    lo = jnp.maximum(r0[:, None], off.T[ce])
    hi = jnp.minimum(r0[:, None] + c, (off + cnt).T[ce])
    overlap = cvalid[:, None] & (lo < hi)
    n_pairs = jnp.sum(overlap)
    pos = jnp.arange(n_items, dtype=jnp.int32)

    flat = jnp.nonzero(overlap.reshape(-1), size=n_items, fill_value=0)[0].astype(jnp.int32)
    flat = jnp.where(pos < n_pairs, flat, flat[jnp.maximum(n_pairs - 1, 0)])
    chunk, block = flat // nb, flat % nb
    valid = pos < n_pairs
    first = valid & ((pos == 0) | (chunk != jnp.roll(chunk, 1)))
    last = valid & ((pos == n_pairs - 1) | (chunk != jnp.roll(chunk, -1)))
    spare_chunk = ends[-1] + (pos - n_pairs)
    fill = (~valid) & (spare_chunk < n_chunks)
    chunk = jnp.where(valid, chunk, jnp.minimum(spare_chunk, n_chunks - 1)).astype(jnp.int32)
    i32 = lambda t: t.astype(jnp.int32)
    by_chunk = dict(chunk=chunk, block=block, expert=ce[chunk], r0=r0[chunk], first=i32(first | fill),
                    last=i32(last | fill), valid=i32(valid))
    region = (ends - nch) * c
    start8 = (region[None, :] + off) // SUBLANES
    r0_window = start8 * SUBLANES - region[None, :]
    rows_used = jnp.where(cnt > 0, off + cnt - r0_window, 0)
    need = (rows_used + MOE_WINDOW - 1) // MOE_WINDOW
    return (ce, i32(cvalid), by_chunk, i32(start8).reshape(-1), i32(r0_window).reshape(-1),
            i32(need).reshape(-1))


def _moe_gather_kernel(chunk_ref, block_ref, expert_ref, r0_ref, first_ref, last_ref, valid_ref,
                       hb_ref, rankt_ref, xs_ref, acc_scr):
    w = pl.program_id(0)

    @pl.when(first_ref[w] == 1)
    def _():
        acc_scr[...] = jnp.zeros_like(acc_scr)

    @pl.when(valid_ref[w] == 1)
    def _():
        c = acc_scr.shape[0]
        rank = rankt_ref[0, pl.ds(expert_ref[w], 1), :]
        slot = lax.broadcasted_iota(jnp.int32, (c, 1), 0).astype(F32) + r0_ref[w].astype(F32)
        pick = jnp.where(rank == slot, 1.0, 0.0).astype(BF16)
        acc_scr[...] += jnp.dot(pick, hb_ref[0], preferred_element_type=F32)

    @pl.when(last_ref[w] == 1)
    def _():
        xs_ref[0] = acc_scr[...].astype(BF16)


def _moe_gather(items, hb, rankt, n_chunks):
    nb, tb, d = hb.shape
    c = MOE_SLOT_CHUNK
    names = ("chunk", "block", "expert", "r0", "first", "last", "valid")
    by_block = lambda shape: pl.BlockSpec(shape, lambda w, ch, bl, *_: (bl[w], 0, 0))
    return pl.pallas_call(
        _moe_gather_kernel,
        grid_spec=pltpu.PrefetchScalarGridSpec(
            num_scalar_prefetch=len(names), grid=(items["chunk"].shape[0],),
            in_specs=[by_block((1, tb, d)), by_block((1, SUBLANES, tb))],
            out_specs=pl.BlockSpec((1, c, d), lambda w, ch, *_: (ch[w], 0, 0)),
            scratch_shapes=[pltpu.VMEM((c, d), F32)]),
        out_shape=jax.ShapeDtypeStruct((n_chunks, c, d), BF16),
        compiler_params=_cparams("arbitrary"),
        name="moe_gather",
    )(*[items[n] for n in names], hb, rankt)


def _moe_expert_kernel(ce_ref, cvalid_ref, xs_ref, wg_ref, wu_ref, wd_ref, *rest):
    *prev, o_ref, wg_scr, wu_scr, wd_scr = rest
    k = pl.program_id(0)

    @pl.when((k == 0) | (ce_ref[k] != ce_ref[jnp.maximum(k - 1, 0)]))
    def _():
        wg_scr[...] = wg_ref[0].astype(BF16)
        wu_scr[...] = wu_ref[0].astype(BF16)
        wd_scr[...] = wd_ref[0].astype(BF16)

    @pl.when(cvalid_ref[k] == 1)
    def _():
        x = xs_ref[0]
        a = jnp.dot(x, wg_scr[...], preferred_element_type=F32)
        u = jnp.dot(x, wu_scr[...], preferred_element_type=F32)
        y = _dot(_silu(a) * u, wd_scr[...])
        o_ref[0] = y + prev[0][0] if prev else y

    @pl.when(cvalid_ref[k] == 0)
    def _():
        o_ref[0] = jnp.zeros(o_ref.shape[1:], F32)


def _moe_experts(ce, cvalid, xs, wg, wu, wd, layer):
    n_chunks, c, d = xs.shape
    fh = wg.shape[3] // MOE_FF_SPLIT
    y = None
    for half in range(MOE_FF_SPLIT):
        row = pl.BlockSpec((1, c, d), lambda k, ce, cv: (k, 0, 0))
        in_specs = [row,
                    pl.BlockSpec((None, 1, d, fh), lambda k, ce, cv, half=half: (layer, ce[k], 0, half)),
                    pl.BlockSpec((None, 1, d, fh), lambda k, ce, cv, half=half: (layer, ce[k], 0, half)),
                    pl.BlockSpec((None, 1, fh, d), lambda k, ce, cv, half=half: (layer, ce[k], half, 0))]
        args = [ce, cvalid, xs, wg, wu, wd]
        aliases = {}
        if y is not None:
            in_specs.append(row)
            args.append(y)
            aliases = {len(args) - 1: 0}
        y = pl.pallas_call(
            _moe_expert_kernel,
            grid_spec=pltpu.PrefetchScalarGridSpec(
                num_scalar_prefetch=2, grid=(n_chunks,), in_specs=in_specs, out_specs=row,
                scratch_shapes=[pltpu.VMEM((d, fh), BF16), pltpu.VMEM((d, fh), BF16),
                                pltpu.VMEM((fh, d), BF16)]),
            out_shape=jax.ShapeDtypeStruct((n_chunks, c, d), F32),
            input_output_aliases=aliases,
            compiler_params=_cparams("arbitrary"),
            name="moe_experts",
        )(*args)
    return y


def _moe_combine_kernel(start_ref, r0_ref, need_ref, *refs):
    win_refs = refs[:MOE_WINDOWS]
    rank_ref, comb_ref, x_ref, gate_ref, lng_ref, lnb_ref, o_ref, acc_scr = refs[MOE_WINDOWS:]
    j, e = pl.program_id(0), pl.program_id(1)
    w = j * pl.num_programs(1) + e

    @pl.when(e == 0)
    def _():
        acc_scr[...] = jnp.zeros_like(acc_scr)

    tb = acc_scr.shape[0]
    for k, ys_ref in enumerate(win_refs):
        @pl.when(need_ref[w] > k)
        def _(k=k, ys_ref=ys_ref):
            lane = lax.broadcasted_iota(jnp.int32, (tb, LANES), 1)
            mine = lane == e
            rank = jnp.sum(jnp.where(mine, rank_ref[0], 0.0), axis=-1, keepdims=True)
            comb = jnp.sum(jnp.where(mine, comb_ref[0], 0.0), axis=-1, keepdims=True)
            first = (r0_ref[w] + k * MOE_WINDOW).astype(F32)
            slot = lax.broadcasted_iota(jnp.int32, (1, MOE_WINDOW), 1).astype(F32) + first
            pick = jnp.where(rank == slot, 1.0, 0.0).astype(BF16)
            rows = jnp.dot(pick, ys_ref[...].astype(BF16), preferred_element_type=F32)
            acc_scr[...] += comb * rows

    @pl.when(e == pl.num_programs(1) - 1)
    def _():
        y = DN_ALPHA * x_ref[0] + (1.0 + gate_ref[0]) * acc_scr[...]
        o_ref[0] = _ln(y, lng_ref[...], lnb_ref[...])


def _moe_combine_ln(start8, r0, need, ys, rank, comb, xb, gate, ln_g, ln_b, ne, blocks_per_seq):
    nb, tb, d = xb.shape
    by_block = lambda width: pl.BlockSpec((1, tb, width), lambda j, e, *_: (j, 0, 0))
    vec = pl.BlockSpec((1, d), lambda j, e, *_: (0, 0))
    window = lambda k: pl.BlockSpec(
        (pl.Element(MOE_WINDOW), pl.Element(d)),
        lambda j, e, start8, r0, need: (
            jnp.where(need[j * ne + e] > k, start8[j * ne + e] + k * (MOE_WINDOW // SUBLANES), 0) * SUBLANES, 0))
    ys2 = ys.reshape(-1, d)
    return pl.pallas_call(
        _moe_combine_kernel,
        grid_spec=pltpu.PrefetchScalarGridSpec(
            num_scalar_prefetch=3, grid=(nb, ne),
            in_specs=[window(k) for k in range(MOE_WINDOWS)] + [
                by_block(LANES), by_block(LANES), by_block(d),
                pl.BlockSpec((1, 1, d), lambda j, e, *_: (j // blocks_per_seq, 0, 0)), vec, vec],
            out_specs=by_block(d),
            scratch_shapes=[pltpu.VMEM((tb, d), F32)]),
        out_shape=jax.ShapeDtypeStruct((nb, tb, d), F32),
        compiler_params=_cparams("arbitrary", "arbitrary"),
        name="moe_combine_ln",
    )(start8, r0, need, *([ys2] * MOE_WINDOWS), rank, comb, xb, gate, ln_g.reshape(1, d),
      ln_b.reshape(1, d))


def _moe_sparse_ln(x, sc, sh, gate, router_w, wg, wu, wd, layer, ln_g, ln_b):
    g, r, d = x.shape
    tb, c, ne = MOE_TOKEN_BLOCK, MOE_SLOT_CHUNK, wg.shape[1]
    assert sc.shape[1] == 1 and r % tb == 0
    nb = g * r // tb
    assert MOE_WINDOWS * MOE_WINDOW >= tb + SUBLANES
    n_chunks = 2 * g * r // c + ne + -(-MOE_WINDOWS * MOE_WINDOW // c)
    n_items = n_chunks + ne * nb
    xb = x.reshape(nb, tb, d)
    hb, comb, rank, rankt, cnt = _moe_route(xb, sc, sh, router_w, r // tb)
    ce, cvalid, by_chunk, start8, r0, need = _moe_tables(cnt[:, 0, :ne].astype(jnp.int32), n_chunks, n_items)
    xs = _moe_gather(by_chunk, hb, rankt, n_chunks)
    ys = _moe_experts(ce, cvalid, xs, wg, wu, wd, layer)
    out = _moe_combine_ln(start8, r0, need, ys, rank, comb, xb, gate, ln_g, ln_b, ne, r // tb)
    return out.reshape(g, r, d)


def _seq_dims(b, l, long_seqs_per_step):
    if l >= SEQ_CHUNK:
        assert l % SEQ_CHUNK == 0 and b % long_seqs_per_step == 0
        return SEQ_CHUNK, l // SEQ_CHUNK, SEQ_CHUNK, long_seqs_per_step
    assert l <= SUBLANES and b % SHORT_SEQ_BLOCK == 0
    return SUBLANES, 1, l, SHORT_SEQ_BLOCK


def _pad_seq(t, c, n):
    pad = c * n - t.shape[1]
    return t if pad == 0 else jnp.pad(t, ((0, 0), (0, pad), (0, 0)))


def _rows_layout(t, c, n, width):
    b = t.shape[0]
    return jnp.swapaxes(t[:, :, :width].reshape(b, n, c, width), 2, 3)


def _conv_window(scr, u, taps, c):
    scr[SUBLANES:SUBLANES + c, :] = u
    out = scr[5:5 + c, :] * taps[0:1, :]
    for j in range(1, CONV_W):
        out = out + scr[5 + j:5 + j + c, :] * taps[j:j + 1, :]
    return out


def _seq_spec(bb, c, width):
    return pl.BlockSpec((bb, c, width), lambda b, i: (b, i, 0))


def _state_spec(bb, shape):
    return pl.BlockSpec((bb,) + shape, lambda b, i: (b,) + (0,) * len(shape))


def _layer_state_spec(bb, shape, layer):
    return pl.BlockSpec((None, bb) + shape, lambda b, i: (layer, b) + (0,) * len(shape))


def _const_spec(shape):
    return pl.BlockSpec(shape, lambda b, i: (0,) * len(shape))


def _lru_kernel(xy_ref, buf_ref, h0_ref, cw_ref, cb_ref, wg_ref, bg_ref, lam_ref,
                out_ref, hnew_ref, bufnew_ref, xs_scr, a_scr, b_scr, h_scr, *, bb, c, nv, pos0):
    i = pl.program_id(1)
    w = LRU_W

    @pl.when(i == 0)
    def _():
        xs_scr[:, 0:SUBLANES, :] = jnp.zeros((bb, SUBLANES, w), F32)
        xs_scr[:, 5:8, :] = buf_ref[...]
        h_scr[...] = h0_ref[...]

    pos = lax.broadcasted_iota(jnp.int32, (c, 1), 0) + (i * c + pos0)
    for bi in range(bb):
        xc = _conv_window(xs_scr.at[bi], xy_ref[bi, :, 0:w], cw_ref[...], c) + cb_ref[...]
        gates = _dot(xc, wg_ref[...]) + bg_ref[...]
        r = _sigmoid(gates[:, 0:w])
        ig = _sigmoid(gates[:, w:2 * w])
        log_a = -LRU_C * r * _softplus(-lam_ref[...])
        mult = jnp.sqrt(-jnp.tanh(log_a) * (jnp.exp(2.0 * log_a) + 1.0))
        mult = jnp.where(pos == 0, 1.0, mult)
        a_scr[bi] = jnp.exp(log_a)
        b_scr[bi] = xc * ig * mult

    def step(t, h):
        h = a_scr[:, pl.ds(t, 1), :] * h + b_scr[:, pl.ds(t, 1), :]
        b_scr[:, pl.ds(t, 1), :] = h
        return h

    h = lax.fori_loop(0, nv, step, h_scr[...], unroll=min(nv, SUBLANES))
    h_scr[...] = h
    out_ref[...] = b_scr[...] * _gelu_tanh(xy_ref[:, :, w:2 * w])
    tail = xs_scr[:, 5 + nv:8 + nv, :]
    xs_scr[:, 5:8, :] = tail

    @pl.when(i == pl.num_programs(1) - 1)
    def _():
        hnew_ref[...] = h
        bufnew_ref[...] = tail


def _lru_mixer(xy, buf, h0, conv_w, conv_b, wr, br, wi, bi, lam, pos0):
    b, l, _ = xy.shape
    c, n, nv, bb = _seq_dims(b, l, LRU_SEQS_PER_STEP)
    w = LRU_W
    bd = lambda m: jax.scipy.linalg.block_diag(*[m[i] for i in range(LRU_BLOCKS)])
    wgate = jnp.concatenate([bd(wr), bd(wi)], axis=1).astype(BF16)
    bgate = jnp.concatenate([br, bi]).reshape(1, 2 * w)
    out, h_new, buf_new = pl.pallas_call(
        functools.partial(_lru_kernel, bb=bb, c=c, nv=nv, pos0=pos0),
        grid=(b // bb, n),
        in_specs=[_seq_spec(bb, c, 2 * w), _state_spec(bb, (CONV_W - 1, w)), _state_spec(bb, (1, w)),
                  _const_spec((CONV_W, w)), _const_spec((1, w)), _const_spec((w, 2 * w)),
                  _const_spec((1, 2 * w)), _const_spec((1, w))],
        out_specs=[_seq_spec(bb, c, w), _state_spec(bb, (1, w)), _state_spec(bb, (CONV_W - 1, w))],
        out_shape=[jax.ShapeDtypeStruct((b, n * c, w), F32),
                   jax.ShapeDtypeStruct((b, 1, w), F32),
                   jax.ShapeDtypeStruct((b, CONV_W - 1, w), F32)],
        scratch_shapes=[pltpu.VMEM((bb, SUBLANES + c, w), F32), pltpu.VMEM((bb, c, w), F32),
                        pltpu.VMEM((bb, c, w), F32), pltpu.VMEM((bb, 1, w), F32)],
        compiler_params=_cparams("parallel", "arbitrary"),
        name="rglru",
    )(_pad_seq(xy, c, n), buf, h0.reshape(b, 1, w), conv_w, conv_b.reshape(1, w), wgate, bgate,
      lam.reshape(1, w))
    return out[:, :l], h_new.reshape(b, w), buf_new


def _gdn_kernel(qkv_ref, z_ref, gcol_ref, grow_ref, buf_ref, s0_ref, cw_ref, pcol_ref, prow_ref,
                nw_ref, out_ref, snew_ref, bufnew_ref, xs_scr, s_scr, *, bb, c, nv):
    i = pl.program_id(1)
    qk = GDN_QK

    @pl.when(i == 0)
    def _():
        xs_scr[:, 0:SUBLANES, :] = jnp.zeros((bb, SUBLANES, GDN_QKV), F32)
        xs_scr[:, 5:8, :] = buf_ref[...]
        s_scr[...] = s0_ref[...]

    valid = _valid_rows(c, nv)
    valid_r = lax.broadcasted_iota(jnp.int32, (1, c), 1) < nv
    alog_c, dtb_c = pcol_ref[0:1, :], pcol_ref[1:2, :]
    alog_r, dtb_r = prow_ref[:, 0:1], prow_ref[:, 1:2]
    incl, strict = _tri(c, True), _tri(c, False)
    upper = jnp.logical_not(strict)
    heads = []
    for bi in range(bb):
        x = _silu(_conv_window(xs_scr.at[bi], qkv_ref[bi], cw_ref[...], c))
        gcol = gcol_ref[bi]
        grow = grow_ref[bi, 0]
        g_col = jnp.where(valid, -jnp.exp(alog_c) * _softplus(gcol[:, 0:GDN_H] + dtb_c), 0.0)
        beta = jnp.where(valid, _sigmoid(gcol[:, GDN_H:2 * GDN_H]), 0.0)
        g_row = jnp.where(valid_r, -jnp.exp(alog_r) * _softplus(grow[0:GDN_H, :] + dtb_r), 0.0)
        gi_cols = _dot_sel_lhs(incl, g_col)
        gi_rows = _dot_sel_rhs(g_row, upper)
        for h in range(GDN_H):
            q = _l2norm(x[:, h * GDN_DK:(h + 1) * GDN_DK]) * (GDN_DK ** -0.5)
            k = _l2norm(x[:, qk + h * GDN_DK:qk + (h + 1) * GDN_DK])
            gh = g_col[:, h:h + 1]
            gi = gi_cols[:, h:h + 1]
            gx = gi - gh
            gi_row = gi_rows[h:h + 1, :]
            g_end = gi[c - 1:c, :]
            kb = k * beta[:, h:h + 1]
            b = -jnp.exp(gh) * kb
            e_end = jnp.exp(g_end - gi)
            heads.append(dict(
                m_lhs=jnp.concatenate([k, q], axis=0), m_rhs=jnp.concatenate([b, kb], axis=0),
                pair_x=jnp.where(strict, jnp.exp(jnp.where(strict, gx - gi_row, 0.0)), 0.0),
                pair_i=jnp.where(incl, jnp.exp(jnp.where(incl, gi - gi_row, 0.0)), 0.0),
                x_lhs=jnp.concatenate([k * jnp.exp(gx), q * jnp.exp(gi)], axis=0),
                v=x[:, 2 * qk + h * GDN_DV:2 * qk + (h + 1) * GDN_DV],
                bk=jnp.concatenate([b * e_end, kb * e_end], axis=0), s_decay=jnp.exp(g_end)))
    pairs = [(bi, h) for bi in range(bb) for h in range(GDN_H)]
    outs, new_states = _dplr_heads(heads, [s_scr[bi, h] for bi, h in pairs], c, nv, state_is_vk=False)
    for (bi, h), o, s_new in zip(pairs, outs, new_states):
        s_scr[bi, h] = s_new
        zh = z_ref[bi, :, h * GDN_DV:(h + 1) * GDN_DV]
        o = o * lax.rsqrt(jnp.mean(o * o, axis=-1, keepdims=True) + 1e-6) * nw_ref[...]
        out_ref[bi, :, h * GDN_DV:(h + 1) * GDN_DV] = o * _silu(zh)
    tail = xs_scr[:, 5 + nv:8 + nv, :]
    xs_scr[:, 5:8, :] = tail

    @pl.when(i == pl.num_programs(1) - 1)
    def _():
        snew_ref[...] = s_scr[...]
        bufnew_ref[...] = tail


def _gdn_mixer(qkv, z, gates, buf, s0, layer, conv_w, a_log, dt_bias, norm_w):
    b, l, _ = qkv.shape
    c, n, nv, bb = _seq_dims(b, l, GDN_SEQS_PER_STEP)
    gates = _pad_seq(gates, c, n)
    pcol = jnp.stack([a_log, dt_bias])
    out, s_new, buf_new = pl.pallas_call(
        functools.partial(_gdn_kernel, bb=bb, c=c, nv=nv),
        grid=(b // bb, n),
        in_specs=[_seq_spec(bb, c, GDN_QKV), _seq_spec(bb, c, GDN_W), _seq_spec(bb, c, LANES),
                  pl.BlockSpec((bb, 1, SUBLANES, c), lambda b, i: (b, i, 0, 0)),
                  _state_spec(bb, (CONV_W - 1, GDN_QKV)),
                  _layer_state_spec(bb, (GDN_H, GDN_DK, GDN_DV), layer),
                  _const_spec((CONV_W, GDN_QKV)), _const_spec((2, GDN_H)), _const_spec((GDN_H, 2)),
                  _const_spec((1, GDN_DV))],
        out_specs=[_seq_spec(bb, c, GDN_W), _state_spec(bb, (GDN_H, GDN_DK, GDN_DV)),
                   _state_spec(bb, (CONV_W - 1, GDN_QKV))],
        out_shape=[jax.ShapeDtypeStruct((b, n * c, GDN_W), F32),
                   jax.ShapeDtypeStruct((b, GDN_H, GDN_DK, GDN_DV), F32),
                   jax.ShapeDtypeStruct((b, CONV_W - 1, GDN_QKV), F32)],
        scratch_shapes=[pltpu.VMEM((bb, SUBLANES + c, GDN_QKV), F32),
                        pltpu.VMEM((bb, GDN_H, GDN_DK, GDN_DV), F32)],
        compiler_params=_cparams("parallel", "arbitrary"),
        name="gdn",
    )(_pad_seq(qkv, c, n), _pad_seq(z, c, n), gates, _rows_layout(gates, c, n, SUBLANES), buf, s0,
      conv_w, pcol, pcol.T, norm_w.reshape(1, GDN_DV))
    return out[:, :l], s_new, buf_new


def _rwkv_kernel(rw_ref, prev_ref, s0_ref, mix_ref, w0_ref, w2_ref, a0_ref, a2_ref, g2_ref,
                 kk_ref, ka_ref, rk_ref, lnw_ref, lnb_ref, out_ref, snew_ref, xs_scr, s_scr,
                 *, bb, c, nv):
    i = pl.program_id(1)
    hd = RWKV_HD
    w = RWKV_W

    @pl.when(i == 0)
    def _():
        xs_scr[:, 0:SUBLANES, :] = jnp.zeros((bb, SUBLANES, RWKV_PROJ_W), F32)
        xs_scr[:, 7:8, :] = prev_ref[...]
        s_scr[...] = s0_ref[...]

    valid = _valid_rows(c, nv)
    incl = _tri(c, True)
    heads, post = [], []
    for bi in range(bb):
        rw = rw_ref[bi]
        xs_scr[bi, SUBLANES:SUBLANES + c, :] = rw
        prev = xs_scr[bi, 7:7 + c, :]
        xs_scr[bi, 7:8, :] = xs_scr[bi, 7 + nv:8 + nv, :]
        xs = rw + (prev - rw) * mix_ref[...]
        r_all, k_all, v_all = xs[:, 0:w], xs[:, w:2 * w], xs[:, 2 * w:3 * w]
        o1 = 3 * w
        wl = xs[:, o1:o1 + RWKV_RW]
        al = xs[:, o1 + RWKV_RW:o1 + RWKV_RW + RWKV_RA]
        gl = xs[:, o1 + RWKV_RW + RWKV_RA:]
        wdec = -_softplus(-(w0_ref[...] + _dot(jnp.tanh(wl), w2_ref[...]))) - 0.5
        a_all = _sigmoid(a0_ref[...] + _dot(al, a2_ref[...]))
        gate = _dot(_sigmoid(gl), g2_ref[...])
        lw_all = jnp.where(valid, -jnp.exp(wdec), 0.0)
        kmod = jnp.where(valid, k_all * (1.0 + (a_all - 1.0) * ka_ref[...]), 0.0)
        kk_all = k_all * kk_ref[...]
        gi = _dot_sel_lhs(incl, lw_all)
        gm = gi[c // 2:c // 2 + 1, :]
        g_end = gi[c - 1:c, :]
        e_nlw = jnp.exp(-lw_all)
        e_r = jnp.exp(gi - gm)
        e_a = e_r * e_nlw
        e_m = jnp.exp(gm - gi)
        e_gi = jnp.exp(gi)
        e_gx = e_gi * e_nlw
        e_end = jnp.exp(g_end - gm) * e_m
        s_dec = jnp.exp(g_end)
        for h in range(RWKV_H):
            sl = slice(h * hd, (h + 1) * hd)
            kk = _l2norm(kk_all[:, sl])
            r, k = r_all[:, sl], kmod[:, sl]
            a = -kk
            b = jnp.where(valid, kk * a_all[:, sl], 0.0)
            heads.append(dict(
                m_lhs=jnp.concatenate([a * e_a[:, sl], r * e_r[:, sl]], axis=0),
                m_rhs=jnp.concatenate([b * e_m[:, sl], k * e_m[:, sl]], axis=0),
                x_lhs=jnp.concatenate([a * e_gx[:, sl], r * e_gi[:, sl]], axis=0),
                v=v_all[:, sl],
                bk=jnp.concatenate([b * e_end[:, sl], k * e_end[:, sl]], axis=0), s_decay=s_dec[:, sl]))
            post.append((bi, h, r, k, v_all[:, sl], gate[:, sl]))
    outs, new_states = _dplr_heads(heads, [s_scr[bi, h] for bi, h, *_ in post], c, nv, state_is_vk=True)
    for (bi, h, r, k, v, gate_h), o, s_new in zip(post, outs, new_states):
        sl = slice(h * hd, (h + 1) * hd)
        s_scr[bi, h] = s_new
        y = _ln(o, lnw_ref[:, sl], lnb_ref[:, sl], RWKV_GN_EPS)
        y = y + jnp.sum(r * k * rk_ref[:, sl], axis=-1, keepdims=True) * v
        out_ref[bi, :, sl] = y * gate_h

    @pl.when(i == pl.num_programs(1) - 1)
    def _():
        snew_ref[...] = s_scr[...]


def _rwkv_mixer(rw, shift0, s0, layer, mix, w0, w2, a0, a2, g2, k_k, k_a, r_k, ln_w, ln_b):
    b, l, _ = rw.shape
    c, n, nv, bb = _seq_dims(b, l, RWKV_SEQS_PER_STEP)
    w = RWKV_W
    row = lambda t: t.reshape(1, -1)
    state = (RWKV_H, RWKV_HD, RWKV_HD)
    out, s_new = pl.pallas_call(
        functools.partial(_rwkv_kernel, bb=bb, c=c, nv=nv),
        grid=(b // bb, n),
        in_specs=[_seq_spec(bb, c, RWKV_PROJ_W), _state_spec(bb, (1, RWKV_PROJ_W)),
                  _layer_state_spec(bb, state, layer),
                  _const_spec((1, RWKV_PROJ_W)), _const_spec((1, w)), _const_spec((RWKV_RW, w)),
                  _const_spec((1, w)), _const_spec((RWKV_RA, w)), _const_spec((RWKV_RG, w)),
                  _const_spec((1, w)), _const_spec((1, w)), _const_spec((1, w)), _const_spec((1, w)),
                  _const_spec((1, w))],
        out_specs=[_seq_spec(bb, c, w), _state_spec(bb, state)],
        out_shape=[jax.ShapeDtypeStruct((b, n * c, w), F32),
                   jax.ShapeDtypeStruct((b,) + state, F32)],
        scratch_shapes=[pltpu.VMEM((bb, SUBLANES + c, RWKV_PROJ_W), F32), pltpu.VMEM((bb,) + state, F32)],
        compiler_params=_cparams("parallel", "arbitrary"),
        name="rwkv7",
    )(_pad_seq(rw, c, n), shift0.reshape(b, 1, RWKV_PROJ_W), s0, row(mix), row(w0),
      w2.astype(BF16), row(a0), a2.astype(BF16), g2.astype(BF16), row(k_k), row(k_a), row(r_k),
      row(ln_w), row(ln_b))
    return out[:, :l], s_new


def _mlstm_kernel(p_ref, gcol_ref, grow_ref, c0_ref, n0_ref, m0_ref, bcol_ref, brow_ref, nw_ref,
                  out_ref, cnew_ref, nnew_ref, mnew_ref, c_scr, n_scr, m_scr, *, bb, c, nv):
    i = pl.program_id(1)
    nh, dk, dv = MLSTM_H, MLSTM_DK, MLSTM_DV

    @pl.when(i == 0)
    def _():
        c_scr[...] = c0_ref[...]
        n_scr[...] = n0_ref[...]
        m_scr[...] = m0_ref[...]

    valid = _valid_rows(c, nv)
    valid_r = lax.broadcasted_iota(jnp.int32, (1, c), 1) < nv
    incl = _tri(c, True)
    upper = jnp.logical_not(_tri(c, False))
    ps = [(bi, h) for bi in range(bb) for h in range(nh)]
    gate_cols = {}
    for bi in range(bb):
        gcol = gcol_ref[bi]
        grow = grow_ref[bi, 0]
        li_col = jnp.where(valid, gcol[:, 0:nh] + bcol_ref[0:1, :], NEG_BIG)
        lf_col = jnp.where(valid, -_softplus(-(gcol[:, nh:2 * nh] + bcol_ref[1:2, :])), 0.0)
        li_row = jnp.where(valid_r, grow[0:nh, :] + brow_ref[:, 0:1], NEG_BIG)
        lf_row = jnp.where(valid_r, -_softplus(-(grow[nh:2 * nh, :] + brow_ref[:, 1:2])), 0.0)
        b_cols = _dot_sel_lhs(incl, lf_col)
        b_rows = _dot_sel_rhs(lf_row, upper)
        gate_cols[bi] = (li_col, li_row, b_cols, b_rows)
    qs = [p_ref[bi, :, h * dk:(h + 1) * dk] for bi, h in ps]
    ks = [p_ref[bi, :, nh * dk + h * dk:nh * dk + (h + 1) * dk] * (dk ** -0.5) for bi, h in ps]
    vs = [p_ref[bi, :, 2 * nh * dk + h * dv:2 * nh * dk + (h + 1) * dv] for bi, h in ps]
    cms = [c_scr[bi, h] for bi, h in ps]
    nvecs = [n_scr[bi, h] for bi, h in ps]
    m_prevs = [m_scr[bi, h] for bi, h in ps]
    qks = [_dot_nt(q, k) for q, k in zip(qs, ks)]
    qcs = [_dot_nt(q, cm) for q, cm in zip(qs, cms)]
    bcs = [gate_cols[bi][2][:, h:h + 1] for bi, h in ps]
    dms = [jnp.where(incl, bc - gate_cols[bi][3][h:h + 1, :] + gate_cols[bi][1][h:h + 1, :], -jnp.inf)
           for (bi, h), bc in zip(ps, bcs)]
    m_inters = [bc + m_prev for bc, m_prev in zip(bcs, m_prevs)]
    m_ts = [jnp.maximum(mi, jnp.max(dm, axis=-1, keepdims=True)) for mi, dm in zip(m_inters, dms)]
    w_inters = [jnp.exp(mi - mt) for mi, mt in zip(m_inters, m_ts)]
    scs = [qk * jnp.exp(dm - mt) for qk, dm, mt in zip(qks, dms, m_ts)]
    scvs = [_dot(sc, v) for sc, v in zip(scs, vs)]
    m_news = [mt[c - 1:c, :] for mt in m_ts]
    b_lasts = [bc[c - 1:c, :] for bc in bcs]
    wss = [jnp.exp(b_last - bc + gate_cols[bi][0][:, h:h + 1] - m_new)
           for (bi, h), b_last, bc, m_new in zip(ps, b_lasts, bcs, m_news)]
    upds = [_dot_tn(v * ws, k) for v, ws, k in zip(vs, wss, ks)]
    for j, (bi, h) in enumerate(ps):
        num = w_inters[j] * qcs[j] + scvs[j]
        den = (w_inters[j] * jnp.sum(qs[j] * nvecs[j], axis=-1, keepdims=True)
               + jnp.sum(scs[j], axis=-1, keepdims=True))
        hh = num / jnp.maximum(jnp.abs(den), jnp.exp(-m_ts[j]))
        dec = jnp.exp(b_lasts[j] + m_prevs[j] - m_news[j])
        c_scr[bi, h] = dec * cms[j] + upds[j]
        n_scr[bi, h] = dec * nvecs[j] + jnp.sum(ks[j] * wss[j], axis=0, keepdims=True)
        m_scr[bi, h] = m_news[j]
        og = p_ref[bi, :, 2 * nh * dk + nh * dv + h * dv:2 * nh * dk + nh * dv + (h + 1) * dv]
        out_ref[bi, :, h * dv:(h + 1) * dv] = _ln(hh, nw_ref[...]) * _sigmoid(og)

    @pl.when(i == pl.num_programs(1) - 1)
    def _():
        cnew_ref[...] = c_scr[...]
        nnew_ref[...] = n_scr[...]
        mnew_ref[...] = m_scr[...]


def _mlstm_mixer(p, gates, c0, layer, n0, m0, i_b, f_b, norm_w):
    b, l, _ = p.shape
    c, n, nv, bb = _seq_dims(b, l, MLSTM_SEQS_PER_STEP)
    nh, dk, dv = MLSTM_H, MLSTM_DK, MLSTM_DV
    gates = _pad_seq(gates, c, n)
    bcol = jnp.stack([i_b, f_b])
    out, c_new, n_new, m_new = pl.pallas_call(
        functools.partial(_mlstm_kernel, bb=bb, c=c, nv=nv),
        grid=(b // bb, n),
        in_specs=[_seq_spec(bb, c, p.shape[2]), _seq_spec(bb, c, LANES),
                  pl.BlockSpec((bb, 1, SUBLANES, c), lambda b, i: (b, i, 0, 0)),
                  _layer_state_spec(bb, (nh, dv, dk), layer), _state_spec(bb, (nh, 1, dk)),
                  _state_spec(bb, (nh, 1, 1)), _const_spec((2, nh)), _const_spec((nh, 2)), _const_spec((1, dv))],
        out_specs=[_seq_spec(bb, c, nh * dv), _state_spec(bb, (nh, dv, dk)),
                   _state_spec(bb, (nh, 1, dk)), _state_spec(bb, (nh, 1, 1))],
        out_shape=[jax.ShapeDtypeStruct((b, n * c, nh * dv), F32),
                   jax.ShapeDtypeStruct((b, nh, dv, dk), F32),
                   jax.ShapeDtypeStruct((b, nh, 1, dk), F32),
                   jax.ShapeDtypeStruct((b, nh, 1, 1), F32)],
        scratch_shapes=[pltpu.VMEM((bb, nh, dv, dk), F32), pltpu.VMEM((bb, nh, 1, dk), F32),
                        pltpu.VMEM((bb, nh, 1, 1), F32)],
        compiler_params=_cparams("parallel", "arbitrary"),
        name="mlstm",
    )(_pad_seq(p, c, n), gates, _rows_layout(gates, c, n, SUBLANES), c0,
      n0.reshape(b, nh, 1, dk), m0.reshape(b, nh, 1, 1), bcol, bcol.T, norm_w.reshape(1, dv))
    return out[:, :l], c_new, n_new.reshape(b, nh, dk), m_new.reshape(b, nh)


def _pad_cols(w, n):
    return jnp.pad(w, ((0, 0), (0, n - w.shape[1])))


def _ab_in_weight(w):
    o = 2 * LRU_W
    xy, qkv = w[:, :o], w[:, o:o + GDN_QKV]
    o += GDN_QKV
    ab, z = w[:, o:o + 2 * GDN_H], w[:, o + 2 * GDN_H:]
    return jnp.concatenate([xy, qkv, z, _pad_cols(ab, LANES)], axis=1).astype(BF16)


AB_SPLITS = ((0, 2 * LRU_W), (2 * LRU_W, GDN_QKV), (2 * LRU_W + GDN_QKV, GDN_W),
             (2 * LRU_W + GDN_QKV + GDN_W, LANES))


def _cd_in_weight(w):
    o = RWKV_PROJ_W
    rw, qkv = w[:, :o], w[:, o:o + 2 * MLSTM_QK + MLSTM_W]
    o += 2 * MLSTM_QK + MLSTM_W
    gates, og = w[:, o:o + 2 * MLSTM_H], w[:, o + 2 * MLSTM_H:]
    return jnp.concatenate([rw, qkv, og, _pad_cols(gates, LANES)], axis=1).astype(BF16)


CD_SPLITS = ((0, RWKV_PROJ_W), (RWKV_PROJ_W, 2 * MLSTM_QK + 2 * MLSTM_W),
             (RWKV_PROJ_W + 2 * MLSTM_QK + 2 * MLSTM_W, LANES))


def kernel(x_prompt, x_sample, c_prompt, c_sample,
           state_lru_h, state_lru_conv, state_gdn_S, state_gdn_conv,
           state_rwkv_S, state_rwkv_shift, state_mlstm_C, state_mlstm_n, state_mlstm_m,
           mod_w, mod_b, ln1_g, ln1_b, ln2_g, ln2_b,
           ab_w_in, ab_w_out, lru_conv_w, lru_conv_b, lru_wr, lru_br, lru_wi, lru_bi, lru_lambda,
           gdn_conv_w, gdn_a_log, gdn_dt_bias, gdn_norm_w,
           cd_w_in, cd_w_out, rwkv_mix, rwkv_w0, rwkv_w2, rwkv_a0, rwkv_a2, rwkv_g2,
           rwkv_k_k, rwkv_k_a, rwkv_r_k, rwkv_ln_w, rwkv_ln_b,
           mlstm_i_b, mlstm_f_b, mlstm_norm_w,
           ffn_w_gate, ffn_w_up, ffn_w_down,
           router_w, moe_w_gate, moe_w_up, moe_w_down):
    d = D_MODEL
    bp, lp, _ = x_prompt.shape
    bs, ls, _ = x_sample.shape
    mod = _modulation(jnp.concatenate([c_prompt, c_sample], axis=0), mod_w, mod_b)

    ab_in = [_ab_in_weight(ab_w_in[j]) for j in range(ab_w_in.shape[0])]
    cd_in = [_cd_in_weight(cd_w_in[j]) for j in range(cd_w_in.shape[0])]
    ab_out, cd_out = ab_w_out.astype(BF16), cd_w_out.astype(BF16)
    ffn_g, ffn_u, ffn_d = (t.astype(BF16) for t in (ffn_w_gate, ffn_w_up, ffn_w_down))

    def trunk(x, mods, batch, length, states, pos0):
        lru_h, lru_conv, gdn_s, gdn_conv, rwkv_s, rwkv_shift, m_c, m_n, m_m = states
        new = [[] for _ in range(9)]
        seq = lambda t: t.reshape(batch, length, t.shape[-1])
        tok = lambda t: t.reshape(x.shape[0], x.shape[1], t.shape[-1])
        for l in range(DEPTH):
            j = l // 2
            sh1, sc1, g1, sh2, sc2, g2 = mods[l]
            if l % 2 == 0:
                xy, qkv, z, gates = _inproj(x, sc1, sh1, ab_in[j], AB_SPLITS)
                out_a, s0, s1 = _lru_mixer(seq(xy), lru_conv[j], lru_h[j], lru_conv_w[j], lru_conv_b[j],
                                           lru_wr[j], lru_br[j], lru_wi[j], lru_bi[j], lru_lambda[j], pos0)
                out_b, s2, s3 = _gdn_mixer(seq(qkv), seq(z), seq(gates), gdn_conv[j], gdn_s, j,
                                           gdn_conv_w[j], gdn_a_log[j], gdn_dt_bias[j], gdn_norm_w[j])
                for slot, s in zip((0, 1, 2, 3), (s0, s1, s2, s3)):
                    new[slot].append(s)
                x = _outproj_ln(x, tok(out_a), tok(out_b), g1, ab_out, j, ln1_g[l], ln1_b[l])
                x = _ffn_ln(x, sc2, sh2, g2, ffn_g, ffn_u, ffn_d, j, ln2_g[l], ln2_b[l])
            else:
                rw, mp, gates = _inproj(x, sc1, sh1, cd_in[j], CD_SPLITS)
                out_c, s0 = _rwkv_mixer(seq(rw), rwkv_shift[j], rwkv_s, j, rwkv_mix[j], rwkv_w0[j],
                                        rwkv_w2[j], rwkv_a0[j], rwkv_a2[j], rwkv_g2[j], rwkv_k_k[j],
                                        rwkv_k_a[j], rwkv_r_k[j], rwkv_ln_w[j], rwkv_ln_b[j])
                out_d, s2, s3, s4 = _mlstm_mixer(seq(mp), seq(gates), m_c, j, m_n[j], m_m[j],
                                                 mlstm_i_b[j], mlstm_f_b[j], mlstm_norm_w[j])
                for slot, s in zip((4, 5, 6, 7, 8), (s0, seq(rw)[:, -1], s2, s3, s4)):
                    new[slot].append(s)
                x = _outproj_ln(x, tok(out_c), tok(out_d), g1, cd_out, j, ln1_g[l], ln1_b[l])
                moe = _moe_sparse_ln if x.shape[1] % MOE_TOKEN_BLOCK == 0 else _moe_ln
                x = moe(x, sc2, sh2, g2, router_w[j], moe_w_gate, moe_w_up, moe_w_down, j,
                        ln2_g[l], ln2_b[l])
        return x, tuple(jnp.stack(s) for s in new)

    def zeros(ref):
        return jnp.zeros((ref.shape[0], bp) + ref.shape[2:], F32)

    mods_p = [[mod[l, :bp, k * d:(k + 1) * d].reshape(bp, 1, d) for k in range(6)] for l in range(DEPTH)]
    mods_s = [[mod[l, bp:, k * d:(k + 1) * d].reshape(1, bs * ls, d) for k in range(6)] for l in range(DEPTH)]
    states_s = (state_lru_h, state_lru_conv, state_gdn_S, state_gdn_conv, state_rwkv_S,
                state_rwkv_shift, state_mlstm_C, state_mlstm_n, state_mlstm_m)
    y_p, new_p = trunk(x_prompt, mods_p, bp, lp, tuple(zeros(s) for s in states_s), 0)
    y_s, new_s = trunk(x_sample.reshape(1, bs * ls, d), mods_s, bs, ls, states_s, PAST_LEN)
    out = [y_p, y_s.reshape(bs, ls, d)]
    for p_leaf, s_leaf in zip(new_p, new_s):
        out += [p_leaf, s_leaf]
    return tuple(out)
```

```python
import functools
import math

import jax
import jax.numpy as jnp
from jax import lax
from jax.experimental import pallas as pl
from jax.experimental.pallas import tpu as pltpu

F32 = jnp.float32
BF16 = jnp.bfloat16

D_MODEL = 1024
DEPTH = 4
PAST_LEN = 16384
CONV_W = 4
LRU_W = D_MODEL // 2
LRU_BLOCKS = 8
LRU_BW = LRU_W // LRU_BLOCKS
LRU_C = 8.0
GDN_H = D_MODEL // 256
GDN_DK = 128
GDN_DV = 128
GDN_QK = GDN_H * GDN_DK
GDN_W = GDN_H * GDN_DV
GDN_QKV = 2 * GDN_QK + GDN_W
RWKV_HD = 64
RWKV_H = D_MODEL // 2 // RWKV_HD
RWKV_W = RWKV_H * RWKV_HD
RWKV_RW = 64
RWKV_RA = 64
RWKV_RG = 128
RWKV_PROJ_W = 3 * RWKV_W + RWKV_RW + RWKV_RA + RWKV_RG
RWKV_GN_EPS = 64e-5
MLSTM_H = D_MODEL // 256
MLSTM_DK = 128
MLSTM_DV = 128
MLSTM_QK = MLSTM_H * MLSTM_DK
MLSTM_W = MLSTM_H * MLSTM_DV
D_FF = 7 * D_MODEL // 2
N_EXPERTS = 8
LN_EPS = 1e-5
NEG_BIG = -1e30
DN_ALPHA = (2.0 * DEPTH) ** 0.25

LANES = 128
SUBLANES = 8
SEQ_CHUNK = 64
MLSTM_CHUNK = 256
SHORT_SEQ_BLOCK = 8
LRU_SEQS_PER_STEP = 8
GDN_SEQS_PER_STEP = 4
RWKV_SEQS_PER_STEP = 2
MLSTM_SEQS_PER_STEP = 1
ROW_TILE = 512
FFN_ROW_TILE = 1024
FF_TILE = 512
MOE_TOKEN_BLOCK = 512
MOE_SLOT_CHUNK = 512
MOE_FF_SPLIT = 4
MOE_WINDOW = 256
MOE_WINDOWS = 3
VMEM_LIMIT = 48 * 1024 * 1024


def _cparams(*sem):
    return pltpu.CompilerParams(dimension_semantics=sem, vmem_limit_bytes=VMEM_LIMIT)


def _dot(a, b):
    return jnp.dot(a.astype(BF16), b.astype(BF16), preferred_element_type=F32)


def _dot_nt(a, b):
    return lax.dot_general(a.astype(BF16), b.astype(BF16), (((1,), (1,)), ((), ())),
                           preferred_element_type=F32)


def _dot_tn(a, b):
    return lax.dot_general(a.astype(BF16), b.astype(BF16), (((0,), (0,)), ((), ())),
                           preferred_element_type=F32)


def _split3(x):
    hi = x.astype(BF16)
    r1 = x - hi.astype(F32)
    mid = r1.astype(BF16)
    lo = (r1 - mid.astype(F32)).astype(BF16)
    return hi, mid, lo


def _dot_sel_lhs(t, x):
    tb = jnp.where(t, 1.0, 0.0).astype(BF16)
    hi, mid, lo = _split3(x)
    d = lambda p: jnp.dot(tb, p, preferred_element_type=F32)
    return d(hi) + d(mid) + d(lo)


def _dot_sel_rhs(x, t):
    tb = jnp.where(t, 1.0, 0.0).astype(BF16)
    hi, mid, lo = _split3(x)
    d = lambda p: jnp.dot(p, tb, preferred_element_type=F32)
    return d(hi) + d(mid) + d(lo)


def _dot2(p, x):
    pb = p.astype(BF16)
    xh = x.astype(BF16)
    xl = (x - xh.astype(F32)).astype(BF16)
    return (jnp.dot(pb, xh, preferred_element_type=F32)
            + jnp.dot(pb, xl, preferred_element_type=F32))


def _sigmoid(x):
    return 1.0 / (1.0 + jnp.exp(-x))


def _silu(x):
    return x * _sigmoid(x)


def _softplus(x):
    return jnp.maximum(x, 0.0) + jnp.log1p(jnp.exp(-jnp.abs(x)))


def _gelu_tanh(x):
    return 0.5 * x * (1.0 + jnp.tanh(math.sqrt(2.0 / math.pi) * (x + 0.044715 * (x * x * x))))


def _ln(y, g=None, b=None, eps=LN_EPS):
    mu = jnp.mean(y, axis=-1, keepdims=True)
    d = y - mu
    var = jnp.mean(d * d, axis=-1, keepdims=True)
    out = d * lax.rsqrt(var + eps)
    if g is not None:
        out = out * g
    if b is not None:
        out = out + b
    return out


def _l2norm(x, eps=1e-6):
    return x * lax.rsqrt(jnp.sum(x * x, axis=-1, keepdims=True) + eps)


def _tri(c, inclusive):
    t = lax.broadcasted_iota(jnp.int32, (c, c), 0)
    s = lax.broadcasted_iota(jnp.int32, (c, c), 1)
    return (s <= t) if inclusive else (s < t)


def _unit_lower_solve(ns, xs, c):
    steps = max(1, int(math.ceil(math.log2(c))))
    for i in range(steps):
        xs = [x + _dot2(p, x) for p, x in zip(ns, xs)]
        if i + 1 < steps:
            ns = [_dot(p, p) for p in ns]
    return xs


def _dplr_heads(heads, states, c, nv, state_is_vk):
    incl = _tri(c, True)
    strict = _tri(c, False)
    ms = [_dot_nt(h["m_lhs"], h["m_rhs"]) for h in heads]
    if state_is_vk:
        xhs = [_dot_nt(h["x_lhs"], s) for h, s in zip(heads, states)]
    else:
        xhs = [_dot(h["x_lhs"], s) for h, s in zip(heads, states)]
    a_abs, a_aks, r_bks = [], [], []
    for h, m in zip(heads, ms):
        if "pair_x" in h:
            a_abs.append(m[:c, :c] * h["pair_x"])
            a_aks.append(m[:c, c:] * h["pair_x"])
            r_bks.append(jnp.concatenate([m[c:, :c] * h["pair_i"], m[c:, c:] * h["pair_i"]], axis=1))
        else:
            a_abs.append(jnp.where(strict, m[:c, :c], 0.0))
            a_aks.append(jnp.where(strict, m[:c, c:], 0.0))
            r_bks.append(jnp.concatenate([jnp.where(incl, m[c:, :c], 0.0),
                                          jnp.where(incl, m[c:, c:], 0.0)], axis=1))
    if nv == 1:
        us = [xh[:c] for xh in xhs]
    else:
        rhs = [xh[:c] + _dot(a_ak, h["v"]) for xh, a_ak, h in zip(xhs, a_aks, heads)]
        us = _unit_lower_solve(a_abs, rhs, c)
    uvs = [jnp.concatenate([u, h["v"]], axis=0) for u, h in zip(us, heads)]
    outs = [xh[c:] + _dot(r_bk, uv) for xh, r_bk, uv in zip(xhs, r_bks, uvs)]
    if state_is_vk:
        new = [s * h["s_decay"] + _dot_tn(uv, h["bk"]) for s, h, uv in zip(states, heads, uvs)]
    else:
        new = [s * h["s_decay"] + _dot_tn(h["bk"], uv) for s, h, uv in zip(states, heads, uvs)]
    return outs, new


def _valid_rows(c, n_valid):
    return lax.broadcasted_iota(jnp.int32, (c, 1), 0) < n_valid


def _mod_kernel(c_ref, w_ref, b_ref, o_ref):
    o_ref[0] = _dot(_silu(c_ref[...]), w_ref[0]) + b_ref[0]


def _modulation(c_all, mod_w, mod_b):
    n = c_all.shape[0]
    d = D_MODEL
    return pl.pallas_call(
        _mod_kernel,
        grid=(DEPTH, 6),
        in_specs=[pl.BlockSpec((n, d), lambda l, j: (0, 0)),
                  pl.BlockSpec((1, d, d), lambda l, j: (l, 0, j)),
                  pl.BlockSpec((1, 1, d), lambda l, j: (l, 0, j))],
        out_specs=pl.BlockSpec((1, n, d), lambda l, j: (l, 0, j)),
        out_shape=jax.ShapeDtypeStruct((DEPTH, n, 6 * d), F32),
        compiler_params=_cparams("parallel", "parallel"),
        name="modulation",
    )(c_all, mod_w, mod_b.reshape(DEPTH, 1, 6 * d))


def _mod_spec(mod, tm):
    if mod.shape[1] == 1:
        return pl.BlockSpec((1, 1, mod.shape[2]), lambda g, i, *_: (g, 0, 0))
    return pl.BlockSpec((1, tm, mod.shape[2]), lambda g, i, *_: (g, i, 0))


def _inproj_kernel(x_ref, sc_ref, sh_ref, w_ref, *o_refs, splits):
    h = (x_ref[0] * (1.0 + sc_ref[0]) + sh_ref[0]).astype(BF16)
    for o_ref, (s, n) in zip(o_refs, splits):
        o_ref[0] = jnp.dot(h, w_ref[:, s:s + n], preferred_element_type=F32)


def _inproj(x, sc, sh, w, splits):
    g, r, d = x.shape
    tm = min(r, ROW_TILE)
    n_all = w.shape[1]
    return pl.pallas_call(
        functools.partial(_inproj_kernel, splits=splits),
        grid=(g, r // tm),
        in_specs=[pl.BlockSpec((1, tm, d), lambda g, i: (g, i, 0)),
                  _mod_spec(sc, tm), _mod_spec(sh, tm),
                  pl.BlockSpec((d, n_all), lambda g, i: (0, 0))],
        out_specs=[pl.BlockSpec((1, tm, n), lambda g, i: (g, i, 0)) for _, n in splits],
        out_shape=[jax.ShapeDtypeStruct((g, r, n), F32) for _, n in splits],
        compiler_params=_cparams("parallel", "parallel"),
        name="inproj",
    )(x, sc, sh, w)


def _outproj_ln_kernel(x_ref, ma_ref, mb_ref, gate_ref, w_ref, lng_ref, lnb_ref, o_ref):
    half = ma_ref.shape[2]
    f = _dot(ma_ref[0], w_ref[0:half, :]) + _dot(mb_ref[0], w_ref[half:, :])
    y = DN_ALPHA * x_ref[0] + (1.0 + gate_ref[0]) * f
    o_ref[0] = _ln(y, lng_ref[...], lnb_ref[...])


def _outproj_ln(x, mix_a, mix_b, gate, w, layer, ln_g, ln_b):
    g, r, d = x.shape
    tm = min(r, ROW_TILE)
    half = mix_a.shape[2]
    row = pl.BlockSpec((1, tm, d), lambda g, i: (g, i, 0))
    mrow = pl.BlockSpec((1, tm, half), lambda g, i: (g, i, 0))
    vec = pl.BlockSpec((1, d), lambda g, i: (0, 0))
    return pl.pallas_call(
        _outproj_ln_kernel,
        grid=(g, r // tm),
        in_specs=[row, mrow, mrow, _mod_spec(gate, tm),
                  pl.BlockSpec((None, 2 * half, d), lambda g, i: (layer, 0, 0)), vec, vec],
        out_specs=row,
        out_shape=jax.ShapeDtypeStruct((g, r, d), F32),
        compiler_params=_cparams("parallel", "parallel"),
        name="outproj_ln",
    )(x, mix_a, mix_b, gate, w, ln_g.reshape(1, d), ln_b.reshape(1, d))


def _ffn_kernel(x_ref, sc_ref, sh_ref, gate_ref, wg_ref, wu_ref, wd_ref, lng_ref, lnb_ref,
                o_ref, h_scr, acc_scr):
    j = pl.program_id(2)

    @pl.when(j == 0)
    def _():
        h_scr[...] = (x_ref[0] * (1.0 + sc_ref[0]) + sh_ref[0]).astype(BF16)
        acc_scr[...] = jnp.zeros_like(acc_scr)

    h = h_scr[...]
    a = jnp.dot(h, wg_ref[...], preferred_element_type=F32)
    u = jnp.dot(h, wu_ref[...], preferred_element_type=F32)
    acc_scr[...] += _dot(_silu(a) * u, wd_ref[...])

    @pl.when(j == pl.num_programs(2) - 1)
    def _():
        y = DN_ALPHA * x_ref[0] + (1.0 + gate_ref[0]) * acc_scr[...]
        o_ref[0] = _ln(y, lng_ref[...], lnb_ref[...])


def _ffn_ln(x, sc, sh, gate, wg, wu, wd, layer, ln_g, ln_b):
    g, r, d = x.shape
    tm = min(r, FFN_ROW_TILE)
    f = wg.shape[2]
    row = pl.BlockSpec((1, tm, d), lambda g, i, j: (g, i, 0))
    vec = pl.BlockSpec((1, d), lambda g, i, j: (0, 0))
    return pl.pallas_call(
        _ffn_kernel,
        grid=(g, r // tm, f // FF_TILE),
        in_specs=[row, _mod_spec(sc, tm), _mod_spec(sh, tm), _mod_spec(gate, tm),
                  pl.BlockSpec((None, d, FF_TILE), lambda g, i, j: (layer, 0, j)),
                  pl.BlockSpec((None, d, FF_TILE), lambda g, i, j: (layer, 0, j)),
                  pl.BlockSpec((None, FF_TILE, d), lambda g, i, j: (layer, j, 0)), vec, vec],
        out_specs=row,
        out_shape=jax.ShapeDtypeStruct((g, r, d), F32),
        scratch_shapes=[pltpu.VMEM((tm, d), BF16), pltpu.VMEM((tm, d), F32)],
        compiler_params=_cparams("parallel", "parallel", "arbitrary"),
        name="ffn_ln",
    )(x, sc, sh, gate, wg, wu, wd, ln_g.reshape(1, d), ln_b.reshape(1, d))


def _top2_route(h, rw, lane):
    logits = jnp.dot(h, rw, preferred_element_type=F32, precision=lax.Precision.HIGHEST)
    logits = jnp.where(lane < N_EXPERTS, logits, -jnp.inf)
    m1 = jnp.max(logits, axis=-1, keepdims=True)
    i1 = jnp.min(jnp.where(logits == m1, lane, LANES), axis=-1, keepdims=True)
    rest = jnp.where(lane == i1, -jnp.inf, logits)
    m2 = jnp.max(rest, axis=-1, keepdims=True)
    i2 = jnp.min(jnp.where(rest == m2, lane, LANES), axis=-1, keepdims=True)
    e2 = jnp.exp(m2 - m1)
    g1 = 1.0 / (1.0 + e2)
    g2 = e2 / (1.0 + e2)
    sel = (lane == i1) | (lane == i2)
    return sel, jnp.where(lane == i1, g1, 0.0) + jnp.where(lane == i2, g2, 0.0)


def _moe_kernel(x_ref, sc_ref, sh_ref, gate_ref, rw_ref, wg_ref, wu_ref, wd_ref, lng_ref, lnb_ref,
                o_ref, h_scr, comb_scr, acc_scr):
    e = pl.program_id(2)
    j = pl.program_id(3)
    lane = lax.broadcasted_iota(jnp.int32, comb_scr.shape, 1)

    @pl.when((e == 0) & (j == 0))
    def _():
        h = x_ref[0] * (1.0 + sc_ref[0]) + sh_ref[0]
        h_scr[...] = h.astype(BF16)
        acc_scr[...] = jnp.zeros_like(acc_scr)
        _, comb_scr[...] = _top2_route(h, rw_ref[...], lane)

    h = h_scr[...]
    a = _dot(h, wg_ref[0])
    u = _dot(h, wu_ref[0])
    comb_e = jnp.sum(jnp.where(lane == e, comb_scr[...], 0.0), axis=-1, keepdims=True)
    acc_scr[...] += comb_e * _dot(_silu(a) * u, wd_ref[0])

    @pl.when((e == pl.num_programs(2) - 1) & (j == pl.num_programs(3) - 1))
    def _():
        y = DN_ALPHA * x_ref[0] + (1.0 + gate_ref[0]) * acc_scr[...]
        o_ref[0] = _ln(y, lng_ref[...], lnb_ref[...])


def _moe_ln(x, sc, sh, gate, router_w, wg, wu, wd, layer, ln_g, ln_b):
    g, r, d = x.shape
    tm = min(r, ROW_TILE)
    _, ne, _, f = wg.shape
    row = pl.BlockSpec((1, tm, d), lambda g, i, e, j: (g, i, 0))
    vec = pl.BlockSpec((1, d), lambda g, i, e, j: (0, 0))
    rw = jnp.pad(router_w, ((0, 0), (0, LANES - ne)))
    return pl.pallas_call(
        _moe_kernel,
        grid=(g, r // tm, ne, f // FF_TILE),
        in_specs=[row, _mod_spec(sc, tm), _mod_spec(sh, tm), _mod_spec(gate, tm),
                  pl.BlockSpec((d, LANES), lambda g, i, e, j: (0, 0)),
                  pl.BlockSpec((None, 1, d, FF_TILE), lambda g, i, e, j: (layer, e, 0, j)),
                  pl.BlockSpec((None, 1, d, FF_TILE), lambda g, i, e, j: (layer, e, 0, j)),
                  pl.BlockSpec((None, 1, FF_TILE, d), lambda g, i, e, j: (layer, e, j, 0)), vec, vec],
        out_specs=row,
        out_shape=jax.ShapeDtypeStruct((g, r, d), F32),
        scratch_shapes=[pltpu.VMEM((tm, d), BF16), pltpu.VMEM((tm, LANES), F32),
                        pltpu.VMEM((tm, d), F32)],
        compiler_params=_cparams("parallel", "parallel", "arbitrary", "arbitrary"),
        name="moe_ln",
    )(x, sc, sh, gate, rw, wg, wu, wd, ln_g.reshape(1, d), ln_b.reshape(1, d))


def _moe_route_kernel(x_ref, sc_ref, sh_ref, rw_ref, hb_ref, comb_ref, rank_ref, rankt_ref, cnt_ref,
                      run_scr):
    @pl.when(pl.program_id(0) == 0)
    def _():
        run_scr[...] = jnp.zeros_like(run_scr)

    h = x_ref[0] * (1.0 + sc_ref[0]) + sh_ref[0]
    hb_ref[0] = h.astype(BF16)
    tb = h.shape[0]
    lane = lax.broadcasted_iota(jnp.int32, (tb, LANES), 1)
    sel, comb = _top2_route(h, rw_ref[...], lane)
    comb_ref[0] = comb
    ones = jnp.where(sel, 1.0, 0.0)
    before = jnp.dot(jnp.where(_tri(tb, False), 1.0, 0.0).astype(BF16), ones.astype(BF16),
                     preferred_element_type=F32)
    rank = jnp.where(sel, before + run_scr[...], -1.0)
    rank_ref[0] = rank
    rankt_ref[0] = rank.T[0:SUBLANES, :]
    cnt = jnp.sum(ones, axis=0, keepdims=True)
    cnt_ref[0] = cnt
    run_scr[...] += cnt


def _moe_route(xb, sc, sh, router_w, blocks_per_seq):
    nb, tb, d = xb.shape
    seq = lambda j: (j // blocks_per_seq, 0, 0)
    blk = lambda width: pl.BlockSpec((1, tb, width), lambda j: (j, 0, 0))
    return pl.pallas_call(
        _moe_route_kernel,
        grid=(nb,),
        in_specs=[blk(d), pl.BlockSpec((1, 1, d), seq), pl.BlockSpec((1, 1, d), seq),
                  pl.BlockSpec((d, LANES), lambda j: (0, 0))],
        out_specs=[blk(d), blk(LANES), blk(LANES),
                   pl.BlockSpec((1, SUBLANES, tb), lambda j: (j, 0, 0)),
                   pl.BlockSpec((1, 1, LANES), lambda j: (j, 0, 0))],
        out_shape=[jax.ShapeDtypeStruct((nb, tb, d), BF16),
                   jax.ShapeDtypeStruct((nb, tb, LANES), F32),
                   jax.ShapeDtypeStruct((nb, tb, LANES), F32),
                   jax.ShapeDtypeStruct((nb, SUBLANES, tb), F32),
                   jax.ShapeDtypeStruct((nb, 1, LANES), F32)],
        scratch_shapes=[pltpu.VMEM((1, LANES), F32)],
        compiler_params=_cparams("arbitrary"),
        name="moe_route",
    )(xb, sc, sh, jnp.pad(router_w, ((0, 0), (0, LANES - router_w.shape[1]))))


def _moe_tables(cnt, n_chunks, n_items):
    nb, ne = cnt.shape
    c = MOE_SLOT_CHUNK
    off = jnp.cumsum(cnt, axis=0) - cnt
    total = jnp.sum(cnt, axis=0)
    nch = (total + c - 1) // c
    ends = jnp.cumsum(nch)
    k = jnp.arange(n_chunks, dtype=jnp.int32)
    ce = jnp.minimum(jnp.searchsorted(ends, k, side="right"), ne - 1).astype(jnp.int32)
    cvalid = k < ends[-1]
    r0 = (k - (ends - nch)[ce]) * c
    lo = jnp.maximum(r0[:, None], off.T[ce])
    hi = jnp.minimum(r0[:, None] + c, (off + cnt).T[ce])
    overlap = cvalid[:, None] & (lo < hi)
    n_pairs = jnp.sum(overlap)
    pos = jnp.arange(n_items, dtype=jnp.int32)

    flat = jnp.nonzero(overlap.reshape(-1), size=n_items, fill_value=0)[0].astype(jnp.int32)
    flat = jnp.where(pos < n_pairs, flat, flat[jnp.maximum(n_pairs - 1, 0)])
    chunk, block = flat // nb, flat % nb
    valid = pos < n_pairs
    first = valid & ((pos == 0) | (chunk != jnp.roll(chunk, 1)))
    last = valid & ((pos == n_pairs - 1) | (chunk != jnp.roll(chunk, -1)))
    spare_chunk = ends[-1] + (pos - n_pairs)
    fill = (~valid) & (spare_chunk < n_chunks)
    chunk = jnp.where(valid, chunk, jnp.minimum(spare_chunk, n_chunks - 1)).astype(jnp.int32)
    i32 = lambda t: t.astype(jnp.int32)
    by_chunk = dict(chunk=chunk, block=block, expert=ce[chunk], r0=r0[chunk], first=i32(first | fill),
                    last=i32(last | fill), valid=i32(valid))
    region = (ends - nch) * c
    start8 = (region[None, :] + off) // SUBLANES
    r0_window = start8 * SUBLANES - region[None, :]
    rows_used = jnp.where(cnt > 0, off + cnt - r0_window, 0)
    need = (rows_used + MOE_WINDOW - 1) // MOE_WINDOW
    return (ce, i32(cvalid), by_chunk, i32(start8).reshape(-1), i32(r0_window).reshape(-1),
            i32(need).reshape(-1))


def _moe_gather_kernel(chunk_ref, block_ref, expert_ref, r0_ref, first_ref, last_ref, valid_ref,
                       hb_ref, rankt_ref, xs_ref, acc_scr):
    w = pl.program_id(0)

    @pl.when(first_ref[w] == 1)
    def _():
        acc_scr[...] = jnp.zeros_like(acc_scr)

    @pl.when(valid_ref[w] == 1)
    def _():
        c = acc_scr.shape[0]
        rank = rankt_ref[0, pl.ds(expert_ref[w], 1), :]
        slot = lax.broadcasted_iota(jnp.int32, (c, 1), 0).astype(F32) + r0_ref[w].astype(F32)
        pick = jnp.where(rank == slot, 1.0, 0.0).astype(BF16)
        acc_scr[...] += jnp.dot(pick, hb_ref[0], preferred_element_type=F32)

    @pl.when(last_ref[w] == 1)
    def _():
        xs_ref[0] = acc_scr[...].astype(BF16)


def _moe_gather(items, hb, rankt, n_chunks):
    nb, tb, d = hb.shape
    c = MOE_SLOT_CHUNK
    names = ("chunk", "block", "expert", "r0", "first", "last", "valid")
    by_block = lambda shape: pl.BlockSpec(shape, lambda w, ch, bl, *_: (bl[w], 0, 0))
    return pl.pallas_call(
        _moe_gather_kernel,
        grid_spec=pltpu.PrefetchScalarGridSpec(
            num_scalar_prefetch=len(names), grid=(items["chunk"].shape[0],),
            in_specs=[by_block((1, tb, d)), by_block((1, SUBLANES, tb))],
            out_specs=pl.BlockSpec((1, c, d), lambda w, ch, *_: (ch[w], 0, 0)),
            scratch_shapes=[pltpu.VMEM((c, d), F32)]),
        out_shape=jax.ShapeDtypeStruct((n_chunks, c, d), BF16),
        compiler_params=_cparams("arbitrary"),
        name="moe_gather",
    )(*[items[n] for n in names], hb, rankt)


def _moe_expert_kernel(ce_ref, cvalid_ref, xs_ref, wg_ref, wu_ref, wd_ref, *rest):
    *prev, o_ref, wg_scr, wu_scr, wd_scr = rest
    k = pl.program_id(0)

    @pl.when((k == 0) | (ce_ref[k] != ce_ref[jnp.maximum(k - 1, 0)]))
    def _():
        wg_scr[...] = wg_ref[0].astype(BF16)
        wu_scr[...] = wu_ref[0].astype(BF16)
        wd_scr[...] = wd_ref[0].astype(BF16)

    @pl.when(cvalid_ref[k] == 1)
    def _():
        x = xs_ref[0]
        a = jnp.dot(x, wg_scr[...], preferred_element_type=F32)
        u = jnp.dot(x, wu_scr[...], preferred_element_type=F32)
        y = _dot(_silu(a) * u, wd_scr[...])
        o_ref[0] = y + prev[0][0] if prev else y

    @pl.when(cvalid_ref[k] == 0)
    def _():
        o_ref[0] = jnp.zeros(o_ref.shape[1:], F32)


def _moe_experts(ce, cvalid, xs, wg, wu, wd, layer):
    n_chunks, c, d = xs.shape
    fh = wg.shape[3] // MOE_FF_SPLIT
    y = None
    for half in range(MOE_FF_SPLIT):
        row = pl.BlockSpec((1, c, d), lambda k, ce, cv: (k, 0, 0))
        in_specs = [row,
                    pl.BlockSpec((None, 1, d, fh), lambda k, ce, cv, half=half: (layer, ce[k], 0, half)),
                    pl.BlockSpec((None, 1, d, fh), lambda k, ce, cv, half=half: (layer, ce[k], 0, half)),
                    pl.BlockSpec((None, 1, fh, d), lambda k, ce, cv, half=half: (layer, ce[k], half, 0))]
        args = [ce, cvalid, xs, wg, wu, wd]
        aliases = {}
        if y is not None:
            in_specs.append(row)
            args.append(y)
            aliases = {len(args) - 1: 0}
        y = pl.pallas_call(
            _moe_expert_kernel,
            grid_spec=pltpu.PrefetchScalarGridSpec(
                num_scalar_prefetch=2, grid=(n_chunks,), in_specs=in_specs, out_specs=row,
                scratch_shapes=[pltpu.VMEM((d, fh), BF16), pltpu.VMEM((d, fh), BF16),
                                pltpu.VMEM((fh, d), BF16)]),
            out_shape=jax.ShapeDtypeStruct((n_chunks, c, d), F32),
            input_output_aliases=aliases,
            compiler_params=_cparams("arbitrary"),
            name="moe_experts",
        )(*args)
    return y


def _moe_combine_kernel(start_ref, r0_ref, need_ref, *refs):
    win_refs = refs[:MOE_WINDOWS]
    rank_ref, comb_ref, x_ref, gate_ref, lng_ref, lnb_ref, o_ref, acc_scr = refs[MOE_WINDOWS:]
    j, e = pl.program_id(0), pl.program_id(1)
    w = j * pl.num_programs(1) + e

    @pl.when(e == 0)
    def _():
        acc_scr[...] = jnp.zeros_like(acc_scr)

    tb = acc_scr.shape[0]
    for k, ys_ref in enumerate(win_refs):
        @pl.when(need_ref[w] > k)
        def _(k=k, ys_ref=ys_ref):
            lane = lax.broadcasted_iota(jnp.int32, (tb, LANES), 1)
            mine = lane == e
            rank = jnp.sum(jnp.where(mine, rank_ref[0], 0.0), axis=-1, keepdims=True)
            comb = jnp.sum(jnp.where(mine, comb_ref[0], 0.0), axis=-1, keepdims=True)
            first = (r0_ref[w] + k * MOE_WINDOW).astype(F32)
            slot = lax.broadcasted_iota(jnp.int32, (1, MOE_WINDOW), 1).astype(F32) + first
            pick = jnp.where(rank == slot, 1.0, 0.0).astype(BF16)
            rows = jnp.dot(pick, ys_ref[...].astype(BF16), preferred_element_type=F32)
            acc_scr[...] += comb * rows

    @pl.when(e == pl.num_programs(1) - 1)
    def _():
        y = DN_ALPHA * x_ref[0] + (1.0 + gate_ref[0]) * acc_scr[...]
        o_ref[0] = _ln(y, lng_ref[...], lnb_ref[...])


def _moe_combine_ln(start8, r0, need, ys, rank, comb, xb, gate, ln_g, ln_b, ne, blocks_per_seq):
    nb, tb, d = xb.shape
    by_block = lambda width: pl.BlockSpec((1, tb, width), lambda j, e, *_: (j, 0, 0))
    vec = pl.BlockSpec((1, d), lambda j, e, *_: (0, 0))
    window = lambda k: pl.BlockSpec(
        (pl.Element(MOE_WINDOW), pl.Element(d)),
        lambda j, e, start8, r0, need: (
            jnp.where(need[j * ne + e] > k, start8[j * ne + e] + k * (MOE_WINDOW // SUBLANES), 0) * SUBLANES, 0))
    ys2 = ys.reshape(-1, d)
    return pl.pallas_call(
        _moe_combine_kernel,
        grid_spec=pltpu.PrefetchScalarGridSpec(
            num_scalar_prefetch=3, grid=(nb, ne),
            in_specs=[window(k) for k in range(MOE_WINDOWS)] + [
                by_block(LANES), by_block(LANES), by_block(d),
                pl.BlockSpec((1, 1, d), lambda j, e, *_: (j // blocks_per_seq, 0, 0)), vec, vec],
            out_specs=by_block(d),
            scratch_shapes=[pltpu.VMEM((tb, d), F32)]),
        out_shape=jax.ShapeDtypeStruct((nb, tb, d), F32),
        compiler_params=_cparams("arbitrary", "arbitrary"),
        name="moe_combine_ln",
    )(start8, r0, need, *([ys2] * MOE_WINDOWS), rank, comb, xb, gate, ln_g.reshape(1, d),
      ln_b.reshape(1, d))


def _moe_sparse_ln(x, sc, sh, gate, router_w, wg, wu, wd, layer, ln_g, ln_b):
    g, r, d = x.shape
    tb, c, ne = MOE_TOKEN_BLOCK, MOE_SLOT_CHUNK, wg.shape[1]
    assert sc.shape[1] == 1 and r % tb == 0
    nb = g * r // tb
    assert MOE_WINDOWS * MOE_WINDOW >= tb + SUBLANES
    n_chunks = 2 * g * r // c + ne + -(-MOE_WINDOWS * MOE_WINDOW // c)
    n_items = n_chunks + ne * nb
    xb = x.reshape(nb, tb, d)
    hb, comb, rank, rankt, cnt = _moe_route(xb, sc, sh, router_w, r // tb)
    ce, cvalid, by_chunk, start8, r0, need = _moe_tables(cnt[:, 0, :ne].astype(jnp.int32), n_chunks, n_items)
    xs = _moe_gather(by_chunk, hb, rankt, n_chunks)
    ys = _moe_experts(ce, cvalid, xs, wg, wu, wd, layer)
    out = _moe_combine_ln(start8, r0, need, ys, rank, comb, xb, gate, ln_g, ln_b, ne, r // tb)
    return out.reshape(g, r, d)


def _seq_dims(b, l, long_seqs_per_step, chunk=SEQ_CHUNK):
    if l >= chunk:
        assert l % chunk == 0 and b % long_seqs_per_step == 0
        return chunk, l // chunk, chunk, long_seqs_per_step
    assert l <= SUBLANES and b % SHORT_SEQ_BLOCK == 0
    return SUBLANES, 1, l, SHORT_SEQ_BLOCK


def _pad_seq(t, c, n):
    pad = c * n - t.shape[1]
    return t if pad == 0 else jnp.pad(t, ((0, 0), (0, pad), (0, 0)))


def _rows_layout(t, c, n, width):
    b = t.shape[0]
    return jnp.swapaxes(t[:, :, :width].reshape(b, n, c, width), 2, 3)


def _conv_window(scr, u, taps, c):
    scr[SUBLANES:SUBLANES + c, :] = u
    out = scr[5:5 + c, :] * taps[0:1, :]
    for j in range(1, CONV_W):
        out = out + scr[5 + j:5 + j + c, :] * taps[j:j + 1, :]
    return out


def _seq_spec(bb, c, width):
    return pl.BlockSpec((bb, c, width), lambda b, i: (b, i, 0))


def _state_spec(bb, shape):
    return pl.BlockSpec((bb,) + shape, lambda b, i: (b,) + (0,) * len(shape))


def _layer_state_spec(bb, shape, layer):
    return pl.BlockSpec((None, bb) + shape, lambda b, i: (layer, b) + (0,) * len(shape))


def _const_spec(shape):
    return pl.BlockSpec(shape, lambda b, i: (0,) * len(shape))


def _lru_kernel(xy_ref, buf_ref, h0_ref, cw_ref, cb_ref, wg_ref, bg_ref, lam_ref,
                out_ref, hnew_ref, bufnew_ref, xs_scr, a_scr, b_scr, h_scr, *, bb, c, nv, pos0):
    i = pl.program_id(1)
    w = LRU_W

    @pl.when(i == 0)
    def _():
        xs_scr[:, 0:SUBLANES, :] = jnp.zeros((bb, SUBLANES, w), F32)
        xs_scr[:, 5:8, :] = buf_ref[...]
        h_scr[...] = h0_ref[...]

    pos = lax.broadcasted_iota(jnp.int32, (c, 1), 0) + (i * c + pos0)
    for bi in range(bb):
        xc = _conv_window(xs_scr.at[bi], xy_ref[bi, :, 0:w], cw_ref[...], c) + cb_ref[...]
        gates = _dot(xc, wg_ref[...]) + bg_ref[...]
        r = _sigmoid(gates[:, 0:w])
        ig = _sigmoid(gates[:, w:2 * w])
        log_a = -LRU_C * r * _softplus(-lam_ref[...])
        mult = jnp.sqrt(-jnp.tanh(log_a) * (jnp.exp(2.0 * log_a) + 1.0))
        mult = jnp.where(pos == 0, 1.0, mult)
        a_scr[bi] = jnp.exp(log_a)
        b_scr[bi] = xc * ig * mult

    def step(t, h):
        h = a_scr[:, pl.ds(t, 1), :] * h + b_scr[:, pl.ds(t, 1), :]
        b_scr[:, pl.ds(t, 1), :] = h
        return h

    h = lax.fori_loop(0, nv, step, h_scr[...], unroll=min(nv, SUBLANES))
    h_scr[...] = h
    out_ref[...] = b_scr[...] * _gelu_tanh(xy_ref[:, :, w:2 * w])
    tail = xs_scr[:, 5 + nv:8 + nv, :]
    xs_scr[:, 5:8, :] = tail

    @pl.when(i == pl.num_programs(1) - 1)
    def _():
        hnew_ref[...] = h
        bufnew_ref[...] = tail


def _lru_mixer(xy, buf, h0, conv_w, conv_b, wr, br, wi, bi, lam, pos0):
    b, l, _ = xy.shape
    c, n, nv, bb = _seq_dims(b, l, LRU_SEQS_PER_STEP)
    w = LRU_W
    bd = lambda m: jax.scipy.linalg.block_diag(*[m[i] for i in range(LRU_BLOCKS)])
    wgate = jnp.concatenate([bd(wr), bd(wi)], axis=1).astype(BF16)
    bgate = jnp.concatenate([br, bi]).reshape(1, 2 * w)
    out, h_new, buf_new = pl.pallas_call(
        functools.partial(_lru_kernel, bb=bb, c=c, nv=nv, pos0=pos0),
        grid=(b // bb, n),
        in_specs=[_seq_spec(bb, c, 2 * w), _state_spec(bb, (CONV_W - 1, w)), _state_spec(bb, (1, w)),
                  _const_spec((CONV_W, w)), _const_spec((1, w)), _const_spec((w, 2 * w)),
                  _const_spec((1, 2 * w)), _const_spec((1, w))],
        out_specs=[_seq_spec(bb, c, w), _state_spec(bb, (1, w)), _state_spec(bb, (CONV_W - 1, w))],
        out_shape=[jax.ShapeDtypeStruct((b, n * c, w), F32),
                   jax.ShapeDtypeStruct((b, 1, w), F32),
                   jax.ShapeDtypeStruct((b, CONV_W - 1, w), F32)],
        scratch_shapes=[pltpu.VMEM((bb, SUBLANES + c, w), F32), pltpu.VMEM((bb, c, w), F32),
                        pltpu.VMEM((bb, c, w), F32), pltpu.VMEM((bb, 1, w), F32)],
        compiler_params=_cparams("parallel", "arbitrary"),
        name="rglru",
    )(_pad_seq(xy, c, n), buf, h0.reshape(b, 1, w), conv_w, conv_b.reshape(1, w), wgate, bgate,
      lam.reshape(1, w))
    return out[:, :l], h_new.reshape(b, w), buf_new


def _gdn_kernel(qkv_ref, z_ref, gcol_ref, grow_ref, buf_ref, s0_ref, cw_ref, pcol_ref, prow_ref,
                nw_ref, out_ref, snew_ref, bufnew_ref, xs_scr, s_scr, *, bb, c, nv):
    i = pl.program_id(1)
    qk = GDN_QK

    @pl.when(i == 0)
    def _():
        xs_scr[:, 0:SUBLANES, :] = jnp.zeros((bb, SUBLANES, GDN_QKV), F32)
        xs_scr[:, 5:8, :] = buf_ref[...]
        s_scr[...] = s0_ref[...]

    valid = _valid_rows(c, nv)
    valid_r = lax.broadcasted_iota(jnp.int32, (1, c), 1) < nv
    alog_c, dtb_c = pcol_ref[0:1, :], pcol_ref[1:2, :]
    alog_r, dtb_r = prow_ref[:, 0:1], prow_ref[:, 1:2]
    incl, strict = _tri(c, True), _tri(c, False)
    upper = jnp.logical_not(strict)
    heads = []
    for bi in range(bb):
        x = _silu(_conv_window(xs_scr.at[bi], qkv_ref[bi], cw_ref[...], c))
        gcol = gcol_ref[bi]
        grow = grow_ref[bi, 0]
        g_col = jnp.where(valid, -jnp.exp(alog_c) * _softplus(gcol[:, 0:GDN_H] + dtb_c), 0.0)
        beta = jnp.where(valid, _sigmoid(gcol[:, GDN_H:2 * GDN_H]), 0.0)
        g_row = jnp.where(valid_r, -jnp.exp(alog_r) * _softplus(grow[0:GDN_H, :] + dtb_r), 0.0)
        gi_cols = _dot_sel_lhs(incl, g_col)
        gi_rows = _dot_sel_rhs(g_row, upper)
        for h in range(GDN_H):
            q = _l2norm(x[:, h * GDN_DK:(h + 1) * GDN_DK]) * (GDN_DK ** -0.5)
            k = _l2norm(x[:, qk + h * GDN_DK:qk + (h + 1) * GDN_DK])
            gh = g_col[:, h:h + 1]
            gi = gi_cols[:, h:h + 1]
            gx = gi - gh
            gi_row = gi_rows[h:h + 1, :]
            g_end = gi[c - 1:c, :]
            kb = k * beta[:, h:h + 1]
            b = -jnp.exp(gh) * kb
            e_end = jnp.exp(g_end - gi)
            heads.append(dict(
                m_lhs=jnp.concatenate([k, q], axis=0), m_rhs=jnp.concatenate([b, kb], axis=0),
                pair_x=jnp.where(strict, jnp.exp(jnp.where(strict, gx - gi_row, 0.0)), 0.0),
                pair_i=jnp.where(incl, jnp.exp(jnp.where(incl, gi - gi_row, 0.0)), 0.0),
                x_lhs=jnp.concatenate([k * jnp.exp(gx), q * jnp.exp(gi)], axis=0),
                v=x[:, 2 * qk + h * GDN_DV:2 * qk + (h + 1) * GDN_DV],
                bk=jnp.concatenate([b * e_end, kb * e_end], axis=0), s_decay=jnp.exp(g_end)))
    pairs = [(bi, h) for bi in range(bb) for h in range(GDN_H)]
    outs, new_states = _dplr_heads(heads, [s_scr[bi, h] for bi, h in pairs], c, nv, state_is_vk=False)
    for (bi, h), o, s_new in zip(pairs, outs, new_states):
        s_scr[bi, h] = s_new
        zh = z_ref[bi, :, h * GDN_DV:(h + 1) * GDN_DV]
        o = o * lax.rsqrt(jnp.mean(o * o, axis=-1, keepdims=True) + 1e-6) * nw_ref[...]
        out_ref[bi, :, h * GDN_DV:(h + 1) * GDN_DV] = o * _silu(zh)
    tail = xs_scr[:, 5 + nv:8 + nv, :]
    xs_scr[:, 5:8, :] = tail

    @pl.when(i == pl.num_programs(1) - 1)
    def _():
        snew_ref[...] = s_scr[...]
        bufnew_ref[...] = tail


def _gdn_mixer(qkv, z, gates, buf, s0, layer, conv_w, a_log, dt_bias, norm_w):
    b, l, _ = qkv.shape
    c, n, nv, bb = _seq_dims(b, l, GDN_SEQS_PER_STEP)
    gates = _pad_seq(gates, c, n)
    pcol = jnp.stack([a_log, dt_bias])
    out, s_new, buf_new = pl.pallas_call(
        functools.partial(_gdn_kernel, bb=bb, c=c, nv=nv),
        grid=(b // bb, n),
        in_specs=[_seq_spec(bb, c, GDN_QKV), _seq_spec(bb, c, GDN_W), _seq_spec(bb, c, LANES),
                  pl.BlockSpec((bb, 1, SUBLANES, c), lambda b, i: (b, i, 0, 0)),
                  _state_spec(bb, (CONV_W - 1, GDN_QKV)),
                  _layer_state_spec(bb, (GDN_H, GDN_DK, GDN_DV), layer),
                  _const_spec((CONV_W, GDN_QKV)), _const_spec((2, GDN_H)), _const_spec((GDN_H, 2)),
                  _const_spec((1, GDN_DV))],
        out_specs=[_seq_spec(bb, c, GDN_W), _state_spec(bb, (GDN_H, GDN_DK, GDN_DV)),
                   _state_spec(bb, (CONV_W - 1, GDN_QKV))],
        out_shape=[jax.ShapeDtypeStruct((b, n * c, GDN_W), F32),
                   jax.ShapeDtypeStruct((b, GDN_H, GDN_DK, GDN_DV), F32),
                   jax.ShapeDtypeStruct((b, CONV_W - 1, GDN_QKV), F32)],
        scratch_shapes=[pltpu.VMEM((bb, SUBLANES + c, GDN_QKV), F32),
                        pltpu.VMEM((bb, GDN_H, GDN_DK, GDN_DV), F32)],
        compiler_params=_cparams("parallel", "arbitrary"),
        name="gdn",
    )(_pad_seq(qkv, c, n), _pad_seq(z, c, n), gates, _rows_layout(gates, c, n, SUBLANES), buf, s0,
      conv_w, pcol, pcol.T, norm_w.reshape(1, GDN_DV))
    return out[:, :l], s_new, buf_new


def _rwkv_kernel(rw_ref, prev_ref, s0_ref, mix_ref, w0_ref, w2_ref, a0_ref, a2_ref, g2_ref,
                 kk_ref, ka_ref, rk_ref, lnw_ref, lnb_ref, out_ref, snew_ref, xs_scr, s_scr,
                 *, bb, c, nv):
    i = pl.program_id(1)
    hd = RWKV_HD
    w = RWKV_W

    @pl.when(i == 0)
    def _():
        xs_scr[:, 0:SUBLANES, :] = jnp.zeros((bb, SUBLANES, RWKV_PROJ_W), F32)
        xs_scr[:, 7:8, :] = prev_ref[...]
        s_scr[...] = s0_ref[...]

    valid = _valid_rows(c, nv)
    incl = _tri(c, True)
    heads, post = [], []
    for bi in range(bb):
        rw = rw_ref[bi]
        xs_scr[bi, SUBLANES:SUBLANES + c, :] = rw
        prev = xs_scr[bi, 7:7 + c, :]
        xs_scr[bi, 7:8, :] = xs_scr[bi, 7 + nv:8 + nv, :]
        xs = rw + (prev - rw) * mix_ref[...]
        r_all, k_all, v_all = xs[:, 0:w], xs[:, w:2 * w], xs[:, 2 * w:3 * w]
        o1 = 3 * w
        wl = xs[:, o1:o1 + RWKV_RW]
        al = xs[:, o1 + RWKV_RW:o1 + RWKV_RW + RWKV_RA]
        gl = xs[:, o1 + RWKV_RW + RWKV_RA:]
        wdec = -_softplus(-(w0_ref[...] + _dot(jnp.tanh(wl), w2_ref[...]))) - 0.5
        a_all = _sigmoid(a0_ref[...] + _dot(al, a2_ref[...]))
        gate = _dot(_sigmoid(gl), g2_ref[...])
        lw_all = jnp.where(valid, -jnp.exp(wdec), 0.0)
        kmod = jnp.where(valid, k_all * (1.0 + (a_all - 1.0) * ka_ref[...]), 0.0)
        kk_all = k_all * kk_ref[...]
        gi = _dot_sel_lhs(incl, lw_all)
        gm = gi[c // 2:c // 2 + 1, :]
        g_end = gi[c - 1:c, :]
        e_nlw = jnp.exp(-lw_all)
        e_r = jnp.exp(gi - gm)
        e_a = e_r * e_nlw
        e_m = jnp.exp(gm - gi)
        e_gi = jnp.exp(gi)
        e_gx = e_gi * e_nlw
        e_end = jnp.exp(g_end - gm) * e_m
        s_dec = jnp.exp(g_end)
        for h in range(RWKV_H):
            sl = slice(h * hd, (h + 1) * hd)
            kk = _l2norm(kk_all[:, sl])
            r, k = r_all[:, sl], kmod[:, sl]
            a = -kk
            b = jnp.where(valid, kk * a_all[:, sl], 0.0)
            heads.append(dict(
                m_lhs=jnp.concatenate([a * e_a[:, sl], r * e_r[:, sl]], axis=0),
                m_rhs=jnp.concatenate([b * e_m[:, sl], k * e_m[:, sl]], axis=0),
                x_lhs=jnp.concatenate([a * e_gx[:, sl], r * e_gi[:, sl]], axis=0),
                v=v_all[:, sl],
                bk=jnp.concatenate([b * e_end[:, sl], k * e_end[:, sl]], axis=0), s_decay=s_dec[:, sl]))
            post.append((bi, h, r, k, v_all[:, sl], gate[:, sl]))
    outs, new_states = _dplr_heads(heads, [s_scr[bi, h] for bi, h, *_ in post], c, nv, state_is_vk=True)
    for (bi, h, r, k, v, gate_h), o, s_new in zip(post, outs, new_states):
        sl = slice(h * hd, (h + 1) * hd)
        s_scr[bi, h] = s_new
        y = _ln(o, lnw_ref[:, sl], lnb_ref[:, sl], RWKV_GN_EPS)
        y = y + jnp.sum(r * k * rk_ref[:, sl], axis=-1, keepdims=True) * v
        out_ref[bi, :, sl] = y * gate_h

    @pl.when(i == pl.num_programs(1) - 1)
    def _():
        snew_ref[...] = s_scr[...]


def _rwkv_mixer(rw, shift0, s0, layer, mix, w0, w2, a0, a2, g2, k_k, k_a, r_k, ln_w, ln_b):
    b, l, _ = rw.shape
    c, n, nv, bb = _seq_dims(b, l, RWKV_SEQS_PER_STEP)
    w = RWKV_W
    row = lambda t: t.reshape(1, -1)
    state = (RWKV_H, RWKV_HD, RWKV_HD)
    out, s_new = pl.pallas_call(
        functools.partial(_rwkv_kernel, bb=bb, c=c, nv=nv),
        grid=(b // bb, n),
        in_specs=[_seq_spec(bb, c, RWKV_PROJ_W), _state_spec(bb, (1, RWKV_PROJ_W)),
                  _layer_state_spec(bb, state, layer),
                  _const_spec((1, RWKV_PROJ_W)), _const_spec((1, w)), _const_spec((RWKV_RW, w)),
                  _const_spec((1, w)), _const_spec((RWKV_RA, w)), _const_spec((RWKV_RG, w)),
                  _const_spec((1, w)), _const_spec((1, w)), _const_spec((1, w)), _const_spec((1, w)),
                  _const_spec((1, w))],
        out_specs=[_seq_spec(bb, c, w), _state_spec(bb, state)],
        out_shape=[jax.ShapeDtypeStruct((b, n * c, w), F32),
                   jax.ShapeDtypeStruct((b,) + state, F32)],
        scratch_shapes=[pltpu.VMEM((bb, SUBLANES + c, RWKV_PROJ_W), F32), pltpu.VMEM((bb,) + state, F32)],
        compiler_params=_cparams("parallel", "arbitrary"),
        name="rwkv7",
    )(_pad_seq(rw, c, n), shift0.reshape(b, 1, RWKV_PROJ_W), s0, row(mix), row(w0),
      w2.astype(BF16), row(a0), a2.astype(BF16), g2.astype(BF16), row(k_k), row(k_a), row(r_k),
      row(ln_w), row(ln_b))
    return out[:, :l], s_new


def _mlstm_kernel(p_ref, gcol_ref, grow_ref, c0_ref, n0_ref, m0_ref, bcol_ref, brow_ref, nw_ref,
                  out_ref, cnew_ref, nnew_ref, mnew_ref, c_scr, n_scr, m_scr, *, bb, c, nv):
    i = pl.program_id(1)
    nh, dk, dv = MLSTM_H, MLSTM_DK, MLSTM_DV

    @pl.when(i == 0)
    def _():
        c_scr[...] = c0_ref[...]
        n_scr[...] = n0_ref[...]
        m_scr[...] = m0_ref[...]

    valid = _valid_rows(c, nv)
    valid_r = lax.broadcasted_iota(jnp.int32, (1, c), 1) < nv
    incl = _tri(c, True)
    upper = jnp.logical_not(_tri(c, False))
    ps = [(bi, h) for bi in range(bb) for h in range(nh)]
    gate_cols = {}
    for bi in range(bb):
        gcol = gcol_ref[bi]
        grow = grow_ref[bi, 0]
        li_col = jnp.where(valid, gcol[:, 0:nh] + bcol_ref[0:1, :], NEG_BIG)
        lf_col = jnp.where(valid, -_softplus(-(gcol[:, nh:2 * nh] + bcol_ref[1:2, :])), 0.0)
        li_row = jnp.where(valid_r, grow[0:nh, :] + brow_ref[:, 0:1], NEG_BIG)
        lf_row = jnp.where(valid_r, -_softplus(-(grow[nh:2 * nh, :] + brow_ref[:, 1:2])), 0.0)
        b_cols = _dot_sel_lhs(incl, lf_col)
        b_rows = _dot_sel_rhs(lf_row, upper)
        gate_cols[bi] = (li_col, li_row, b_cols, b_rows)
    qs = [p_ref[bi, :, h * dk:(h + 1) * dk] for bi, h in ps]
    ks = [p_ref[bi, :, nh * dk + h * dk:nh * dk + (h + 1) * dk] * (dk ** -0.5) for bi, h in ps]
    vs = [p_ref[bi, :, 2 * nh * dk + h * dv:2 * nh * dk + (h + 1) * dv] for bi, h in ps]
    cms = [c_scr[bi, h] for bi, h in ps]
    nvecs = [n_scr[bi, h] for bi, h in ps]
    m_prevs = [m_scr[bi, h] for bi, h in ps]
    qks = [_dot_nt(q, k) for q, k in zip(qs, ks)]
    qcs = [_dot_nt(q, cm) for q, cm in zip(qs, cms)]
    bcs = [gate_cols[bi][2][:, h:h + 1] for bi, h in ps]
    dms = [jnp.where(incl, bc - gate_cols[bi][3][h:h + 1, :] + gate_cols[bi][1][h:h + 1, :], -jnp.inf)
           for (bi, h), bc in zip(ps, bcs)]
    m_inters = [bc + m_prev for bc, m_prev in zip(bcs, m_prevs)]
    m_ts = [jnp.maximum(mi, jnp.max(dm, axis=-1, keepdims=True)) for mi, dm in zip(m_inters, dms)]
    w_inters = [jnp.exp(mi - mt) for mi, mt in zip(m_inters, m_ts)]
    scs = [qk * jnp.exp(dm - mt) for qk, dm, mt in zip(qks, dms, m_ts)]
    scvs = [_dot(sc, v) for sc, v in zip(scs, vs)]
    m_news = [mt[c - 1:c, :] for mt in m_ts]
    b_lasts = [bc[c - 1:c, :] for bc in bcs]
    wss = [jnp.exp(b_last - bc + gate_cols[bi][0][:, h:h + 1] - m_new)
           for (bi, h), b_last, bc, m_new in zip(ps, b_lasts, bcs, m_news)]
    upds = [_dot_tn(v * ws, k) for v, ws, k in zip(vs, wss, ks)]
    for j, (bi, h) in enumerate(ps):
        num = w_inters[j] * qcs[j] + scvs[j]
        den = (w_inters[j] * jnp.sum(qs[j] * nvecs[j], axis=-1, keepdims=True)
               + jnp.sum(scs[j], axis=-1, keepdims=True))
        hh = num / jnp.maximum(jnp.abs(den), jnp.exp(-m_ts[j]))
        dec = jnp.exp(b_lasts[j] + m_prevs[j] - m_news[j])
        c_scr[bi, h] = dec * cms[j] + upds[j]
        n_scr[bi, h] = dec * nvecs[j] + jnp.sum(ks[j] * wss[j], axis=0, keepdims=True)
        m_scr[bi, h] = m_news[j]
        og = p_ref[bi, :, 2 * nh * dk + nh * dv + h * dv:2 * nh * dk + nh * dv + (h + 1) * dv]
        out_ref[bi, :, h * dv:(h + 1) * dv] = _ln(hh, nw_ref[...]) * _sigmoid(og)

    @pl.when(i == pl.num_programs(1) - 1)
    def _():
        cnew_ref[...] = c_scr[...]
        nnew_ref[...] = n_scr[...]
        mnew_ref[...] = m_scr[...]


def _mlstm_mixer(p, gates, c0, layer, n0, m0, i_b, f_b, norm_w):
    b, l, _ = p.shape
    c, n, nv, bb = _seq_dims(b, l, MLSTM_SEQS_PER_STEP, MLSTM_CHUNK)
    nh, dk, dv = MLSTM_H, MLSTM_DK, MLSTM_DV
    gates = _pad_seq(gates, c, n)
    bcol = jnp.stack([i_b, f_b])
    out, c_new, n_new, m_new = pl.pallas_call(
        functools.partial(_mlstm_kernel, bb=bb, c=c, nv=nv),
        grid=(b // bb, n),
        in_specs=[_seq_spec(bb, c, p.shape[2]), _seq_spec(bb, c, LANES),
                  pl.BlockSpec((bb, 1, SUBLANES, c), lambda b, i: (b, i, 0, 0)),
                  _layer_state_spec(bb, (nh, dv, dk), layer), _state_spec(bb, (nh, 1, dk)),
                  _state_spec(bb, (nh, 1, 1)), _const_spec((2, nh)), _const_spec((nh, 2)), _const_spec((1, dv))],
        out_specs=[_seq_spec(bb, c, nh * dv), _state_spec(bb, (nh, dv, dk)),
                   _state_spec(bb, (nh, 1, dk)), _state_spec(bb, (nh, 1, 1))],
        out_shape=[jax.ShapeDtypeStruct((b, n * c, nh * dv), F32),
                   jax.ShapeDtypeStruct((b, nh, dv, dk), F32),
                   jax.ShapeDtypeStruct((b, nh, 1, dk), F32),
                   jax.ShapeDtypeStruct((b, nh, 1, 1), F32)],
        scratch_shapes=[pltpu.VMEM((bb, nh, dv, dk), F32), pltpu.VMEM((bb, nh, 1, dk), F32),
                        pltpu.VMEM((bb, nh, 1, 1), F32)],
        compiler_params=_cparams("parallel", "arbitrary"),
        name="mlstm",
    )(_pad_seq(p, c, n), gates, _rows_layout(gates, c, n, SUBLANES), c0,
      n0.reshape(b, nh, 1, dk), m0.reshape(b, nh, 1, 1), bcol, bcol.T, norm_w.reshape(1, dv))
    return out[:, :l], c_new, n_new.reshape(b, nh, dk), m_new.reshape(b, nh)


def _pad_cols(w, n):
    return jnp.pad(w, ((0, 0), (0, n - w.shape[1])))


def _ab_in_weight(w):
    o = 2 * LRU_W
    xy, qkv = w[:, :o], w[:, o:o + GDN_QKV]
    o += GDN_QKV
    ab, z = w[:, o:o + 2 * GDN_H], w[:, o + 2 * GDN_H:]
    return jnp.concatenate([xy, qkv, z, _pad_cols(ab, LANES)], axis=1).astype(BF16)


AB_SPLITS = ((0, 2 * LRU_W), (2 * LRU_W, GDN_QKV), (2 * LRU_W + GDN_QKV, GDN_W),
             (2 * LRU_W + GDN_QKV + GDN_W, LANES))


def _cd_in_weight(w):
    o = RWKV_PROJ_W
    rw, qkv = w[:, :o], w[:, o:o + 2 * MLSTM_QK + MLSTM_W]
    o += 2 * MLSTM_QK + MLSTM_W
    gates, og = w[:, o:o + 2 * MLSTM_H], w[:, o + 2 * MLSTM_H:]
    return jnp.concatenate([rw, qkv, og, _pad_cols(gates, LANES)], axis=1).astype(BF16)


CD_SPLITS = ((0, RWKV_PROJ_W), (RWKV_PROJ_W, 2 * MLSTM_QK + 2 * MLSTM_W),
             (RWKV_PROJ_W + 2 * MLSTM_QK + 2 * MLSTM_W, LANES))


def kernel(x_prompt, x_sample, c_prompt, c_sample,
           state_lru_h, state_lru_conv, state_gdn_S, state_gdn_conv,
           state_rwkv_S, state_rwkv_shift, state_mlstm_C, state_mlstm_n, state_mlstm_m,
           mod_w, mod_b, ln1_g, ln1_b, ln2_g, ln2_b,
           ab_w_in, ab_w_out, lru_conv_w, lru_conv_b, lru_wr, lru_br, lru_wi, lru_bi, lru_lambda,
           gdn_conv_w, gdn_a_log, gdn_dt_bias, gdn_norm_w,
           cd_w_in, cd_w_out, rwkv_mix, rwkv_w0, rwkv_w2, rwkv_a0, rwkv_a2, rwkv_g2,
           rwkv_k_k, rwkv_k_a, rwkv_r_k, rwkv_ln_w, rwkv_ln_b,
           mlstm_i_b, mlstm_f_b, mlstm_norm_w,
           ffn_w_gate, ffn_w_up, ffn_w_down,
           router_w, moe_w_gate, moe_w_up, moe_w_down):
    d = D_MODEL
    bp, lp, _ = x_prompt.shape
    bs, ls, _ = x_sample.shape
    mod = _modulation(jnp.concatenate([c_prompt, c_sample], axis=0), mod_w, mod_b)

    ab_in = [_ab_in_weight(ab_w_in[j]) for j in range(ab_w_in.shape[0])]
    cd_in = [_cd_in_weight(cd_w_in[j]) for j in range(cd_w_in.shape[0])]
    ab_out, cd_out = ab_w_out.astype(BF16), cd_w_out.astype(BF16)
    ffn_g, ffn_u, ffn_d = (t.astype(BF16) for t in (ffn_w_gate, ffn_w_up, ffn_w_down))

    def trunk(x, mods, batch, length, states, pos0):
        lru_h, lru_conv, gdn_s, gdn_conv, rwkv_s, rwkv_shift, m_c, m_n, m_m = states
        new = [[] for _ in range(9)]
        seq = lambda t: t.reshape(batch, length, t.shape[-1])
        tok = lambda t: t.reshape(x.shape[0], x.shape[1], t.shape[-1])
        for l in range(DEPTH):
            j = l // 2
            sh1, sc1, g1, sh2, sc2, g2 = mods[l]
            if l % 2 == 0:
                xy, qkv, z, gates = _inproj(x, sc1, sh1, ab_in[j], AB_SPLITS)
                out_a, s0, s1 = _lru_mixer(seq(xy), lru_conv[j], lru_h[j], lru_conv_w[j], lru_conv_b[j],
                                           lru_wr[j], lru_br[j], lru_wi[j], lru_bi[j], lru_lambda[j], pos0)
                out_b, s2, s3 = _gdn_mixer(seq(qkv), seq(z), seq(gates), gdn_conv[j], gdn_s, j,
                                           gdn_conv_w[j], gdn_a_log[j], gdn_dt_bias[j], gdn_norm_w[j])
                for slot, s in zip((0, 1, 2, 3), (s0, s1, s2, s3)):
                    new[slot].append(s)
                x = _outproj_ln(x, tok(out_a), tok(out_b), g1, ab_out, j, ln1_g[l], ln1_b[l])
                x = _ffn_ln(x, sc2, sh2, g2, ffn_g, ffn_u, ffn_d, j, ln2_g[l], ln2_b[l])
            else:
                rw, mp, gates = _inproj(x, sc1, sh1, cd_in[j], CD_SPLITS)
                out_c, s0 = _rwkv_mixer(seq(rw), rwkv_shift[j], rwkv_s, j, rwkv_mix[j], rwkv_w0[j],
                                        rwkv_w2[j], rwkv_a0[j], rwkv_a2[j], rwkv_g2[j], rwkv_k_k[j],
                                        rwkv_k_a[j], rwkv_r_k[j], rwkv_ln_w[j], rwkv_ln_b[j])
                out_d, s2, s3, s4 = _mlstm_mixer(seq(mp), seq(gates), m_c, j, m_n[j], m_m[j],
                                                 mlstm_i_b[j], mlstm_f_b[j], mlstm_norm_w[j])
                for slot, s in zip((4, 5, 6, 7, 8), (s0, seq(rw)[:, -1], s2, s3, s4)):
                    new[slot].append(s)
                x = _outproj_ln(x, tok(out_c), tok(out_d), g1, cd_out, j, ln1_g[l], ln1_b[l])
                moe = _moe_sparse_ln if x.shape[1] % MOE_TOKEN_BLOCK == 0 else _moe_ln
                x = moe(x, sc2, sh2, g2, router_w[j], moe_w_gate, moe_w_up, moe_w_down, j,
                        ln2_g[l], ln2_b[l])
        return x, tuple(jnp.stack(s) for s in new)

    def zeros(ref):
        return jnp.zeros((ref.shape[0], bp) + ref.shape[2:], F32)

    mods_p = [[mod[l, :bp, k * d:(k + 1) * d].reshape(bp, 1, d) for k in range(6)] for l in range(DEPTH)]
    mods_s = [[mod[l, bp:, k * d:(k + 1) * d].reshape(1, bs * ls, d) for k in range(6)] for l in range(DEPTH)]
    states_s = (state_lru_h, state_lru_conv, state_gdn_S, state_gdn_conv, state_rwkv_S,
                state_rwkv_shift, state_mlstm_C, state_mlstm_n, state_mlstm_m)
    y_p, new_p = trunk(x_prompt, mods_p, bp, lp, tuple(zeros(s) for s in states_s), 0)
    y_s, new_s = trunk(x_sample.reshape(1, bs * ls, d), mods_s, bs, ls, states_s, PAST_LEN)
    out = [y_p, y_s.reshape(bs, ls, d)]
    for p_leaf, s_leaf in zip(new_p, new_s):
        out += [p_leaf, s_leaf]
    return tuple(out)
```
